```python
import math
import jax
import jax.numpy as jnp
from jax import lax
import numpy as np

D_MODEL = 1024
BATCH = 4
SEQ = 4096
DEPTH = 1

HG_HEADS = 8
HG_DK = 128
HG_DV = 128
HG_WIDTH = HG_HEADS * HG_DK
HG_VWIDTH = HG_HEADS * HG_DV
HG_CHUNK = 64
NSA_HEADS = 16
NSA_KV = 4
NSA_GROUP = NSA_HEADS // NSA_KV
NSA_DH = 64
NSA_WIDTH = NSA_HEADS * NSA_DH
NSA_KVW = NSA_KV * NSA_DH
CMP_LEN = 32
CMP_STRIDE = 16
CMP_HIDDEN = 256
SEL_BLOCK = 64
SEL_TOPK = 16
WINDOW = 512
Q_BLOCK = 64
ROPE_THETA = 500000.0
ROPE_DIM = NSA_DH // 4
D_FF = 4 * D_MODEL
PLE_DIM = 256
EPS = 1e-6
NEG = -1e30

IN_SIZES = [HG_WIDTH, HG_WIDTH, HG_VWIDTH, HG_VWIDTH,
            NSA_WIDTH, 6 * NSA_KVW, 3 * NSA_HEADS,
            D_MODEL, D_MODEL]
N_IN = int(sum(IN_SIZES))
SPLIT_POINTS = [int(v) for v in np.cumsum(IN_SIZES)[:-1]]

kernel_name = "hybrid_hgrn2_nsa_sandwich_block"


def rms_norm(x, w):
    xf = x.astype(jnp.float32)
    y = xf * lax.rsqrt(jnp.mean(xf * xf, axis=-1, keepdims=True) + EPS)
    return y.astype(x.dtype) * w


def masked_softmax(s, mask):
    s = jnp.where(mask, s.astype(jnp.float32), NEG)
    pr = jax.nn.softmax(s, axis=-1)
    return jnp.where(mask, pr, 0.0)


def rope_tables(T):
    half = ROPE_DIM // 2
    inv = jnp.asarray(ROPE_THETA ** (-np.arange(half) * 2.0 / ROPE_DIM), jnp.float32)
    ang = jnp.arange(T, dtype=jnp.float32)[:, None] * inv[None, :]
    return jnp.cos(ang), jnp.sin(ang)


def apply_partial_rope(x, cos, sin):
    half = ROPE_DIM // 2
    extra = x.ndim - 3
    shp = (cos.shape[0],) + (1,) * extra + (half,)
    c = cos.reshape(shp).astype(x.dtype)
    s = sin.reshape(shp).astype(x.dtype)
    x1 = x[..., :half]
    x2 = x[..., half:ROPE_DIM]
    rest = x[..., ROPE_DIM:]
    return jnp.concatenate([x1 * c - x2 * s, x2 * c + x1 * s, rest], axis=-1)


def hgrn2_mixer(q, f_pre, i_in, g, lb, gnorm_w):
    B, T, _ = q.shape
    n = T // HG_CHUNK
    f32 = jnp.float32
    qf = jax.nn.silu(q.astype(f32)) * (HG_DK ** -0.5)
    forget = lb + (1.0 - lb) * jax.nn.sigmoid(f_pre.astype(f32))
    k = 1.0 - forget
    logf = jnp.log(forget)

    def heads(t, d):
        return t.reshape(B, n, HG_CHUNK, HG_HEADS, d).transpose(1, 0, 3, 2, 4)

    qc, kc, lc = heads(qf, HG_DK), heads(k, HG_DK), heads(logf, HG_DK)
    vc = heads(i_in.astype(f32), HG_DV)
    causal = jnp.tril(jnp.ones((HG_CHUNK, HG_CHUNK), dtype=bool))

    def step(S, inp):
        qj, kj, vj, lj = inp
        b = jnp.cumsum(lj, axis=2)
        o_inter = jnp.einsum('bhtk,bhkv->bhtv', qj * jnp.exp(b), S)
        rel = b[:, :, :, None, :] - b[:, :, None, :, :]
        decay = jnp.exp(jnp.where(causal[:, :, None], rel, -jnp.inf))
        A = jnp.einsum('bhtk,bhsk,bhtsk->bhts', qj, kj, decay)
        o = o_inter + jnp.einsum('bhts,bhsv->bhtv', A, vj)
        b_last = b[:, :, -1:, :]
        S = jnp.exp(b_last[:, :, 0, :, None]) * S + jnp.einsum(
            'bhsk,bhsv->bhkv', kj * jnp.exp(b_last - b), vj)
        return S, o

    S0 = jnp.zeros((B, HG_HEADS, HG_DK, HG_DV), f32)
    _, o = lax.scan(step, S0, (qc, kc, vc, lc))
    o = o.transpose(1, 0, 3, 2, 4).reshape(B, T, HG_HEADS, HG_DV)
    gh = g.astype(f32).reshape(B, T, HG_HEADS, HG_DV)
    o = rms_norm(o, gnorm_w.astype(f32)) * jax.nn.silu(gh)
    return o.reshape(B, T, HG_VWIDTH).astype(q.dtype)


def nsa_mixer(q, kc, vc, ks, vs, kw, vw, gates, pe_k, pe_v, wk1, wk2, wv1, wv2):
    B, T, _ = q.shape
    G, R, Dh = NSA_KV, NSA_GROUP, NSA_DH
    scale = Dh ** -0.5
    cos, sin = rope_tables(T)
    q = q.reshape(B, T, G, R, Dh) * scale
    q_rot = apply_partial_rope(q, cos, sin)
    kvh = lambda t: t.reshape(B, T, G, Dh)
    ks = apply_partial_rope(kvh(ks), cos, sin)
    kw = apply_partial_rope(kvh(kw), cos, sin)

    n_cmp = (T - CMP_LEN) // CMP_STRIDE + 1
    blk_idx = np.arange(n_cmp)[:, None] * CMP_STRIDE + np.arange(CMP_LEN)[None, :]
    cmp_end = blk_idx[:, -1]

    def compress(t, pe, w1, w2):
        blocks = kvh(t)[:, blk_idx] + pe[:, None, :]
        flat = blocks.transpose(0, 1, 3, 2, 4).reshape(B, n_cmp, G, CMP_LEN * Dh)
        out = jax.nn.gelu(flat @ w1) @ w2
        return out.transpose(0, 2, 1, 3)

    k_cmp = compress(kc, pe_k, wk1, wk2)
    v_cmp = compress(vc, pe_v, wv1, wv2)

    n_sel = T // SEL_BLOCK
    top = min(SEL_TOPK, n_sel)
    sel_start = np.arange(n_sel) * SEL_BLOCK
    overlap = jnp.asarray(((blk_idx[:, :1] < sel_start[None, :] + SEL_BLOCK)
                           & (cmp_end[:, None] >= sel_start[None, :])).astype(np.float32))
    ks_blocks = ks.transpose(0, 2, 1, 3).reshape(B, G, n_sel, SEL_BLOCK * Dh)
    vs_blocks = kvh(vs).transpose(0, 2, 1, 3).reshape(B, G, n_sel, SEL_BLOCK * Dh)
    sel_ids = jnp.arange(n_sel)

    pad = ((0, 0), (0, 0), (WINDOW, 0), (0, 0))
    kw_pad = jnp.pad(kw.transpose(0, 2, 1, 3), pad)
    vw_pad = jnp.pad(kvh(vw).transpose(0, 2, 1, 3), pad)

    q_all = q.transpose(0, 2, 3, 1, 4)
    qr_all = q_rot.transpose(0, 2, 3, 1, 4)

    def block(c):
        s0 = c * Q_BLOCK
        t_pos = s0 + jnp.arange(Q_BLOCK)
        qb = lax.dynamic_slice_in_dim(q_all, s0, Q_BLOCK, axis=3)
        qr = lax.dynamic_slice_in_dim(qr_all, s0, Q_BLOCK, axis=3)
        s = jnp.einsum('bgrqd,bgnd->bgrqn', qb, k_cmp)
        p_cmp = masked_softmax(s, cmp_end[None, :] <= t_pos[:, None])
        o_cmp = jnp.einsum('bgrqn,bgnd->bgrqd', p_cmp, v_cmp)
        imp = jnp.einsum('bgrqn,ns->bgqs', p_cmp, overlap)
        cur = t_pos // SEL_BLOCK
        forced = ((sel_ids[None, :] == 0) | (sel_ids[None, :] == cur[:, None])
                  | (sel_ids[None, :] == cur[:, None] - 1))
        causal_blk = sel_ids[None, :] * SEL_BLOCK <= t_pos[:, None]
        imp = jnp.where(forced, jnp.inf, imp)
        imp = jnp.where(causal_blk, imp, -jnp.inf)
        _, idx = lax.top_k(imp, top)
        flat_idx = idx.reshape(B, G, Q_BLOCK * top, 1)
        kg = jnp.take_along_axis(ks_blocks, flat_idx, axis=2).reshape(
            B, G, Q_BLOCK, top * SEL_BLOCK, Dh)
        vg = jnp.take_along_axis(vs_blocks, flat_idx, axis=2).reshape(
            B, G, Q_BLOCK, top * SEL_BLOCK, Dh)
        key_pos = (idx[..., None] * SEL_BLOCK + jnp.arange(SEL_BLOCK)).reshape(
            B, G, Q_BLOCK, top * SEL_BLOCK)
        m_sel = (key_pos <= t_pos[None, None, :, None])[:, :, None]
        s = jnp.einsum('bgrqd,bgqkd->bgrqk', qr, kg)
        o_sel = jnp.einsum('bgrqk,bgqkd->bgrqd', masked_softmax(s, m_sel), vg)
        kwb = lax.dynamic_slice_in_dim(kw_pad, s0, WINDOW + Q_BLOCK, axis=2)
        vwb = lax.dynamic_slice_in_dim(vw_pad, s0, WINDOW + Q_BLOCK, axis=2)
        wpos = s0 - WINDOW + jnp.arange(WINDOW + Q_BLOCK)
        diff = t_pos[:, None] - wpos[None, :]
        m_win = (diff >= 0) & (diff < WINDOW) & (wpos[None, :] >= 0)
        s = jnp.einsum('bgrqd,bgkd->bgrqk', qr, kwb)
        o_win = jnp.einsum('bgrqk,bgkd->bgrqd', masked_softmax(s, m_win), vwb)
        return o_cmp, o_sel, o_win

    o_cmp, o_sel, o_win = lax.map(block, jnp.arange(T // Q_BLOCK))
    to_bthd = lambda o: o.transpose(1, 0, 4, 2, 3, 5).reshape(B, T, NSA_HEADS, Dh)
    gt = jax.nn.sigmoid(gates.astype(jnp.float32)).reshape(B, T, 3, NSA_HEADS, 1)
    out = (gt[:, :, 0] * to_bthd(o_cmp) + gt[:, :, 1] * to_bthd(o_sel)
           + gt[:, :, 2] * to_bthd(o_win))
    return out.reshape(B, T, NSA_WIDTH).astype(q.dtype)


def setup_inputs(seed: int = 0) -> dict:
    key = jax.random.key(seed)
    k = jax.random.split(key, 24)
    f32 = jnp.float32
    nrm = lambda kk, shape, sc: jax.random.normal(kk, shape, f32) * sc
    gain = lambda kk, n: 1.0 + 0.05 * jax.random.normal(kk, (DEPTH, n), f32)
    return {
        "x": nrm(k[0], (BATCH, SEQ, D_MODEL), 1.0),
        "p": nrm(k[1], (DEPTH, BATCH, SEQ, PLE_DIM), 1.0),
        "w_in": nrm(k[2], (DEPTH, D_MODEL, N_IN), D_MODEL ** -0.5),
        "w_branch_a": nrm(k[3], (DEPTH, HG_VWIDTH, D_MODEL), HG_VWIDTH ** -0.5),
        "w_branch_b": nrm(k[4], (DEPTH, NSA_WIDTH, D_MODEL), NSA_WIDTH ** -0.5),
        "w_out": nrm(k[5], (DEPTH, D_MODEL, D_MODEL), D_MODEL ** -0.5),
        "norm_pre_mix": gain(k[6], D_MODEL),
        "norm_post_mix": gain(k[7], D_MODEL),
        "norm_pre_mlp": gain(k[8], D_MODEL),
        "norm_post_mlp": gain(k[9], D_MODEL),
        "hg_lb_logits": nrm(k[10], (DEPTH + 1, HG_WIDTH), 0.5),
        "hg_gnorm": gain(k[11], HG_DV),
        "cmp_pe_k": nrm(k[12], (DEPTH, CMP_LEN, NSA_DH), 0.1),
        "cmp_pe_v": nrm(k[13], (DEPTH, CMP_LEN, NSA_DH), 0.1),
        "cmp_wk1": nrm(k[14], (DEPTH, CMP_LEN * NSA_DH, CMP_HIDDEN), (CMP_LEN * NSA_DH) ** -0.5),
        "cmp_wk2": nrm(k[15], (DEPTH, CMP_HIDDEN, NSA_DH), CMP_HIDDEN ** -0.5),
        "cmp_wv1": nrm(k[16], (DEPTH, CMP_LEN * NSA_DH, CMP_HIDDEN), (CMP_LEN * NSA_DH) ** -0.5),
        "cmp_wv2": nrm(k[17], (DEPTH, CMP_HIDDEN, NSA_DH), CMP_HIDDEN ** -0.5),
        "w_up": nrm(k[18], (DEPTH, D_MODEL, D_FF), D_MODEL ** -0.5),
        "w_down": nrm(k[19], (DEPTH, D_FF, D_MODEL), D_FF ** -0.5),
        "w_ple": nrm(k[20], (DEPTH, PLE_DIM, D_MODEL), PLE_DIM ** -0.5),
        "w_ple_gate": nrm(k[21], (DEPTH, D_MODEL, D_MODEL), D_MODEL ** -0.5),
        "norm_ple": gain(k[22], D_MODEL),
    }


def reference(x, p, w_in, w_branch_a, w_branch_b, w_out, norm_pre_mix, norm_post_mix,
              norm_pre_mlp, norm_post_mlp, hg_lb_logits, hg_gnorm, cmp_pe_k, cmp_pe_v,
              cmp_wk1, cmp_wk2, cmp_wv1, cmp_wv2, w_up, w_down, w_ple, w_ple_gate,
              norm_ple):
    lower_bounds = jnp.cumsum(jax.nn.softmax(hg_lb_logits.astype(jnp.float32), axis=0), axis=0)
    h = x
    for i in range(DEPTH):
        u = rms_norm(h, norm_pre_mix[i])
        proj = u @ w_in[i]
        hq, hf, hi, hg, nq, nkv, ngate, ga, gb = jnp.split(proj, SPLIT_POINTS, axis=-1)
        kc, vc, ks, vs, kw, vw = jnp.split(nkv, 6, axis=-1)
        y_a = hgrn2_mixer(hq, hf, hi, hg, lower_bounds[i], hg_gnorm[i])
        y_b = nsa_mixer(nq, kc, vc, ks, vs, kw, vw, ngate, cmp_pe_k[i], cmp_pe_v[i],
                        cmp_wk1[i], cmp_wk2[i], cmp_wv1[i], cmp_wv2[i])
        merged = (jax.nn.sigmoid(ga) * (y_a @ w_branch_a[i])
                  + jax.nn.sigmoid(gb) * (y_b @ w_branch_b[i]))
        h = h + rms_norm(merged @ w_out[i], norm_post_mix[i])
        v = rms_norm(h, norm_pre_mlp[i])
        ff = jnp.square(jax.nn.relu(v @ w_up[i])) @ w_down[i]
        h = h + rms_norm(ff, norm_post_mlp[i])
        e = (p[i] @ w_ple[i]) * jax.nn.sigmoid(h @ w_ple_gate[i])
        h = h + rms_norm(e, norm_ple[i])
    return h
```

```python
import functools

import numpy as np
import jax
import jax.numpy as jnp
from jax import lax
from jax.experimental import pallas as pl
from jax.experimental.pallas import tpu as pltpu

F32 = jnp.float32
BF16 = jnp.bfloat16

D_MODEL = 1024
HG_HEADS = 8
HG_DK = 128
HG_DV = 128
HG_WIDTH = HG_HEADS * HG_DK
HG_CHUNK = 64
NSA_HEADS = 16
NSA_KV = 4
NSA_GROUP = NSA_HEADS // NSA_KV
NSA_DH = 64
NSA_WIDTH = NSA_HEADS * NSA_DH
NSA_KVW = NSA_KV * NSA_DH
CMP_LEN = 32
CMP_STRIDE = 16
CMP_HIDDEN = 256
SEL_BLOCK = 64
SEL_TOPK = 16
WINDOW = 512
ROPE_THETA = 500000.0
ROPE_DIM = NSA_DH // 4
D_FF = 4 * D_MODEL
EPS = 1e-6
NEG = -1e30
ATT_SCALE = NSA_DH ** -0.5

LANES = 128
VMEM_LIMIT = 56 * 1024 * 1024

TOK_TILE = 512
ATT_Q = 128
ATT_KT = 512
HG_TOK = 256
HG_EXP_CLAMP = 80.0


def _cparams(sem):
    return pltpu.CompilerParams(dimension_semantics=sem, vmem_limit_bytes=VMEM_LIMIT)


def _rms(x, gain):
    ms = jnp.mean(x * x, axis=-1, keepdims=True)
    return (x * lax.rsqrt(ms + EPS)) * gain


def _dot(a, b):
    return jnp.dot(a, b, preferred_element_type=F32)


def _dot_nt(a, b):
    return lax.dot_general(a, b, (((1,), (1,)), ((), ())), preferred_element_type=F32)


def _dot_tn(a, b):
    return lax.dot_general(a, b, (((0,), (0,)), ((), ())), preferred_element_type=F32)


def _split3(x):
    hi = x.astype(BF16)
    r1 = x - hi.astype(F32)
    mid = r1.astype(BF16)
    lo = (r1 - mid.astype(F32)).astype(BF16)
    return hi, mid, lo


def _rms_matmul_kernel(x_ref, g_ref, w_ref, o_ref, u_scr):
    @pl.when(pl.program_id(1) == 0)
    def _():
        u_scr[...] = _rms(x_ref[...], g_ref[...]).astype(BF16)

    o_ref[...] = _dot(u_scr[...], w_ref[...]).astype(o_ref.dtype)


def _rms_matmul(x, gain, w, out_dtype, tn):
    n, d = x.shape
    wd = w.shape[1]
    tm = min(TOK_TILE, n)
    return pl.pallas_call(
        _rms_matmul_kernel,
        grid=(n // tm, wd // tn),
        in_specs=[pl.BlockSpec((tm, d), lambda i, j: (i, 0)),
                  pl.BlockSpec((1, d), lambda i, j: (0, 0)),
                  pl.BlockSpec((d, tn), lambda i, j: (0, j))],
        out_specs=pl.BlockSpec((tm, tn), lambda i, j: (i, j)),
        out_shape=jax.ShapeDtypeStruct((n, wd), out_dtype),
        scratch_shapes=[pltpu.VMEM((tm, d), BF16)],
        compiler_params=_cparams(("parallel", "arbitrary")),
    )(x, gain, w)


def _hgrn_kernel(q_ref, i_ref, g_ref, f_ref, lbl_ref, gn_ref, tri_ref, o_ref, st_ref, *, n_chunks):
    c_len = HG_CHUNK
    half = c_len // 2

    @pl.when(pl.program_id(2) == 0)
    def _():
        st_ref[...] = jnp.zeros_like(st_ref)

    lg = lbl_ref[...]
    mx = jnp.max(lg, axis=0, keepdims=True)
    ex = jnp.exp(lg - mx)
    lb = ex[0:1, :] / jnp.sum(ex, axis=0, keepdims=True)

    row = lax.broadcasted_iota(jnp.int32, (c_len, c_len), 0)
    col = lax.broadcasted_iota(jnp.int32, (c_len, c_len), 1)
    causal = col <= row
    tri3 = tri_ref[...]

    st = st_ref[...]
    for c in range(n_chunks):
        rows = slice(c * c_len, (c + 1) * c_len)
        qx = q_ref[0, rows, :].astype(F32)
        qf = qx * jax.nn.sigmoid(qx) * (HG_DK ** -0.5)
        forget = lb + (1.0 - lb) * jax.nn.sigmoid(f_ref[0, rows, :])
        k = 1.0 - forget
        lf = jnp.log(forget)
        hi, mid, lo = _split3(lf)
        b = _dot(tri3, jnp.concatenate([hi, mid, lo], axis=0))
        b_ref_row = b[half - 1:half, :]
        b_last = b[c_len - 1:c_len, :]
        v = i_ref[0, rows, :]
        qe = qf * jnp.exp(jnp.minimum(b - b_ref_row, HG_EXP_CLAMP))
        ke = k * jnp.exp(jnp.minimum(b_ref_row - b, HG_EXP_CLAMP))
        a = _dot_nt(qe.astype(BF16), ke.astype(BF16))
        a = jnp.where(causal, a, 0.0)
        o = _dot(a.astype(BF16), v)
        o = o + _dot_nt((qf * jnp.exp(b)).astype(BF16), st.astype(BF16))
        kd = k * jnp.exp(b_last - b)
        st = st * jnp.exp(b_last) + _dot_tn(v, kd.astype(BF16))
        gx = g_ref[0, rows, :].astype(F32)
        y = _rms(o, gn_ref[...]) * (gx * jax.nn.sigmoid(gx))
        o_ref[0, rows, :] = y.astype(o_ref.dtype)
    st_ref[...] = st


def _hgrn(h3, hf, lb_logits, gnorm, bsz, seq):
    n_chunks = HG_TOK // HG_CHUNK
    tri = np.tril(np.ones((HG_CHUNK, HG_CHUNK), np.float32))
    tri3 = jnp.asarray(np.concatenate([tri, tri, tri], axis=1), BF16)
    hh = HG_HEADS
    blk = (1, HG_TOK, HG_DK)
    return pl.pallas_call(
        functools.partial(_hgrn_kernel, n_chunks=n_chunks),
        grid=(bsz, hh, seq // HG_TOK),
        in_specs=[pl.BlockSpec(blk, lambda b, h, t: (b, t, h)),
                  pl.BlockSpec(blk, lambda b, h, t: (b, t, hh + h)),
                  pl.BlockSpec(blk, lambda b, h, t: (b, t, 2 * hh + h)),
                  pl.BlockSpec(blk, lambda b, h, t: (b, t, h)),
                  pl.BlockSpec((2, HG_DK), lambda b, h, t: (0, h)),
                  pl.BlockSpec((1, HG_DV), lambda b, h, t: (0, 0)),
                  pl.BlockSpec((HG_CHUNK, 3 * HG_CHUNK), lambda b, h, t: (0, 0))],
        out_specs=pl.BlockSpec(blk, lambda b, h, t: (b, t, h)),
        out_shape=jax.ShapeDtypeStruct((bsz, seq, HG_WIDTH), BF16),
        scratch_shapes=[pltpu.VMEM((HG_DV, HG_DK), F32)],
        compiler_params=_cparams(("parallel", "parallel", "arbitrary")),
    )(h3, h3, h3, hf, lb_logits, gnorm, tri3)


def _rope_kernel(x_ref, c_ref, s1_ref, s2_ref, o_ref, *, scale):
    width = x_ref.shape[-1]
    c = c_ref[...]
    s1 = s1_ref[...]
    s2 = s2_ref[...]
    for j in range(width // LANES):
        cols = slice(j * LANES, (j + 1) * LANES)
        x = x_ref[0, :, cols].astype(F32)
        if scale != 1.0:
            x = x * scale
        y = x * c + pltpu.roll(x, LANES - ROPE_DIM // 2, 1) * s1 + pltpu.roll(x, ROPE_DIM // 2, 1) * s2
        o_ref[0, :, cols] = y.astype(o_ref.dtype)


def _rope(src, col_block, width, tables, scale, bsz, seq):
    tt = min(TOK_TILE, seq)
    tab = pl.BlockSpec((tt, LANES), lambda b, t: (t, 0))
    return pl.pallas_call(
        functools.partial(_rope_kernel, scale=scale),
        grid=(bsz, seq // tt),
        in_specs=[pl.BlockSpec((1, tt, width), lambda b, t: (b, t, col_block)), tab, tab, tab],
        out_specs=pl.BlockSpec((1, tt, width), lambda b, t: (b, t, 0)),
        out_shape=jax.ShapeDtypeStruct((bsz, seq, width), BF16),
        compiler_params=_cparams(("parallel", "parallel")),
    )(src, *tables)


def _rope_tables(seq):
    half = ROPE_DIM // 2
    inv = jnp.asarray(ROPE_THETA ** (-np.arange(half) * 2.0 / ROPE_DIM), F32)
    ang = jnp.arange(seq, dtype=F32)[:, None] * inv[None, :]
    cos, sin = jnp.cos(ang), jnp.sin(ang)
    lane = np.arange(LANES) % NSA_DH
    pick = lane % half
    cos_l = cos[:, pick]
    sin_l = sin[:, pick]
    first = jnp.asarray(lane < half)
    second = jnp.asarray((lane >= half) & (lane < ROPE_DIM))
    c = jnp.where(first | second, cos_l, 1.0)
    s1 = jnp.where(first, -sin_l, 0.0)
    s2 = jnp.where(second, sin_l, 0.0)
    return c, s1, s2


def _gelu_tanh(x):
    return 0.5 * x * (1.0 + jnp.tanh(np.sqrt(2.0 / np.pi).astype(np.float32) * (x + 0.044715 * (x * x * x))))


def _compress_kernel(xk_ref, xv_ref, pek_ref, pev_ref, wk1_ref, wk2_ref, wv1_ref, wv2_ref, ok_ref, ov_ref):
    half_w = CMP_STRIDE * NSA_DH

    def one(x_ref, pe_ref, w1_ref, w2_ref):
        x = x_ref[0, 0]
        n_slab = x.shape[0]
        p0 = _dot(x, w1_ref[0:half_w, :])
        p1 = _dot(x, w1_ref[half_w:2 * half_w, :])
        pe = jnp.broadcast_to(pe_ref[...], (8, 2 * half_w)).astype(BF16)
        pw = _dot(pe, w1_ref[...])[0:1, :]
        h = p0 + pltpu.roll(p1, n_slab - 1, 0) + pw
        return _dot(_gelu_tanh(h).astype(BF16), w2_ref[...])

    ok_ref[0, 0] = one(xk_ref, pek_ref, wk1_ref, wk2_ref).astype(ok_ref.dtype)
    ov_ref[0, 0] = one(xv_ref, pev_ref, wv1_ref, wv2_ref).astype(ov_ref.dtype)


def _compress(xk, xv, pek, pev, wk1, wk2, wv1, wv2):
    bsz, grp, n_slab, wdt = xk.shape
    xs = pl.BlockSpec((1, 1, n_slab, wdt), lambda b, g: (b, g, 0, 0))
    full = lambda a: pl.BlockSpec(a.shape, lambda b, g: (0,) * a.ndim)
    osd = jax.ShapeDtypeStruct((bsz, grp, n_slab, NSA_DH), BF16)
    os_ = pl.BlockSpec((1, 1, n_slab, NSA_DH), lambda b, g: (b, g, 0, 0))
    return pl.pallas_call(
        _compress_kernel,
        grid=(bsz, grp),
        in_specs=[xs, xs, full(pek), full(pev), full(wk1), full(wk2), full(wv1), full(wv2)],
        out_specs=[os_, os_],
        out_shape=[osd, osd],
        compiler_params=_cparams(("parallel", "parallel")),
    )(xk, xv, pek, pev, wk1, wk2, wv1, wv2)


def _attn_kernel(nq_ref, qr_ref, kcmp_ref, vcmpt_ref, ks_ref, vst_ref, kw_ref, vwt_ref, ng_ref, ovt_ref,
                 o_ref, selb_scr, gt_scr, *, seq):
    qn = ATT_Q
    grp = NSA_GROUP
    ln = grp * qn
    n_sel = seq // SEL_BLOCK
    top = min(SEL_TOPK, n_sel)
    g_id = pl.program_id(1)
    q0 = pl.multiple_of(pl.program_id(2) * qn, qn)

    def stack_heads(x):
        return jnp.concatenate([x[:, r * NSA_DH:(r + 1) * NSA_DH] for r in range(grp)], axis=0)

    qs = stack_heads(nq_ref[0])
    qrs = stack_heads(qr_ref[0])
    t_lane = q0 + (lax.broadcasted_iota(jnp.int32, (1, ln), 1) & (qn - 1))

    n_slab = kcmp_ref.shape[2]
    sc = _dot_nt(kcmp_ref[0, 0], qs) * ATT_SCALE
    cend = lax.broadcasted_iota(jnp.int32, (n_slab, 1), 0) * CMP_STRIDE + (CMP_LEN - 1)
    valid = cend <= t_lane
    sm = jnp.where(valid, sc, NEG)
    e = jnp.where(valid, jnp.exp(sm - jnp.max(sm, axis=0, keepdims=True)), 0.0)
    den = jnp.sum(e, axis=0, keepdims=True)
    p = e / jnp.where(den > 0.0, den, 1.0)
    oc = _dot(vcmpt_ref[0, 0], p.astype(BF16))

    psum = p[:, 0:qn]
    for r in range(1, grp):
        psum = psum + p[:, r * qn:(r + 1) * qn]
    hi, mid, lo = _split3(psum)
    ovt = ovt_ref[...]
    imp = _dot(ovt, hi) + _dot(ovt, mid) + _dot(ovt, lo)
    jidx = lax.broadcasted_iota(jnp.int32, (n_sel, qn), 0)
    tq = q0 + lax.broadcasted_iota(jnp.int32, (n_sel, qn), 1)
    cur = tq // SEL_BLOCK
    forced = (jidx == 0) | (jidx == cur) | (jidx == cur - 1)
    causal_blk = jidx * SEL_BLOCK <= tq
    val = jnp.where(forced, 1e9, imp)
    val = jnp.where(causal_blk, val, -1.0)
    sel = jnp.zeros((n_sel, qn), F32)
    for _ in range(top):
        mx = jnp.max(val, axis=0, keepdims=True)
        first = jnp.min(jnp.where(val == mx, jidx, n_sel), axis=0, keepdims=True)
        hit = jidx == first
        sel = jnp.where(hit, 1.0, sel)
        val = jnp.where(hit, -2.0, val)
    selb_scr[...] = jnp.where((sel > 0.0) & causal_blk, 0.0, NEG)

    kt_len = ATT_KT
    blocks_per_tile = kt_len // SEL_BLOCK

    def sel_tile(kt, carry, diag):
        m_run, l_run, acc = carry
        k0 = pl.multiple_of(kt * kt_len, kt_len)
        s = _dot_nt(ks_ref[0, 0, pl.ds(k0, kt_len), :], qrs)
        rows = []
        for c in range(blocks_per_tile):
            rb = selb_scr[pl.ds(kt * blocks_per_tile + c, 1), :]
            rows.append(jnp.broadcast_to(jnp.concatenate([rb] * grp, axis=1), (SEL_BLOCK, ln)))
        s = s + jnp.concatenate(rows, axis=0)
        if diag:
            kpos = k0 + lax.broadcasted_iota(jnp.int32, (kt_len, 1), 0)
            s = jnp.where(kpos <= t_lane, s, NEG)
        m_new = jnp.maximum(m_run, jnp.max(s, axis=0, keepdims=True))
        alpha = jnp.exp(m_run - m_new)
        pt = jnp.exp(s - m_new)
        l_new = alpha * l_run + jnp.sum(pt, axis=0, keepdims=True)
        acc_new = alpha * acc + _dot(vst_ref[0, 0, :, pl.ds(k0, kt_len)], pt.astype(BF16))
        return m_new, l_new, acc_new

    init = (jnp.full((1, ln), NEG, F32), jnp.zeros((1, ln), F32), jnp.zeros((NSA_DH, ln), F32))
    n_full = q0 // kt_len
    carry = lax.fori_loop(0, n_full, lambda kt, cy: sel_tile(kt, cy, False), init)
    _, l_fin, acc = sel_tile(n_full, carry, True)
    osel = acc / l_fin

    wlen = WINDOW + qn
    w0 = pl.multiple_of(jnp.maximum(q0 - WINDOW, 0), qn)
    s = _dot_nt(kw_ref[0, 0, pl.ds(w0, wlen), :], qrs)
    diff = t_lane - (w0 + lax.broadcasted_iota(jnp.int32, (wlen, 1), 0))
    s = jnp.where((diff >= 0) & (diff < WINDOW), s, NEG)
    pw = jnp.exp(s - jnp.max(s, axis=0, keepdims=True))
    pw = pw / jnp.sum(pw, axis=0, keepdims=True)
    ow = _dot(vwt_ref[0, 0, :, pl.ds(w0, wlen)], pw.astype(BF16))

    gt_scr[...] = jax.nn.sigmoid(ng_ref[0]).T

    def gate_row(branch):
        return jnp.concatenate(
            [gt_scr[pl.ds(branch * NSA_HEADS + g_id * grp + r, 1), :] for r in range(grp)], axis=1)

    out_t = gate_row(0) * oc + gate_row(1) * osel + gate_row(2) * ow
    pieces = []
    for r2 in range(grp // 2):
        pair = jnp.concatenate([out_t[:, (2 * r2) * qn:(2 * r2 + 1) * qn],
                                out_t[:, (2 * r2 + 1) * qn:(2 * r2 + 2) * qn]], axis=0)
        pieces.append(pair.T)
    o_ref[0] = jnp.concatenate(pieces, axis=1).astype(o_ref.dtype)


def _attention(nsa, q_rot, k_cmp, v_cmp_t, ks, vs_t, kw, vw_t, ngate, ov_t, bsz, seq):
    qn = ATT_Q
    gw = NSA_GROUP * NSA_DH
    n_slab = k_cmp.shape[2]
    n_sel = seq // SEL_BLOCK
    qspec = pl.BlockSpec((1, qn, gw), lambda b, g, i: (b, i, g))
    per_group = lambda a: pl.BlockSpec((1, 1) + a.shape[2:], lambda b, g, i: (b, g, 0, 0))
    return pl.pallas_call(
        functools.partial(_attn_kernel, seq=seq),
        grid=(bsz, NSA_KV, seq // qn),
        in_specs=[qspec, qspec, per_group(k_cmp), per_group(v_cmp_t), per_group(ks), per_group(vs_t),
                  per_group(kw), per_group(vw_t),
                  pl.BlockSpec((1, qn, LANES), lambda b, g, i: (b, i, 0)),
                  pl.BlockSpec((n_sel, n_slab), lambda b, g, i: (0, 0))],
        out_specs=qspec,
        out_shape=jax.ShapeDtypeStruct((bsz, seq, NSA_WIDTH), BF16),
        scratch_shapes=[pltpu.VMEM((n_sel, qn), F32), pltpu.VMEM((LANES, qn), F32)],
        compiler_params=_cparams(("parallel", "parallel", "arbitrary")),
    )(nsa, q_rot, k_cmp, v_cmp_t, ks, vs_t, kw, vw_t, ngate, ov_t)


def _merge_kernel(x_ref, ya_ref, yb_ref, ga_ref, gb_ref, wa_ref, wb_ref, wo_ref, gn_ref, o_ref):
    ga = jax.nn.sigmoid(ga_ref[...].astype(F32))
    gb = jax.nn.sigmoid(gb_ref[...].astype(F32))
    merged = ga * _dot(ya_ref[...], wa_ref[...]) + gb * _dot(yb_ref[...], wb_ref[...])
    mo = _dot(merged.astype(BF16), wo_ref[...])
    o_ref[...] = x_ref[...] + _rms(mo, gn_ref[...])


def _merge(x, ya, yb, gates, wa, wb, wo, gn):
    n, d = x.shape
    tm = min(TOK_TILE, n)
    row = lambda c: pl.BlockSpec((tm, d), lambda i: (i, c))
    wsp = pl.BlockSpec((d, d), lambda i: (0, 0))
    return pl.pallas_call(
        _merge_kernel,
        grid=(n // tm,),
        in_specs=[row(0), row(0), row(0), row(0), row(1), wsp, wsp, wsp,
                  pl.BlockSpec((1, d), lambda i: (0, 0))],
        out_specs=row(0),
        out_shape=jax.ShapeDtypeStruct((n, d), F32),
        compiler_params=_cparams(("parallel",)),
    )(x, ya, yb, gates, gates, wa, wb, wo, gn)


def _mlp_kernel(h_ref, gpre_ref, wu_ref, wd_ref, gpost_ref, o_ref, v_scr, acc_scr):
    j = pl.program_id(1)

    @pl.when(j == 0)
    def _():
        v_scr[...] = _rms(h_ref[...], gpre_ref[...]).astype(BF16)
        acc_scr[...] = jnp.zeros_like(acc_scr)

    up = _dot(v_scr[...], wu_ref[...])
    act = jnp.square(jnp.maximum(up, 0.0))
    acc_scr[...] += _dot(act.astype(BF16), wd_ref[...])

    @pl.when(j == pl.num_programs(1) - 1)
    def _():
        o_ref[...] = h_ref[...] + _rms(acc_scr[...], gpost_ref[...])


def _mlp(h, gpre, wu, wd, gpost):
    n, d = h.shape
    dff = wu.shape[1]
    tm = min(2 * TOK_TILE, n)
    tf = 1024
    vec = pl.BlockSpec((1, d), lambda i, j: (0, 0))
    return pl.pallas_call(
        _mlp_kernel,
        grid=(n // tm, dff // tf),
        in_specs=[pl.BlockSpec((tm, d), lambda i, j: (i, 0)), vec,
                  pl.BlockSpec((d, tf), lambda i, j: (0, j)),
                  pl.BlockSpec((tf, d), lambda i, j: (j, 0)), vec],
        out_specs=pl.BlockSpec((tm, d), lambda i, j: (i, 0)),
        out_shape=jax.ShapeDtypeStruct((n, d), F32),
        scratch_shapes=[pltpu.VMEM((tm, d), BF16), pltpu.VMEM((tm, d), F32)],
        compiler_params=_cparams(("parallel", "arbitrary")),
    )(h, gpre, wu, wd, gpost)


def _ple_kernel(h_ref, p_ref, wp_ref, wg_ref, gn_ref, o_ref):
    h = h_ref[...]
    e = _dot(p_ref[...].astype(BF16), wp_ref[...]) * jax.nn.sigmoid(_dot(h.astype(BF16), wg_ref[...]))
    o_ref[...] = h + _rms(e, gn_ref[...])


def _ple(h, p, wp, wg, gn):
    n, d = h.shape
    pd = p.shape[1]
    tm = min(TOK_TILE, n)
    return pl.pallas_call(
        _ple_kernel,
        grid=(n // tm,),
        in_specs=[pl.BlockSpec((tm, d), lambda i: (i, 0)),
                  pl.BlockSpec((tm, pd), lambda i: (i, 0)),
                  pl.BlockSpec((pd, d), lambda i: (0, 0)),
                  pl.BlockSpec((d, d), lambda i: (0, 0)),
                  pl.BlockSpec((1, d), lambda i: (0, 0))],
        out_specs=pl.BlockSpec((tm, d), lambda i: (i, 0)),
        out_shape=jax.ShapeDtypeStruct((n, d), F32),
        compiler_params=_cparams(("parallel",)),
    )(h, p, wp, wg, gn)


def _overlap_t(seq):
    n_slab = seq // CMP_STRIDE
    n_cmp = (seq - CMP_LEN) // CMP_STRIDE + 1
    n_sel = seq // SEL_BLOCK
    start = np.arange(n_slab) * CMP_STRIDE
    end = start + CMP_LEN - 1
    sel_start = np.arange(n_sel) * SEL_BLOCK
    ov = (start[None, :] < sel_start[:, None] + SEL_BLOCK) & (end[None, :] >= sel_start[:, None])
    ov = ov & (np.arange(n_slab)[None, :] < n_cmp)
    return jnp.asarray(ov.astype(np.float32), BF16)


def kernel(x, p, w_in, w_branch_a, w_branch_b, w_out, norm_pre_mix, norm_post_mix, norm_pre_mlp,
           norm_post_mlp, hg_lb_logits, hg_gnorm, cmp_pe_k, cmp_pe_v, cmp_wk1, cmp_wk2, cmp_wv1, cmp_wv2,
           w_up, w_down, w_ple, w_ple_gate, norm_ple):
    bsz, seq, d = x.shape
    depth = w_in.shape[0]
    assert depth == 1 and hg_lb_logits.shape[0] == 2 and d == D_MODEL
    assert seq % ATT_KT == 0 and seq >= WINDOW + ATT_Q
    n = bsz * seq
    x2 = x.reshape(n, d)
    row = lambda a: a.reshape(1, -1)

    w = w_in[0]
    o_hq, o_hf, o_hi, o_hg = 0, HG_WIDTH, 2 * HG_WIDTH, 3 * HG_WIDTH
    o_nq = 4 * HG_WIDTH
    o_kv = o_nq + NSA_WIDTH
    o_gate = o_kv + 6 * NSA_KVW
    o_ga = o_gate + 3 * NSA_HEADS
    col = lambda a, b: w[:, a:b]
    w_h3 = jnp.concatenate([col(o_hq, o_hf), col(o_hi, o_hg), col(o_hg, o_nq)], axis=1).astype(BF16)
    w_hf = col(o_hf, o_hi).astype(BF16)
    w_nsa = col(o_nq, o_gate).astype(BF16)
    w_ng = jnp.pad(col(o_gate, o_ga), ((0, 0), (0, LANES - 3 * NSA_HEADS))).astype(BF16)
    w_g = col(o_ga, o_ga + 2 * D_MODEL).astype(BF16)
    g_pre = row(norm_pre_mix[0])
    h3 = _rms_matmul(x2, g_pre, w_h3, BF16, 1024).reshape(bsz, seq, -1)
    hf = _rms_matmul(x2, g_pre, w_hf, F32, 1024).reshape(bsz, seq, -1)
    nsa = _rms_matmul(x2, g_pre, w_nsa, BF16, 512).reshape(bsz, seq, -1)
    ngate = _rms_matmul(x2, g_pre, w_ng, F32, LANES).reshape(bsz, seq, -1)
    gates = _rms_matmul(x2, g_pre, w_g, BF16, 1024)

    y_a = _hgrn(h3, hf, hg_lb_logits, row(hg_gnorm[0]), bsz, seq)

    tables = _rope_tables(seq)
    kvb = NSA_WIDTH // NSA_KVW
    q_rot = _rope(nsa, 0, NSA_WIDTH, tables, ATT_SCALE, bsz, seq)
    ks_rot = _rope(nsa, kvb + 2, NSA_KVW, tables, 1.0, bsz, seq)
    kw_rot = _rope(nsa, kvb + 4, NSA_KVW, tables, 1.0, bsz, seq)
    kv_slice = lambda i: nsa[:, :, NSA_WIDTH + i * NSA_KVW: NSA_WIDTH + (i + 1) * NSA_KVW]
    heads = lambda t: t.reshape(bsz, seq, NSA_KV, NSA_DH).transpose(0, 2, 1, 3)
    heads_t = lambda t: t.reshape(bsz, seq, NSA_KV, NSA_DH).transpose(0, 2, 3, 1)
    slabs = lambda t: heads(t).reshape(bsz, NSA_KV, seq // CMP_STRIDE, CMP_STRIDE * NSA_DH)
    k_cmp, v_cmp = _compress(
        slabs(kv_slice(0)), slabs(kv_slice(1)), cmp_pe_k[0].reshape(1, -1), cmp_pe_v[0].reshape(1, -1),
        cmp_wk1[0].astype(BF16), cmp_wk2[0].astype(BF16), cmp_wv1[0].astype(BF16), cmp_wv2[0].astype(BF16))
    y_b = _attention(nsa, q_rot, k_cmp, v_cmp.transpose(0, 1, 3, 2), heads(ks_rot), heads_t(kv_slice(3)),
                     heads(kw_rot), heads_t(kv_slice(5)), ngate, _overlap_t(seq), bsz, seq)

    h1 = _merge(x2, y_a.reshape(n, -1), y_b.reshape(n, -1), gates, w_branch_a[0].astype(BF16),
                w_branch_b[0].astype(BF16), w_out[0].astype(BF16), row(norm_post_mix[0]))
    h2 = _mlp(h1, row(norm_pre_mlp[0]), w_up[0].astype(BF16), w_down[0].astype(BF16), row(norm_post_mlp[0]))
    h3o = _ple(h2, p[0].reshape(n, -1), w_ple[0].astype(BF16), w_ple_gate[0].astype(BF16), row(norm_ple[0]))
    return h3o.reshape(bsz, seq, d)
```

```python
import functools

import numpy as np
import jax
import jax.numpy as jnp
from jax import lax
from jax.experimental import pallas as pl
from jax.experimental.pallas import tpu as pltpu

F32 = jnp.float32
BF16 = jnp.bfloat16

D_MODEL = 1024
HG_HEADS = 8
HG_DK = 128
HG_DV = 128
HG_WIDTH = HG_HEADS * HG_DK
HG_CHUNK = 64
NSA_HEADS = 16
NSA_KV = 4
NSA_GROUP = NSA_HEADS // NSA_KV
NSA_DH = 64
NSA_WIDTH = NSA_HEADS * NSA_DH
NSA_KVW = NSA_KV * NSA_DH
CMP_LEN = 32
CMP_STRIDE = 16
CMP_HIDDEN = 256
SEL_BLOCK = 64
SEL_TOPK = 16
WINDOW = 512
ROPE_THETA = 500000.0
ROPE_DIM = NSA_DH // 4
D_FF = 4 * D_MODEL
EPS = 1e-6
NEG = -1e30
ATT_SCALE = NSA_DH ** -0.5
LOG2E = float(np.log2(np.e))

LANES = 128
SUBLANES_BF16 = 16
VMEM_LIMIT = 56 * 1024 * 1024

TOK_TILE = 512
ATT_Q = 128
ATT_KT = 512
ATT_SUBTILES = 4
HG_TOK = 256
HG_EXP_CLAMP = 80.0


def _cparams(sem):
    return pltpu.CompilerParams(dimension_semantics=sem, vmem_limit_bytes=VMEM_LIMIT)


def _rms(x, gain):
    ms = jnp.mean(x * x, axis=-1, keepdims=True)
    return (x * lax.rsqrt(ms + EPS)) * gain


def _dot(a, b):
    return jnp.dot(a, b, preferred_element_type=F32)


def _dot_nt(a, b):
    return lax.dot_general(a, b, (((1,), (1,)), ((), ())), preferred_element_type=F32)


def _dot_tn(a, b):
    return lax.dot_general(a, b, (((0,), (0,)), ((), ())), preferred_element_type=F32)


def _colmax(s):
    parts = [s[i:i + 64] for i in range(0, s.shape[0], 64)] if s.shape[0] % 64 == 0 else [s]
    while len(parts) > 1:
        parts = [jnp.maximum(parts[i], parts[i + 1]) if i + 1 < len(parts) else parts[i]
                 for i in range(0, len(parts), 2)]
    return jnp.max(parts[0], axis=0, keepdims=True)


def _split3(x):
    hi = x.astype(BF16)
    r1 = x - hi.astype(F32)
    mid = r1.astype(BF16)
    lo = (r1 - mid.astype(F32)).astype(BF16)
    return hi, mid, lo


def _rms_matmul_kernel(x_ref, g_ref, w_ref, o_ref, u_scr):
    @pl.when(pl.program_id(1) == 0)
    def _():
        u_scr[...] = _rms(x_ref[...], g_ref[...]).astype(BF16)

    o_ref[...] = _dot(u_scr[...], w_ref[...]).astype(o_ref.dtype)


def _rms_matmul(x, gain, w, out_dtype, tn):
    n, d = x.shape
    wd = w.shape[1]
    tm = min(TOK_TILE, n)
    return pl.pallas_call(
        _rms_matmul_kernel,
        grid=(n // tm, wd // tn),
        in_specs=[pl.BlockSpec((tm, d), lambda i, j: (i, 0)),
                  pl.BlockSpec((1, d), lambda i, j: (0, 0)),
                  pl.BlockSpec((d, tn), lambda i, j: (0, j))],
        out_specs=pl.BlockSpec((tm, tn), lambda i, j: (i, j)),
        out_shape=jax.ShapeDtypeStruct((n, wd), out_dtype),
        scratch_shapes=[pltpu.VMEM((tm, d), BF16)],
        compiler_params=_cparams(("parallel", "arbitrary")),
    )(x, gain, w)


def _hgrn_kernel(q_ref, i_ref, g_ref, f_ref, lbl_ref, gn_ref, tri_ref, o_ref, st_ref, *, n_chunks):
    c_len = HG_CHUNK
    half = c_len // 2

    @pl.when(pl.program_id(2) == 0)
    def _():
        st_ref[...] = jnp.zeros_like(st_ref)

    lg = lbl_ref[...]
    mx = jnp.max(lg, axis=0, keepdims=True)
    ex = jnp.exp(lg - mx)
    lb = ex[0:1, :] / jnp.sum(ex, axis=0, keepdims=True)

    row = lax.broadcasted_iota(jnp.int32, (c_len, c_len), 0)
    col = lax.broadcasted_iota(jnp.int32, (c_len, c_len), 1)
    causal = col <= row
    tri3 = tri_ref[...]

    st = st_ref[...]
    for c in range(n_chunks):
        rows = slice(c * c_len, (c + 1) * c_len)
        qx = q_ref[0, rows, :].astype(F32)
        qf = qx * jax.nn.sigmoid(qx) * (HG_DK ** -0.5)
        forget = lb + (1.0 - lb) * jax.nn.sigmoid(f_ref[0, rows, :])
        k = 1.0 - forget
        lf = jnp.log(forget)
        hi, mid, lo = _split3(lf)
        b = _dot(tri3, jnp.concatenate([hi, mid, lo], axis=0))
        b_ref_row = b[half - 1:half, :]
        b_last = b[c_len - 1:c_len, :]
        v = i_ref[0, rows, :]
        qe = qf * jnp.exp(jnp.minimum(b - b_ref_row, HG_EXP_CLAMP))
        ke = k * jnp.exp(jnp.minimum(b_ref_row - b, HG_EXP_CLAMP))
        a = _dot_nt(qe.astype(BF16), ke.astype(BF16))
        a = jnp.where(causal, a, 0.0)
        o = _dot(a.astype(BF16), v)
        o = o + _dot_nt((qf * jnp.exp(b)).astype(BF16), st.astype(BF16))
        kd = k * jnp.exp(b_last - b)
        st = st * jnp.exp(b_last) + _dot_tn(v, kd.astype(BF16))
        gx = g_ref[0, rows, :].astype(F32)
        y = _rms(o, gn_ref[...]) * (gx * jax.nn.sigmoid(gx))
        o_ref[0, rows, :] = y.astype(o_ref.dtype)
    st_ref[...] = st


def _hgrn(h3, hf, lb_logits, gnorm, bsz, seq):
    n_chunks = HG_TOK // HG_CHUNK
    tri = np.tril(np.ones((HG_CHUNK, HG_CHUNK), np.float32))
    tri3 = jnp.asarray(np.concatenate([tri, tri, tri], axis=1), BF16)
    hh = HG_HEADS
    blk = (1, HG_TOK, HG_DK)
    return pl.pallas_call(
        functools.partial(_hgrn_kernel, n_chunks=n_chunks),
        grid=(bsz, hh, seq // HG_TOK),
        in_specs=[pl.BlockSpec(blk, lambda b, h, t: (b, t, h)),
                  pl.BlockSpec(blk, lambda b, h, t: (b, t, hh + h)),
                  pl.BlockSpec(blk, lambda b, h, t: (b, t, 2 * hh + h)),
                  pl.BlockSpec(blk, lambda b, h, t: (b, t, h)),
                  pl.BlockSpec((2, HG_DK), lambda b, h, t: (0, h)),
                  pl.BlockSpec((1, HG_DV), lambda b, h, t: (0, 0)),
                  pl.BlockSpec((HG_CHUNK, 3 * HG_CHUNK), lambda b, h, t: (0, 0))],
        out_specs=pl.BlockSpec(blk, lambda b, h, t: (b, t, h)),
        out_shape=jax.ShapeDtypeStruct((bsz, seq, HG_WIDTH), BF16),
        scratch_shapes=[pltpu.VMEM((HG_DV, HG_DK), F32)],
        compiler_params=_cparams(("parallel", "parallel", "arbitrary")),
    )(h3, h3, h3, hf, lb_logits, gnorm, tri3)


def _rope_kernel(x_ref, c_ref, s1_ref, s2_ref, o_ref, *, scale):
    width = x_ref.shape[-1]
    c = c_ref[...]
    s1 = s1_ref[...]
    s2 = s2_ref[...]
    for j in range(width // LANES):
        cols = slice(j * LANES, (j + 1) * LANES)
        x = x_ref[0, :, cols].astype(F32)
        if scale != 1.0:
            x = x * scale
        y = x * c + pltpu.roll(x, LANES - ROPE_DIM // 2, 1) * s1 + pltpu.roll(x, ROPE_DIM // 2, 1) * s2
        o_ref[0, :, cols] = y.astype(o_ref.dtype)


def _rope(src, col_block, width, tables, scale, bsz, seq):
    tt = min(TOK_TILE, seq)
    tab = pl.BlockSpec((tt, LANES), lambda b, t: (t, 0))
    return pl.pallas_call(
        functools.partial(_rope_kernel, scale=scale),
        grid=(bsz, seq // tt),
        in_specs=[pl.BlockSpec((1, tt, width), lambda b, t: (b, t, col_block)), tab, tab, tab],
        out_specs=pl.BlockSpec((1, tt, width), lambda b, t: (b, t, 0)),
        out_shape=jax.ShapeDtypeStruct((bsz, seq, width), BF16),
        compiler_params=_cparams(("parallel", "parallel")),
    )(src, *tables)


def _rope_tables(seq):
    half = ROPE_DIM // 2
    inv = jnp.asarray(ROPE_THETA ** (-np.arange(half) * 2.0 / ROPE_DIM), F32)
    ang = jnp.arange(seq, dtype=F32)[:, None] * inv[None, :]
    cos, sin = jnp.cos(ang), jnp.sin(ang)
    lane = np.arange(LANES) % NSA_DH
    pick = lane % half
    cos_l = cos[:, pick]
    sin_l = sin[:, pick]
    first = jnp.asarray(lane < half)
    second = jnp.asarray((lane >= half) & (lane < ROPE_DIM))
    c = jnp.where(first | second, cos_l, 1.0)
    s1 = jnp.where(first, -sin_l, 0.0)
    s2 = jnp.where(second, sin_l, 0.0)
    return c, s1, s2


def _gelu_tanh(x):
    return 0.5 * x * (1.0 + jnp.tanh(np.sqrt(2.0 / np.pi).astype(np.float32) * (x + 0.044715 * (x * x * x))))


def _compress_kernel(xk_ref, xv_ref, pek_ref, pev_ref, wk1_ref, wk2_ref, wv1_ref, wv2_ref, ok_ref, ov_ref):
    half_w = CMP_STRIDE * NSA_DH

    def one(x_ref, pe_ref, w1_ref, w2_ref):
        x = x_ref[0, 0]
        n_slab = x.shape[0]
        p0 = _dot(x, w1_ref[0:half_w, :])
        p1 = _dot(x, w1_ref[half_w:2 * half_w, :])
        pe = jnp.broadcast_to(pe_ref[...], (8, 2 * half_w)).astype(BF16)
        pw = _dot(pe, w1_ref[...])[0:1, :]
        h = p0 + pltpu.roll(p1, n_slab - 1, 0) + pw
        return _dot(_gelu_tanh(h).astype(BF16), w2_ref[...])

    ok_ref[0, 0] = one(xk_ref, pek_ref, wk1_ref, wk2_ref).astype(ok_ref.dtype)
    ov_ref[0, 0] = one(xv_ref, pev_ref, wv1_ref, wv2_ref).astype(ov_ref.dtype)


def _compress(xk, xv, pek, pev, wk1, wk2, wv1, wv2):
    bsz, grp, n_slab, wdt = xk.shape
    xs = pl.BlockSpec((1, 1, n_slab, wdt), lambda b, g: (b, g, 0, 0))
    full = lambda a: pl.BlockSpec(a.shape, lambda b, g: (0,) * a.ndim)
    osd = jax.ShapeDtypeStruct((bsz, grp, n_slab, NSA_DH), BF16)
    os_ = pl.BlockSpec((1, 1, n_slab, NSA_DH), lambda b, g: (b, g, 0, 0))
    return pl.pallas_call(
        _compress_kernel,
        grid=(bsz, grp),
        in_specs=[xs, xs, full(pek), full(pev), full(wk1), full(wk2), full(wv1), full(wv2)],
        out_specs=[os_, os_],
        out_shape=[osd, osd],
        compiler_params=_cparams(("parallel", "parallel")),
    )(xk, xv, pek, pev, wk1, wk2, wv1, wv2)


def _attn_kernel(nq_ref, qr_ref, kcmp_ref, vcmpt_ref, ks_ref, vst_ref, kw_ref, vwt_ref, ng_ref, ovt_ref,
                 o_ref, gt_scr, qaug_scr, *, seq):
    qn = ATT_Q
    grp = NSA_GROUP
    ln = grp * qn
    n_sel = seq // SEL_BLOCK
    top = min(SEL_TOPK, n_sel)
    g_id = pl.program_id(1)
    q0 = pl.multiple_of(pl.program_id(2) * qn, qn)

    def stack_heads(x):
        return jnp.concatenate([x[:, r * NSA_DH:(r + 1) * NSA_DH] for r in range(grp)], axis=0)

    def add_per_head(s, bias):
        return jnp.concatenate([s[:, r * qn:(r + 1) * qn] + bias for r in range(grp)], axis=1)

    qs = stack_heads(nq_ref[0])
    qrs = stack_heads(qr_ref[0])
    t_lane = q0 + (lax.broadcasted_iota(jnp.int32, (1, ln), 1) & (qn - 1))
    t_q = q0 + lax.broadcasted_iota(jnp.int32, (1, qn), 1)

    n_slab = kcmp_ref.shape[2]
    sc = _dot_nt(kcmp_ref[0, 0], qs) * ATT_SCALE
    cend = lax.broadcasted_iota(jnp.int32, (n_slab, 1), 0) * CMP_STRIDE + (CMP_LEN - 1)
    sm = add_per_head(sc, jnp.where(cend <= t_q, 0.0, NEG))
    e = jnp.exp(sm - _colmax(sm))
    den = jnp.sum(e, axis=0, keepdims=True)
    p = e * jnp.where(t_lane >= CMP_LEN - 1, 1.0 / den, 0.0)
    oc = _dot(vcmpt_ref[0, 0], p.astype(BF16))

    psum = p[:, 0:qn]
    for r in range(1, grp):
        psum = psum + p[:, r * qn:(r + 1) * qn]
    hi, mid, lo = _split3(psum)
    ovt = ovt_ref[...]
    imp = _dot(ovt, hi) + _dot(ovt, mid) + _dot(ovt, lo)
    jidx = lax.broadcasted_iota(jnp.int32, (n_sel, qn), 0)
    tq = q0 + lax.broadcasted_iota(jnp.int32, (n_sel, qn), 1)
    cur = tq // SEL_BLOCK
    forced = (jidx == 0) | (jidx == cur) | (jidx == cur - 1)
    causal_blk = jidx * SEL_BLOCK <= tq
    val = jnp.where(forced, -2.0, imp)
    val = jnp.where(causal_blk, val, -1.0)
    sel = jnp.where(forced, 1.0, 0.0)
    for _ in range(max(top - 3, 0)):
        mx = jnp.max(val, axis=0, keepdims=True)
        first = jnp.min(jnp.where(val == mx, jidx, n_sel), axis=0, keepdims=True)
        hit = jidx == first
        sel = jnp.where(hit, 1.0, sel)
        val = jnp.where(hit, -2.0, val)
    diag_blk = q0 // SEL_BLOCK
    in_diag = (jidx >= diag_blk) & (jidx < diag_blk + qn // SEL_BLOCK)
    selb = jnp.where((sel > 0.0) & causal_blk & jnp.logical_not(in_diag), 0.0, NEG)
    selb_t = jnp.concatenate([selb, jnp.zeros((LANES - n_sel, qn), F32)], axis=0).T
    selb_t = selb_t[:, 0:NSA_DH].astype(BF16)
    qaug_scr[...] = jnp.concatenate([qrs, jnp.concatenate([selb_t] * grp, axis=0)], axis=1)

    wlen = WINDOW + qn
    w0 = pl.multiple_of(jnp.maximum(q0 - WINDOW, 0), qn)
    s = _dot_nt(kw_ref[0, 0, pl.ds(w0, wlen), :], qrs)
    diff = t_q - (w0 + lax.broadcasted_iota(jnp.int32, (wlen, 1), 0))
    s = add_per_head(s, jnp.where((diff >= 0) & (diff < WINDOW), 0.0, NEG))
    pw = jnp.exp2(s - _colmax(s))
    rw = _dot(vwt_ref[0, 0, :, pl.ds(w0, wlen)], pw.astype(BF16))
    ow = rw[0:NSA_DH] / rw[NSA_DH:NSA_DH + 1]

    s = _dot_nt(ks_ref[0, 0, pl.ds(q0, qn), :][:, 0:NSA_DH], qrs)
    kpos = q0 + lax.broadcasted_iota(jnp.int32, (qn, 1), 0)
    s = add_per_head(s, jnp.where(kpos <= t_q, 0.0, NEG))
    m0 = _colmax(s)
    acc0 = _dot(vst_ref[0, 0, :, pl.ds(q0, qn)], jnp.exp2(s - m0).astype(BF16))

    kt_len = ATT_KT
    sub = kt_len // ATT_SUBTILES

    def sweep_step(kt, carry):
        m_run, acc = carry
        k0s = [pl.multiple_of(kt * kt_len + j * sub, sub) for j in range(ATT_SUBTILES)]
        scores = [_dot_nt(ks_ref[0, 0, pl.ds(k0, sub), :], qaug_scr[...]) for k0 in k0s]
        for k0, s in zip(k0s, scores):
            m_new = jnp.maximum(m_run, _colmax(s))
            alpha = jnp.exp2(m_run - m_new)
            pt = jnp.exp2(s - m_new).astype(BF16)
            acc = alpha * acc + _dot(vst_ref[0, 0, :, pl.ds(k0, sub)], pt)
            m_run = m_new
        return m_run, acc

    n_tiles = (q0 + kt_len - 1) // kt_len
    _, acc = lax.fori_loop(0, n_tiles, sweep_step, (m0, acc0))
    osel = acc[0:NSA_DH] / acc[NSA_DH:NSA_DH + 1]

    gt_scr[...] = jax.nn.sigmoid(ng_ref[0]).T

    def gate_row(branch):
        return jnp.concatenate(
            [gt_scr[pl.ds(branch * NSA_HEADS + g_id * grp + r, 1), :] for r in range(grp)], axis=1)

    out_t = gate_row(0) * oc + gate_row(1) * osel + gate_row(2) * ow
    pieces = []
    for r2 in range(grp // 2):
        pair = jnp.concatenate([out_t[:, (2 * r2) * qn:(2 * r2 + 1) * qn],
                                out_t[:, (2 * r2 + 1) * qn:(2 * r2 + 2) * qn]], axis=0)
        pieces.append(pair.T)
    o_ref[0] = jnp.concatenate(pieces, axis=1).astype(o_ref.dtype)


def _attention(nsa, q_rot, k_cmp, v_cmp_t, ks, vs_t, kw, vw_t, ngate, ov_t, bsz, seq):
    qn = ATT_Q
    gw = NSA_GROUP * NSA_DH
    n_slab = k_cmp.shape[2]
    n_sel = seq // SEL_BLOCK
    qspec = pl.BlockSpec((1, qn, gw), lambda b, g, i: (b, i, g))
    per_group = lambda a: pl.BlockSpec((1, 1) + a.shape[2:], lambda b, g, i: (b, g, 0, 0))
    return pl.pallas_call(
        functools.partial(_attn_kernel, seq=seq),
        grid=(bsz, NSA_KV, seq // qn),
        in_specs=[qspec, qspec, per_group(k_cmp), per_group(v_cmp_t), per_group(ks), per_group(vs_t),
                  per_group(kw), per_group(vw_t),
                  pl.BlockSpec((1, qn, LANES), lambda b, g, i: (b, i, 0)),
                  pl.BlockSpec((n_sel, n_slab), lambda b, g, i: (0, 0))],
        out_specs=qspec,
        out_shape=jax.ShapeDtypeStruct((bsz, seq, NSA_WIDTH), BF16),
        scratch_shapes=[pltpu.VMEM((LANES, qn), F32), pltpu.VMEM((NSA_GROUP * qn, 2 * NSA_DH), BF16)],
        compiler_params=_cparams(("parallel", "parallel", "arbitrary")),
    )(nsa, q_rot, k_cmp, v_cmp_t, ks, vs_t, kw, vw_t, ngate, ov_t)


def _merge_kernel(x_ref, ya_ref, yb_ref, ga_ref, gb_ref, wa_ref, wb_ref, wo_ref, gn_ref, o_ref):
    ga = jax.nn.sigmoid(ga_ref[...].astype(F32))
    gb = jax.nn.sigmoid(gb_ref[...].astype(F32))
    merged = ga * _dot(ya_ref[...], wa_ref[...]) + gb * _dot(yb_ref[...], wb_ref[...])
    mo = _dot(merged.astype(BF16), wo_ref[...])
    o_ref[...] = x_ref[...] + _rms(mo, gn_ref[...])


def _merge(x, ya, yb, gates, wa, wb, wo, gn):
    n, d = x.shape
    tm = min(TOK_TILE, n)
    row = lambda c: pl.BlockSpec((tm, d), lambda i: (i, c))
    wsp = pl.BlockSpec((d, d), lambda i: (0, 0))
    return pl.pallas_call(
        _merge_kernel,
        grid=(n // tm,),
        in_specs=[row(0), row(0), row(0), row(0), row(1), wsp, wsp, wsp,
                  pl.BlockSpec((1, d), lambda i: (0, 0))],
        out_specs=row(0),
        out_shape=jax.ShapeDtypeStruct((n, d), F32),
        compiler_params=_cparams(("parallel",)),
    )(x, ya, yb, gates, gates, wa, wb, wo, gn)


def _mlp_kernel(h_ref, gpre_ref, wu_ref, wd_ref, gpost_ref, o_ref, v_scr, acc_scr):
    j = pl.program_id(1)

    @pl.when(j == 0)
    def _():
        v_scr[...] = _rms(h_ref[...], gpre_ref[...]).astype(BF16)
        acc_scr[...] = jnp.zeros_like(acc_scr)

    up = _dot(v_scr[...], wu_ref[...])
    act = jnp.square(jnp.maximum(up, 0.0))
    acc_scr[...] += _dot(act.astype(BF16), wd_ref[...])

    @pl.when(j == pl.num_programs(1) - 1)
    def _():
        o_ref[...] = h_ref[...] + _rms(acc_scr[...], gpost_ref[...])


def _mlp(h, gpre, wu, wd, gpost):
    n, d = h.shape
    dff = wu.shape[1]
    tm = min(2 * TOK_TILE, n)
    tf = 1024
    vec = pl.BlockSpec((1, d), lambda i, j: (0, 0))
    return pl.pallas_call(
        _mlp_kernel,
        grid=(n // tm, dff // tf),
        in_specs=[pl.BlockSpec((tm, d), lambda i, j: (i, 0)), vec,
                  pl.BlockSpec((d, tf), lambda i, j: (0, j)),
                  pl.BlockSpec((tf, d), lambda i, j: (j, 0)), vec],
        out_specs=pl.BlockSpec((tm, d), lambda i, j: (i, 0)),
        out_shape=jax.ShapeDtypeStruct((n, d), F32),
        scratch_shapes=[pltpu.VMEM((tm, d), BF16), pltpu.VMEM((tm, d), F32)],
        compiler_params=_cparams(("parallel", "arbitrary")),
    )(h, gpre, wu, wd, gpost)


def _ple_kernel(h_ref, p_ref, wp_ref, wg_ref, gn_ref, o_ref):
    h = h_ref[...]
    e = _dot(p_ref[...].astype(BF16), wp_ref[...]) * jax.nn.sigmoid(_dot(h.astype(BF16), wg_ref[...]))
    o_ref[...] = h + _rms(e, gn_ref[...])


def _ple(h, p, wp, wg, gn):
    n, d = h.shape
    pd = p.shape[1]
    tm = min(TOK_TILE, n)
    return pl.pallas_call(
        _ple_kernel,
        grid=(n // tm,),
        in_specs=[pl.BlockSpec((tm, d), lambda i: (i, 0)),
                  pl.BlockSpec((tm, pd), lambda i: (i, 0)),
                  pl.BlockSpec((pd, d), lambda i: (0, 0)),
                  pl.BlockSpec((d, d), lambda i: (0, 0)),
                  pl.BlockSpec((1, d), lambda i: (0, 0))],
        out_specs=pl.BlockSpec((tm, d), lambda i: (i, 0)),
        out_shape=jax.ShapeDtypeStruct((n, d), F32),
        compiler_params=_cparams(("parallel",)),
    )(h, p, wp, wg, gn)


def _overlap_t(seq):
    n_slab = seq // CMP_STRIDE
    n_cmp = (seq - CMP_LEN) // CMP_STRIDE + 1
    n_sel = seq // SEL_BLOCK
    start = np.arange(n_slab) * CMP_STRIDE
    end = start + CMP_LEN - 1
    sel_start = np.arange(n_sel) * SEL_BLOCK
    ov = (start[None, :] < sel_start[:, None] + SEL_BLOCK) & (end[None, :] >= sel_start[:, None])
    ov = ov & (np.arange(n_slab)[None, :] < n_cmp)
    return jnp.asarray(ov.astype(np.float32), BF16)


def kernel(x, p, w_in, w_branch_a, w_branch_b, w_out, norm_pre_mix, norm_post_mix, norm_pre_mlp,
           norm_post_mlp, hg_lb_logits, hg_gnorm, cmp_pe_k, cmp_pe_v, cmp_wk1, cmp_wk2, cmp_wv1, cmp_wv2,
           w_up, w_down, w_ple, w_ple_gate, norm_ple):
    bsz, seq, d = x.shape
    depth = w_in.shape[0]
    assert depth == 1 and hg_lb_logits.shape[0] == 2 and d == D_MODEL
    assert seq % ATT_KT == 0 and seq >= WINDOW + ATT_Q and seq // SEL_BLOCK <= NSA_DH
    assert ATT_Q in (SEL_BLOCK, 2 * SEL_BLOCK)
    n = bsz * seq
    x2 = x.reshape(n, d)
    row = lambda a: a.reshape(1, -1)

    w = w_in[0]
    o_hq, o_hf, o_hi, o_hg = 0, HG_WIDTH, 2 * HG_WIDTH, 3 * HG_WIDTH
    o_nq = 4 * HG_WIDTH
    o_kv = o_nq + NSA_WIDTH
    o_gate = o_kv + 6 * NSA_KVW
    o_ga = o_gate + 3 * NSA_HEADS
    col = lambda a, b: w[:, a:b]
    w_h3 = jnp.concatenate([col(o_hq, o_hf), col(o_hi, o_hg), col(o_hg, o_nq)], axis=1).astype(BF16)
    w_hf = col(o_hf, o_hi).astype(BF16)
    w_nsa = col(o_nq, o_gate).astype(BF16)
    w_ng = jnp.pad(col(o_gate, o_ga), ((0, 0), (0, LANES - 3 * NSA_HEADS))).astype(BF16)
    w_g = col(o_ga, o_ga + 2 * D_MODEL).astype(BF16)
    g_pre = row(norm_pre_mix[0])
    h3 = _rms_matmul(x2, g_pre, w_h3, BF16, 1024).reshape(bsz, seq, -1)
    hf = _rms_matmul(x2, g_pre, w_hf, F32, 1024).reshape(bsz, seq, -1)
    nsa = _rms_matmul(x2, g_pre, w_nsa, BF16, 512).reshape(bsz, seq, -1)
    ngate = _rms_matmul(x2, g_pre, w_ng, F32, LANES).reshape(bsz, seq, -1)
    gates = _rms_matmul(x2, g_pre, w_g, BF16, 1024)

    y_a = _hgrn(h3, hf, hg_lb_logits, row(hg_gnorm[0]), bsz, seq)

    tables = _rope_tables(seq)
    kvb = NSA_WIDTH // NSA_KVW
    q_rot = _rope(nsa, 0, NSA_WIDTH, tables, ATT_SCALE * LOG2E, bsz, seq)
    ks_rot = _rope(nsa, kvb + 2, NSA_KVW, tables, 1.0, bsz, seq)
    kw_rot = _rope(nsa, kvb + 4, NSA_KVW, tables, 1.0, bsz, seq)
    kv_slice = lambda i: nsa[:, :, NSA_WIDTH + i * NSA_KVW: NSA_WIDTH + (i + 1) * NSA_KVW]
    heads = lambda t: t.reshape(bsz, seq, NSA_KV, NSA_DH).transpose(0, 2, 1, 3)
    heads_t = lambda t: t.reshape(bsz, seq, NSA_KV, NSA_DH).transpose(0, 2, 3, 1)
    slabs = lambda t: heads(t).reshape(bsz, NSA_KV, seq // CMP_STRIDE, CMP_STRIDE * NSA_DH)
    k_cmp, v_cmp = _compress(
        slabs(kv_slice(0)), slabs(kv_slice(1)), cmp_pe_k[0].reshape(1, -1), cmp_pe_v[0].reshape(1, -1),
        cmp_wk1[0].astype(BF16), cmp_wk2[0].astype(BF16), cmp_wv1[0].astype(BF16), cmp_wv2[0].astype(BF16))
    onehot = jnp.asarray(np.arange(seq)[:, None] // SEL_BLOCK == np.arange(NSA_DH)[None, :], BF16)
    ks_aug = jnp.concatenate([heads(ks_rot), jnp.broadcast_to(onehot, (bsz, NSA_KV, seq, NSA_DH))], axis=3)
    ones_rows = jnp.ones((bsz, NSA_KV, SUBLANES_BF16, seq), BF16)
    with_ones = lambda t: jnp.concatenate([heads_t(t), ones_rows], axis=2)
    y_b = _attention(nsa, q_rot, k_cmp, v_cmp.transpose(0, 1, 3, 2), ks_aug, with_ones(kv_slice(3)),
                     heads(kw_rot), with_ones(kv_slice(5)), ngate, _overlap_t(seq), bsz, seq)

    h1 = _merge(x2, y_a.reshape(n, -1), y_b.reshape(n, -1), gates, w_branch_a[0].astype(BF16),
                w_branch_b[0].astype(BF16), w_out[0].astype(BF16), row(norm_post_mix[0]))
    h2 = _mlp(h1, row(norm_pre_mlp[0]), w_up[0].astype(BF16), w_down[0].astype(BF16), row(norm_post_mlp[0]))
    h3o = _ple(h2, p[0].reshape(n, -1), w_ple[0].astype(BF16), w_ple_gate[0].astype(BF16), row(norm_ple[0]))
    return h3o.reshape(bsz, seq, d)
```

```python
import functools

import numpy as np
import jax
import jax.numpy as jnp
from jax import lax
from jax.experimental import pallas as pl
from jax.experimental.pallas import tpu as pltpu

F32 = jnp.float32
BF16 = jnp.bfloat16

D_MODEL = 1024
HG_HEADS = 8
HG_DK = 128
HG_DV = 128
HG_WIDTH = HG_HEADS * HG_DK
HG_CHUNK = 64
NSA_HEADS = 16
NSA_KV = 4
NSA_GROUP = NSA_HEADS // NSA_KV
NSA_DH = 64
NSA_WIDTH = NSA_HEADS * NSA_DH
NSA_KVW = NSA_KV * NSA_DH
CMP_LEN = 32
CMP_STRIDE = 16
CMP_HIDDEN = 256
SEL_BLOCK = 64
SEL_TOPK = 16
WINDOW = 512
ROPE_THETA = 500000.0
ROPE_DIM = NSA_DH // 4
D_FF = 4 * D_MODEL
EPS = 1e-6
NEG = -1e30
ATT_SCALE = NSA_DH ** -0.5
LOG2E = float(np.log2(np.e))

LANES = 128
SUBLANES_BF16 = 16
VMEM_LIMIT = 56 * 1024 * 1024

TOK_TILE = 512
ATT_Q = 128
ATT_KT = 512
ATT_SUBTILES = 4
HG_TOK = 512

P_HQ, P_HI, P_HG, P_HF = 0, HG_WIDTH, 2 * HG_WIDTH, 3 * HG_WIDTH
P_NQ = 4 * HG_WIDTH
P_KV = P_NQ + NSA_WIDTH
P_NG = P_KV + 6 * NSA_KVW
P_GA = 7 * D_MODEL
P_GB = 8 * D_MODEL
P_WIDTH = 9 * D_MODEL
HG_EXP_CLAMP = 80.0


def _cparams(sem):
    return pltpu.CompilerParams(dimension_semantics=sem, vmem_limit_bytes=VMEM_LIMIT)


def _rms(x, gain):
    ms = jnp.mean(x * x, axis=-1, keepdims=True)
    return (x * lax.rsqrt(ms + EPS)) * gain


def _dot(a, b):
    return jnp.dot(a, b, preferred_element_type=F32)


def _dot_nt(a, b):
    return lax.dot_general(a, b, (((1,), (1,)), ((), ())), preferred_element_type=F32)


def _dot_tn(a, b):
    return lax.dot_general(a, b, (((0,), (0,)), ((), ())), preferred_element_type=F32)


def _colmax(s):
    parts = [s[i:i + 64] for i in range(0, s.shape[0], 64)] if s.shape[0] % 64 == 0 else [s]
    while len(parts) > 1:
        parts = [jnp.maximum(parts[i], parts[i + 1]) if i + 1 < len(parts) else parts[i]
                 for i in range(0, len(parts), 2)]
    return jnp.max(parts[0], axis=0, keepdims=True)


def _split3(x):
    hi = x.astype(BF16)
    r1 = x - hi.astype(F32)
    mid = r1.astype(BF16)
    lo = (r1 - mid.astype(F32)).astype(BF16)
    return hi, mid, lo


def _rms_matmul_kernel(x_ref, g_ref, w_ref, o_ref, u_scr):
    @pl.when(pl.program_id(1) == 0)
    def _():
        u_scr[...] = _rms(x_ref[...], g_ref[...]).astype(BF16)

    o_ref[...] = _dot(u_scr[...], w_ref[...]).astype(o_ref.dtype)


def _rms_matmul(x, gain, w, out_dtype, tn):
    n, d = x.shape
    wd = w.shape[1]
    tm = min(2 * TOK_TILE, n)
    return pl.pallas_call(
        _rms_matmul_kernel,
        grid=(n // tm, wd // tn),
        in_specs=[pl.BlockSpec((tm, d), lambda i, j: (i, 0)),
                  pl.BlockSpec((1, d), lambda i, j: (0, 0)),
                  pl.BlockSpec((d, tn), lambda i, j: (0, j))],
        out_specs=pl.BlockSpec((tm, tn), lambda i, j: (i, j)),
        out_shape=jax.ShapeDtypeStruct((n, wd), out_dtype),
        scratch_shapes=[pltpu.VMEM((tm, d), BF16)],
        compiler_params=_cparams(("parallel", "arbitrary")),
    )(x, gain, w)


def _hgrn_kernel(q_ref, i_ref, g_ref, f_ref, lbl_ref, gn_ref, tri_ref, o_ref, st_ref, *, n_chunks):
    c_len = HG_CHUNK
    half = c_len // 2

    @pl.when(pl.program_id(2) == 0)
    def _():
        st_ref[...] = jnp.zeros_like(st_ref)

    lg = lbl_ref[...]
    mx = jnp.max(lg, axis=0, keepdims=True)
    ex = jnp.exp(lg - mx)
    lb = ex[0:1, :] / jnp.sum(ex, axis=0, keepdims=True)

    row = lax.broadcasted_iota(jnp.int32, (c_len, c_len), 0)
    col = lax.broadcasted_iota(jnp.int32, (c_len, c_len), 1)
    causal = col <= row
    tri3 = tri_ref[...]

    chunks = [slice(c * c_len, (c + 1) * c_len) for c in range(n_chunks)]
    qf, kk, lf = [], [], []
    for rows in chunks:
        qx = q_ref[0, rows, :].astype(F32)
        qf.append(qx * jax.nn.sigmoid(qx) * (HG_DK ** -0.5))
        forget = lb + (1.0 - lb) * jax.nn.sigmoid(f_ref[0, rows, :].astype(F32))
        kk.append(1.0 - forget)
        lf.append(jnp.log(forget))
    bb = [_dot(tri3, jnp.concatenate(_split3(x), axis=0)) for x in lf]
    qe, ke, qb, kd, dlast = [], [], [], [], []
    for c in range(n_chunks):
        b = bb[c]
        b_mid = b[half - 1:half, :]
        b_last = b[c_len - 1:c_len, :]
        qe.append((qf[c] * jnp.exp(jnp.minimum(b - b_mid, HG_EXP_CLAMP))).astype(BF16))
        ke.append((kk[c] * jnp.exp(jnp.minimum(b_mid - b, HG_EXP_CLAMP))).astype(BF16))
        qb.append((qf[c] * jnp.exp(b)).astype(BF16))
        kd.append((kk[c] * jnp.exp(b_last - b)).astype(BF16))
        dlast.append(jnp.exp(b_last))
    vv = [i_ref[0, rows, :] for rows in chunks]
    aa = [_dot_nt(qe[c], ke[c]) for c in range(n_chunks)]
    kv = [_dot_tn(vv[c], kd[c]) for c in range(n_chunks)]
    o_intra = [_dot(jnp.where(causal, aa[c], 0.0).astype(BF16), vv[c]) for c in range(n_chunks)]
    states = [st_ref[...]]
    for c in range(n_chunks):
        states.append(states[c] * dlast[c] + kv[c])
    o_inter = [_dot_nt(qb[c], states[c].astype(BF16)) for c in range(n_chunks)]
    st_ref[...] = states[n_chunks]
    for c, rows in enumerate(chunks):
        gx = g_ref[0, rows, :].astype(F32)
        y = _rms(o_intra[c] + o_inter[c], gn_ref[...]) * (gx * jax.nn.sigmoid(gx))
        o_ref[0, rows, :] = y.astype(o_ref.dtype)


def _hgrn(proj, lb_logits, gnorm, bsz, seq):
    n_chunks = HG_TOK // HG_CHUNK
    tri = np.tril(np.ones((HG_CHUNK, HG_CHUNK), np.float32))
    tri3 = jnp.asarray(np.concatenate([tri, tri, tri], axis=1), BF16)
    hh = HG_HEADS
    blk = (1, HG_TOK, HG_DK)
    head_cols = lambda off: pl.BlockSpec(blk, lambda b, h, t: (b, t, off // HG_DK + h))
    return pl.pallas_call(
        functools.partial(_hgrn_kernel, n_chunks=n_chunks),
        grid=(bsz, hh, seq // HG_TOK),
        in_specs=[head_cols(P_HQ), head_cols(P_HI), head_cols(P_HG), head_cols(P_HF),
                  pl.BlockSpec((2, HG_DK), lambda b, h, t: (0, h)),
                  pl.BlockSpec((1, HG_DV), lambda b, h, t: (0, 0)),
                  pl.BlockSpec((HG_CHUNK, 3 * HG_CHUNK), lambda b, h, t: (0, 0))],
        out_specs=pl.BlockSpec(blk, lambda b, h, t: (b, t, h)),
        out_shape=jax.ShapeDtypeStruct((bsz, seq, HG_WIDTH), BF16),
        scratch_shapes=[pltpu.VMEM((HG_DV, HG_DK), F32)],
        compiler_params=_cparams(("parallel", "parallel", "arbitrary")),
    )(proj, proj, proj, proj, lb_logits, gnorm, tri3)


def _rope_kernel(x_ref, c_ref, s1_ref, s2_ref, o_ref, *, scale):
    width = x_ref.shape[-1]
    c = c_ref[...]
    s1 = s1_ref[...]
    s2 = s2_ref[...]
    for j in range(width // LANES):
        cols = slice(j * LANES, (j + 1) * LANES)
        x = x_ref[0, :, cols].astype(F32)
        if scale != 1.0:
            x = x * scale
        y = x * c + pltpu.roll(x, LANES - ROPE_DIM // 2, 1) * s1 + pltpu.roll(x, ROPE_DIM // 2, 1) * s2
        o_ref[0, :, cols] = y.astype(o_ref.dtype)


def _rope(src, col_block, width, tables, scale, bsz, seq):
    tt = min(TOK_TILE, seq)
    tab = pl.BlockSpec((tt, LANES), lambda b, t: (t, 0))
    return pl.pallas_call(
        functools.partial(_rope_kernel, scale=scale),
        grid=(bsz, seq // tt),
        in_specs=[pl.BlockSpec((1, tt, width), lambda b, t: (b, t, col_block)), tab, tab, tab],
        out_specs=pl.BlockSpec((1, tt, width), lambda b, t: (b, t, 0)),
        out_shape=jax.ShapeDtypeStruct((bsz, seq, width), BF16),
        compiler_params=_cparams(("parallel", "parallel")),
    )(src, *tables)


def _rope_tables(seq):
    half = ROPE_DIM // 2
    inv = jnp.asarray(ROPE_THETA ** (-np.arange(half) * 2.0 / ROPE_DIM), F32)
    ang = jnp.arange(seq, dtype=F32)[:, None] * inv[None, :]
    cos, sin = jnp.cos(ang), jnp.sin(ang)
    lane = np.arange(LANES) % NSA_DH
    pick = lane % half
    cos_l = cos[:, pick]
    sin_l = sin[:, pick]
    first = jnp.asarray(lane < half)
    second = jnp.asarray((lane >= half) & (lane < ROPE_DIM))
    c = jnp.where(first | second, cos_l, 1.0)
    s1 = jnp.where(first, -sin_l, 0.0)
    s2 = jnp.where(second, sin_l, 0.0)
    return c, s1, s2


def _gelu_tanh(x):
    return 0.5 * x * (1.0 + jnp.tanh(np.sqrt(2.0 / np.pi).astype(np.float32) * (x + 0.044715 * (x * x * x))))


def _compress_kernel(xk_ref, xv_ref, pek_ref, pev_ref, wk1_ref, wk2_ref, wv1_ref, wv2_ref, ok_ref, ov_ref):
    half_w = CMP_STRIDE * NSA_DH

    def one(x_ref, pe_ref, w1_ref, w2_ref):
        x = x_ref[0, 0]
        n_slab = x.shape[0]
        p0 = _dot(x, w1_ref[0:half_w, :])
        p1 = _dot(x, w1_ref[half_w:2 * half_w, :])
        pe = jnp.broadcast_to(pe_ref[...], (8, 2 * half_w)).astype(BF16)
        pw = _dot(pe, w1_ref[...])[0:1, :]
        h = p0 + pltpu.roll(p1, n_slab - 1, 0) + pw
        return _dot(_gelu_tanh(h).astype(BF16), w2_ref[...])

    ok_ref[0, 0] = one(xk_ref, pek_ref, wk1_ref, wk2_ref).astype(ok_ref.dtype)
    ov_ref[0, 0] = one(xv_ref, pev_ref, wv1_ref, wv2_ref).astype(ov_ref.dtype)


def _compress(xk, xv, pek, pev, wk1, wk2, wv1, wv2):
    bsz, grp, n_slab, wdt = xk.shape
    xs = pl.BlockSpec((1, 1, n_slab, wdt), lambda b, g: (b, g, 0, 0))
    full = lambda a: pl.BlockSpec(a.shape, lambda b, g: (0,) * a.ndim)
    osd = jax.ShapeDtypeStruct((bsz, grp, n_slab, NSA_DH), BF16)
    os_ = pl.BlockSpec((1, 1, n_slab, NSA_DH), lambda b, g: (b, g, 0, 0))
    return pl.pallas_call(
        _compress_kernel,
        grid=(bsz, grp),
        in_specs=[xs, xs, full(pek), full(pev), full(wk1), full(wk2), full(wv1), full(wv2)],
        out_specs=[os_, os_],
        out_shape=[osd, osd],
        compiler_params=_cparams(("parallel", "parallel")),
    )(xk, xv, pek, pev, wk1, wk2, wv1, wv2)


def _attn_kernel(nq_ref, qr_ref, kcmp_ref, vcmpt_ref, ks_ref, vst_ref, kw_ref, vwt_ref, ng_ref, ovt_ref,
                 o_ref, gt_scr, qaug_scr, *, seq):
    qn = ATT_Q
    grp = NSA_GROUP
    ln = grp * qn
    n_sel = seq // SEL_BLOCK
    top = min(SEL_TOPK, n_sel)
    g_id = pl.program_id(1)
    q0 = pl.multiple_of(pl.program_id(2) * qn, qn)

    def stack_heads(x):
        return jnp.concatenate([x[:, r * NSA_DH:(r + 1) * NSA_DH] for r in range(grp)], axis=0)

    def add_per_head(s, bias):
        return jnp.concatenate([s[:, r * qn:(r + 1) * qn] + bias for r in range(grp)], axis=1)

    qs = stack_heads(nq_ref[0])
    qrs = stack_heads(qr_ref[0])
    t_lane = q0 + (lax.broadcasted_iota(jnp.int32, (1, ln), 1) & (qn - 1))
    t_q = q0 + lax.broadcasted_iota(jnp.int32, (1, qn), 1)

    n_slab = kcmp_ref.shape[2]
    sc = _dot_nt(kcmp_ref[0, 0], qs) * ATT_SCALE
    cend = lax.broadcasted_iota(jnp.int32, (n_slab, 1), 0) * CMP_STRIDE + (CMP_LEN - 1)
    sm = add_per_head(sc, jnp.where(cend <= t_q, 0.0, NEG))
    e = jnp.exp(sm - _colmax(sm))
    den = jnp.sum(e, axis=0, keepdims=True)
    p = e * jnp.where(t_lane >= CMP_LEN - 1, 1.0 / den, 0.0)
    oc = _dot(vcmpt_ref[0, 0], p.astype(BF16))

    psum = p[:, 0:qn]
    for r in range(1, grp):
        psum = psum + p[:, r * qn:(r + 1) * qn]
    hi, mid, lo = _split3(psum)
    ovt = ovt_ref[...]
    imp = _dot(ovt, hi) + _dot(ovt, mid) + _dot(ovt, lo)
    jidx = lax.broadcasted_iota(jnp.int32, (n_sel, qn), 0)
    tq = q0 + lax.broadcasted_iota(jnp.int32, (n_sel, qn), 1)
    cur = tq // SEL_BLOCK
    forced = (jidx == 0) | (jidx == cur) | (jidx == cur - 1)
    causal_blk = jidx * SEL_BLOCK <= tq
    val = jnp.where(forced, -2.0, imp)
    val = jnp.where(causal_blk, val, -1.0)
    sel = jnp.where(forced, 1.0, 0.0)
    for _ in range(max(top - 3, 0)):
        mx = jnp.max(val, axis=0, keepdims=True)
        first = jnp.min(jnp.where(val == mx, jidx, n_sel), axis=0, keepdims=True)
        hit = jidx == first
        sel = jnp.where(hit, 1.0, sel)
        val = jnp.where(hit, -2.0, val)
    diag_blk = q0 // SEL_BLOCK
    in_diag = (jidx >= diag_blk) & (jidx < diag_blk + qn // SEL_BLOCK)
    selb = jnp.where((sel > 0.0) & causal_blk & jnp.logical_not(in_diag), 0.0, NEG)
    selb_t = jnp.concatenate([selb, jnp.zeros((LANES - n_sel, qn), F32)], axis=0).T
    selb_t = selb_t[:, 0:NSA_DH].astype(BF16)
    qaug_scr[...] = jnp.concatenate([qrs, jnp.concatenate([selb_t] * grp, axis=0)], axis=1)

    wlen = WINDOW + qn
    w0 = pl.multiple_of(jnp.maximum(q0 - WINDOW, 0), qn)
    s = _dot_nt(kw_ref[0, 0, pl.ds(w0, wlen), :], qrs)
    diff = t_q - (w0 + lax.broadcasted_iota(jnp.int32, (wlen, 1), 0))
    s = add_per_head(s, jnp.where((diff >= 0) & (diff < WINDOW), 0.0, NEG))
    pw = jnp.exp2(s - _colmax(s))
    rw = _dot(vwt_ref[0, 0, :, pl.ds(w0, wlen)], pw.astype(BF16))
    ow = rw[0:NSA_DH] / rw[NSA_DH:NSA_DH + 1]

    s = _dot_nt(ks_ref[0, 0, pl.ds(q0, qn), :][:, 0:NSA_DH], qrs)
    kpos = q0 + lax.broadcasted_iota(jnp.int32, (qn, 1), 0)
    s = add_per_head(s, jnp.where(kpos <= t_q, 0.0, NEG))
    m0 = _colmax(s)
    acc0 = _dot(vst_ref[0, 0, :, pl.ds(q0, qn)], jnp.exp2(s - m0).astype(BF16))

    kt_len = ATT_KT
    sub = kt_len // ATT_SUBTILES

    def sweep_step(kt, carry):
        m_run, acc = carry
        k0s = [pl.multiple_of(kt * kt_len + j * sub, sub) for j in range(ATT_SUBTILES)]
        scores = [_dot_nt(ks_ref[0, 0, pl.ds(k0, sub), :], qaug_scr[...]) for k0 in k0s]
        for k0, s in zip(k0s, scores):
            m_new = jnp.maximum(m_run, _colmax(s))
            alpha = jnp.exp2(m_run - m_new)
            pt = jnp.exp2(s - m_new).astype(BF16)
            acc = alpha * acc + _dot(vst_ref[0, 0, :, pl.ds(k0, sub)], pt)
            m_run = m_new
        return m_run, acc

    n_tiles = (q0 + kt_len - 1) // kt_len
    _, acc = lax.fori_loop(0, n_tiles, sweep_step, (m0, acc0))
    osel = acc[0:NSA_DH] / acc[NSA_DH:NSA_DH + 1]

    gt_scr[...] = jax.nn.sigmoid(ng_ref[0].astype(F32)).T

    def gate_row(branch):
        return jnp.concatenate(
            [gt_scr[pl.ds(branch * NSA_HEADS + g_id * grp + r, 1), :] for r in range(grp)], axis=1)

    out_t = gate_row(0) * oc + gate_row(1) * osel + gate_row(2) * ow
    pieces = []
    for r2 in range(grp // 2):
        pair = jnp.concatenate([out_t[:, (2 * r2) * qn:(2 * r2 + 1) * qn],
                                out_t[:, (2 * r2 + 1) * qn:(2 * r2 + 2) * qn]], axis=0)
        pieces.append(pair.T)
    o_ref[0] = jnp.concatenate(pieces, axis=1).astype(o_ref.dtype)


def _attention(proj, q_rot, k_cmp, v_cmp_t, ks, vs_t, kw, vw_t, ov_t, bsz, seq):
    qn = ATT_Q
    gw = NSA_GROUP * NSA_DH
    n_slab = k_cmp.shape[2]
    n_sel = seq // SEL_BLOCK
    qspec = pl.BlockSpec((1, qn, gw), lambda b, g, i: (b, i, g))
    per_group = lambda a: pl.BlockSpec((1, 1) + a.shape[2:], lambda b, g, i: (b, g, 0, 0))
    return pl.pallas_call(
        functools.partial(_attn_kernel, seq=seq),
        grid=(bsz, NSA_KV, seq // qn),
        in_specs=[pl.BlockSpec((1, qn, gw), lambda b, g, i: (b, i, P_NQ // gw + g)), qspec,
                  per_group(k_cmp), per_group(v_cmp_t), per_group(ks), per_group(vs_t),
                  per_group(kw), per_group(vw_t),
                  pl.BlockSpec((1, qn, LANES), lambda b, g, i: (b, i, P_NG // LANES)),
                  pl.BlockSpec((n_sel, n_slab), lambda b, g, i: (0, 0))],
        out_specs=qspec,
        out_shape=jax.ShapeDtypeStruct((bsz, seq, NSA_WIDTH), BF16),
        scratch_shapes=[pltpu.VMEM((LANES, qn), F32), pltpu.VMEM((NSA_GROUP * qn, 2 * NSA_DH), BF16)],
        compiler_params=_cparams(("parallel", "parallel", "arbitrary")),
    )(proj, q_rot, k_cmp, v_cmp_t, ks, vs_t, kw, vw_t, proj, ov_t)


def _merge_kernel(x_ref, ya_ref, yb_ref, ga_ref, gb_ref, wa_ref, wb_ref, wo_ref, gn_ref, o_ref):
    ga = jax.nn.sigmoid(ga_ref[...].astype(F32))
    gb = jax.nn.sigmoid(gb_ref[...].astype(F32))
    merged = ga * _dot(ya_ref[...], wa_ref[...]) + gb * _dot(yb_ref[...], wb_ref[...])
    mo = _dot(merged.astype(BF16), wo_ref[...])
    o_ref[...] = x_ref[...] + _rms(mo, gn_ref[...])


def _merge(x, ya, yb, gates, wa, wb, wo, gn):
    n, d = x.shape
    tm = min(TOK_TILE, n)
    row = lambda c: pl.BlockSpec((tm, d), lambda i: (i, c))
    wsp = pl.BlockSpec((d, d), lambda i: (0, 0))
    return pl.pallas_call(
        _merge_kernel,
        grid=(n // tm,),
        in_specs=[row(0), row(0), row(0), row(P_GA // d), row(P_GB // d), wsp, wsp, wsp,
                  pl.BlockSpec((1, d), lambda i: (0, 0))],
        out_specs=row(0),
        out_shape=jax.ShapeDtypeStruct((n, d), F32),
        compiler_params=_cparams(("parallel",)),
    )(x, ya, yb, gates, gates, wa, wb, wo, gn)


def _mlp_kernel(h_ref, gpre_ref, wu_ref, wd_ref, gpost_ref, o_ref, v_scr, acc_scr):
    j = pl.program_id(1)

    @pl.when(j == 0)
    def _():
        v_scr[...] = _rms(h_ref[...], gpre_ref[...]).astype(BF16)
        acc_scr[...] = jnp.zeros_like(acc_scr)

    up = _dot(v_scr[...], wu_ref[...])
    act = jnp.square(jnp.maximum(up, 0.0))
    acc_scr[...] += _dot(act.astype(BF16), wd_ref[...])

    @pl.when(j == pl.num_programs(1) - 1)
    def _():
        o_ref[...] = h_ref[...] + _rms(acc_scr[...], gpost_ref[...])


def _mlp(h, gpre, wu, wd, gpost):
    n, d = h.shape
    dff = wu.shape[1]
    tm = min(2 * TOK_TILE, n)
    tf = 1024
    vec = pl.BlockSpec((1, d), lambda i, j: (0, 0))
    return pl.pallas_call(
        _mlp_kernel,
        grid=(n // tm, dff // tf),
        in_specs=[pl.BlockSpec((tm, d), lambda i, j: (i, 0)), vec,
                  pl.BlockSpec((d, tf), lambda i, j: (0, j)),
                  pl.BlockSpec((tf, d), lambda i, j: (j, 0)), vec],
        out_specs=pl.BlockSpec((tm, d), lambda i, j: (i, 0)),
        out_shape=jax.ShapeDtypeStruct((n, d), F32),
        scratch_shapes=[pltpu.VMEM((tm, d), BF16), pltpu.VMEM((tm, d), F32)],
        compiler_params=_cparams(("parallel", "arbitrary")),
    )(h, gpre, wu, wd, gpost)


def _ple_kernel(h_ref, p_ref, wp_ref, wg_ref, gn_ref, o_ref):
    h = h_ref[...]
    e = _dot(p_ref[...].astype(BF16), wp_ref[...]) * jax.nn.sigmoid(_dot(h.astype(BF16), wg_ref[...]))
    o_ref[...] = h + _rms(e, gn_ref[...])


def _ple(h, p, wp, wg, gn):
    n, d = h.shape
    pd = p.shape[1]
    tm = min(TOK_TILE, n)
    return pl.pallas_call(
        _ple_kernel,
        grid=(n // tm,),
        in_specs=[pl.BlockSpec((tm, d), lambda i: (i, 0)),
                  pl.BlockSpec((tm, pd), lambda i: (i, 0)),
                  pl.BlockSpec((pd, d), lambda i: (0, 0)),
                  pl.BlockSpec((d, d), lambda i: (0, 0)),
                  pl.BlockSpec((1, d), lambda i: (0, 0))],
        out_specs=pl.BlockSpec((tm, d), lambda i: (i, 0)),
        out_shape=jax.ShapeDtypeStruct((n, d), F32),
        compiler_params=_cparams(("parallel",)),
    )(h, p, wp, wg, gn)


def _overlap_t(seq):
    n_slab = seq // CMP_STRIDE
    n_cmp = (seq - CMP_LEN) // CMP_STRIDE + 1
    n_sel = seq // SEL_BLOCK
    start = np.arange(n_slab) * CMP_STRIDE
    end = start + CMP_LEN - 1
    sel_start = np.arange(n_sel) * SEL_BLOCK
    ov = (start[None, :] < sel_start[:, None] + SEL_BLOCK) & (end[None, :] >= sel_start[:, None])
    ov = ov & (np.arange(n_slab)[None, :] < n_cmp)
    return jnp.asarray(ov.astype(np.float32), BF16)


def kernel(x, p, w_in, w_branch_a, w_branch_b, w_out, norm_pre_mix, norm_post_mix, norm_pre_mlp,
           norm_post_mlp, hg_lb_logits, hg_gnorm, cmp_pe_k, cmp_pe_v, cmp_wk1, cmp_wk2, cmp_wv1, cmp_wv2,
           w_up, w_down, w_ple, w_ple_gate, norm_ple):
    bsz, seq, d = x.shape
    depth = w_in.shape[0]
    assert depth == 1 and hg_lb_logits.shape[0] == 2 and d == D_MODEL
    assert seq % ATT_KT == 0 and seq >= WINDOW + ATT_Q and seq // SEL_BLOCK <= NSA_DH
    assert ATT_Q in (SEL_BLOCK, 2 * SEL_BLOCK)
    n = bsz * seq
    x2 = x.reshape(n, d)
    row = lambda a: a.reshape(1, -1)

    w = w_in[0]
    s_hq, s_hf, s_hi, s_hg = 0, HG_WIDTH, 2 * HG_WIDTH, 3 * HG_WIDTH
    s_nq = 4 * HG_WIDTH
    s_ng = s_nq + NSA_WIDTH + 6 * NSA_KVW
    s_ga = s_ng + 3 * NSA_HEADS
    col = lambda a, b: w[:, a:b]
    w_all = jnp.concatenate(
        [col(s_hq, s_hf), col(s_hi, s_hg), col(s_hg, s_nq), col(s_hf, s_hi), col(s_nq, s_ga),
         jnp.zeros((d, P_GA - P_NG - 3 * NSA_HEADS), F32), col(s_ga, s_ga + 2 * D_MODEL)], axis=1).astype(BF16)
    assert w_all.shape[1] == P_WIDTH
    proj2 = _rms_matmul(x2, row(norm_pre_mix[0]), w_all, BF16, 1024)
    proj = proj2.reshape(bsz, seq, P_WIDTH)

    y_a = _hgrn(proj, hg_lb_logits, row(hg_gnorm[0]), bsz, seq)

    tables = _rope_tables(seq)
    kvb = P_KV // NSA_KVW
    q_rot = _rope(proj, P_NQ // NSA_WIDTH, NSA_WIDTH, tables, ATT_SCALE * LOG2E, bsz, seq)
    ks_rot = _rope(proj, kvb + 2, NSA_KVW, tables, 1.0, bsz, seq)
    kw_rot = _rope(proj, kvb + 4, NSA_KVW, tables, 1.0, bsz, seq)
    kv_slice = lambda i: proj[:, :, P_KV + i * NSA_KVW: P_KV + (i + 1) * NSA_KVW]
    heads = lambda t: t.reshape(bsz, seq, NSA_KV, NSA_DH).transpose(0, 2, 1, 3)
    heads_t = lambda t: t.reshape(bsz, seq, NSA_KV, NSA_DH).transpose(0, 2, 3, 1)
    slabs = lambda t: heads(t).reshape(bsz, NSA_KV, seq // CMP_STRIDE, CMP_STRIDE * NSA_DH)
    k_cmp, v_cmp = _compress(
        slabs(kv_slice(0)), slabs(kv_slice(1)), cmp_pe_k[0].reshape(1, -1), cmp_pe_v[0].reshape(1, -1),
        cmp_wk1[0].astype(BF16), cmp_wk2[0].astype(BF16), cmp_wv1[0].astype(BF16), cmp_wv2[0].astype(BF16))
    onehot = jnp.asarray(np.arange(seq)[:, None] // SEL_BLOCK == np.arange(NSA_DH)[None, :], BF16)
    ks_aug = jnp.concatenate([heads(ks_rot), jnp.broadcast_to(onehot, (bsz, NSA_KV, seq, NSA_DH))], axis=3)
    ones_rows = jnp.ones((bsz, NSA_KV, SUBLANES_BF16, seq), BF16)
    with_ones = lambda t: jnp.concatenate([heads_t(t), ones_rows], axis=2)
    y_b = _attention(proj, q_rot, k_cmp, v_cmp.transpose(0, 1, 3, 2), ks_aug, with_ones(kv_slice(3)),
                     heads(kw_rot), with_ones(kv_slice(5)), _overlap_t(seq), bsz, seq)

    h1 = _merge(x2, y_a.reshape(n, -1), y_b.reshape(n, -1), proj2, w_branch_a[0].astype(BF16),
                w_branch_b[0].astype(BF16), w_out[0].astype(BF16), row(norm_post_mix[0]))
    h2 = _mlp(h1, row(norm_pre_mlp[0]), w_up[0].astype(BF16), w_down[0].astype(BF16), row(norm_post_mlp[0]))
    h3o = _ple(h2, p[0].reshape(n, -1), w_ple[0].astype(BF16), w_ple_gate[0].astype(BF16), row(norm_ple[0]))
    return h3o.reshape(bsz, seq, d)
```

```python
import functools

import numpy as np
import jax
import jax.numpy as jnp
from jax import lax
from jax.experimental import pallas as pl
from jax.experimental.pallas import tpu as pltpu

F32 = jnp.float32
BF16 = jnp.bfloat16

D_MODEL = 1024
HG_HEADS = 8
HG_DK = 128
HG_DV = 128
HG_WIDTH = HG_HEADS * HG_DK
HG_CHUNK = 64
NSA_HEADS = 16
NSA_KV = 4
NSA_GROUP = NSA_HEADS // NSA_KV
NSA_DH = 64
NSA_WIDTH = NSA_HEADS * NSA_DH
NSA_KVW = NSA_KV * NSA_DH
CMP_LEN = 32
CMP_STRIDE = 16
CMP_HIDDEN = 256
SEL_BLOCK = 64
SEL_TOPK = 16
WINDOW = 512
ROPE_THETA = 500000.0
ROPE_DIM = NSA_DH // 4
D_FF = 4 * D_MODEL
EPS = 1e-6
NEG = -1e30
ATT_SCALE = NSA_DH ** -0.5
LOG2E = float(np.log2(np.e))

LANES = 128
SUBLANES_BF16 = 16
VMEM_LIMIT = 56 * 1024 * 1024

TOK_TILE = 512
ATT_Q = 128
ATT_KT = 1024
ATT_SUBTILES = 8
ATT_LOOKAHEAD = 7
HG_TOK = 512

P_HQ, P_HI, P_HG, P_HF = 0, HG_WIDTH, 2 * HG_WIDTH, 3 * HG_WIDTH
P_NQ = 4 * HG_WIDTH
P_KV = P_NQ + NSA_WIDTH
P_NG = P_KV + 6 * NSA_KVW
P_GA = 7 * D_MODEL
P_GB = 8 * D_MODEL
P_WIDTH = 9 * D_MODEL
HG_EXP_CLAMP = 80.0


def _cparams(sem):
    return pltpu.CompilerParams(dimension_semantics=sem, vmem_limit_bytes=VMEM_LIMIT)


def _rms(x, gain):
    ms = jnp.mean(x * x, axis=-1, keepdims=True)
    return (x * lax.rsqrt(ms + EPS)) * gain


def _dot(a, b):
    return jnp.dot(a, b, preferred_element_type=F32)


def _dot_nt(a, b):
    return lax.dot_general(a, b, (((1,), (1,)), ((), ())), preferred_element_type=F32)


def _dot_tn(a, b):
    return lax.dot_general(a, b, (((0,), (0,)), ((), ())), preferred_element_type=F32)


def _colmax(s):
    parts = [s[i:i + 64] for i in range(0, s.shape[0], 64)] if s.shape[0] % 64 == 0 else [s]
    while len(parts) > 1:
        parts = [jnp.maximum(parts[i], parts[i + 1]) if i + 1 < len(parts) else parts[i]
                 for i in range(0, len(parts), 2)]
    return jnp.max(parts[0], axis=0, keepdims=True)


def _split3(x):
    hi = x.astype(BF16)
    r1 = x - hi.astype(F32)
    mid = r1.astype(BF16)
    lo = (r1 - mid.astype(F32)).astype(BF16)
    return hi, mid, lo


def _rms_matmul_kernel(x_ref, g_ref, w_ref, o_ref, u_scr):
    @pl.when(pl.program_id(1) == 0)
    def _():
        u_scr[...] = _rms(x_ref[...], g_ref[...]).astype(BF16)

    o_ref[...] = _dot(u_scr[...], w_ref[...]).astype(o_ref.dtype)


def _rms_matmul(x, gain, w, out_dtype, tn):
    n, d = x.shape
    wd = w.shape[1]
    tm = min(2 * TOK_TILE, n)
    return pl.pallas_call(
        _rms_matmul_kernel,
        grid=(n // tm, wd // tn),
        in_specs=[pl.BlockSpec((tm, d), lambda i, j: (i, 0)),
                  pl.BlockSpec((1, d), lambda i, j: (0, 0)),
                  pl.BlockSpec((d, tn), lambda i, j: (0, j))],
        out_specs=pl.BlockSpec((tm, tn), lambda i, j: (i, j)),
        out_shape=jax.ShapeDtypeStruct((n, wd), out_dtype),
        scratch_shapes=[pltpu.VMEM((tm, d), BF16)],
        compiler_params=_cparams(("parallel", "arbitrary")),
    )(x, gain, w)


def _hgrn_kernel(q_ref, i_ref, g_ref, f_ref, lbl_ref, gn_ref, tri_ref, o_ref, st_ref, *, n_chunks):
    c_len = HG_CHUNK
    half = c_len // 2

    @pl.when(pl.program_id(2) == 0)
    def _():
        st_ref[...] = jnp.zeros_like(st_ref)

    lg = lbl_ref[...]
    mx = jnp.max(lg, axis=0, keepdims=True)
    ex = jnp.exp(lg - mx)
    lb = ex[0:1, :] / jnp.sum(ex, axis=0, keepdims=True)

    row = lax.broadcasted_iota(jnp.int32, (c_len, c_len), 0)
    col = lax.broadcasted_iota(jnp.int32, (c_len, c_len), 1)
    causal = col <= row
    tri3 = tri_ref[...]

    chunks = [slice(c * c_len, (c + 1) * c_len) for c in range(n_chunks)]
    qf, kk, lf = [], [], []
    for rows in chunks:
        qx = q_ref[0, rows, :].astype(F32)
        qf.append(qx * jax.nn.sigmoid(qx) * (HG_DK ** -0.5))
        forget = lb + (1.0 - lb) * jax.nn.sigmoid(f_ref[0, rows, :].astype(F32))
        kk.append(1.0 - forget)
        lf.append(jnp.log(forget))
    bb = [_dot(tri3, jnp.concatenate(_split3(x), axis=0)) for x in lf]
    qe, ke, qb, kd, dlast = [], [], [], [], []
    for c in range(n_chunks):
        b = bb[c]
        b_mid = b[half - 1:half, :]
        b_last = b[c_len - 1:c_len, :]
        qe.append((qf[c] * jnp.exp(jnp.minimum(b - b_mid, HG_EXP_CLAMP))).astype(BF16))
        ke.append((kk[c] * jnp.exp(jnp.minimum(b_mid - b, HG_EXP_CLAMP))).astype(BF16))
        qb.append((qf[c] * jnp.exp(b)).astype(BF16))
        kd.append((kk[c] * jnp.exp(b_last - b)).astype(BF16))
        dlast.append(jnp.exp(b_last))
    vv = [i_ref[0, rows, :] for rows in chunks]
    aa = [_dot_nt(qe[c], ke[c]) for c in range(n_chunks)]
    kv = [_dot_tn(vv[c], kd[c]) for c in range(n_chunks)]
    o_intra = [_dot(jnp.where(causal, aa[c], 0.0).astype(BF16), vv[c]) for c in range(n_chunks)]
    states = [st_ref[...]]
    for c in range(n_chunks):
        states.append(states[c] * dlast[c] + kv[c])
    o_inter = [_dot_nt(qb[c], states[c].astype(BF16)) for c in range(n_chunks)]
    st_ref[...] = states[n_chunks]
    for c, rows in enumerate(chunks):
        gx = g_ref[0, rows, :].astype(F32)
        y = _rms(o_intra[c] + o_inter[c], gn_ref[...]) * (gx * jax.nn.sigmoid(gx))
        o_ref[0, rows, :] = y.astype(o_ref.dtype)


def _hgrn(proj, lb_logits, gnorm, bsz, seq):
    n_chunks = HG_TOK // HG_CHUNK
    tri = np.tril(np.ones((HG_CHUNK, HG_CHUNK), np.float32))
    tri3 = jnp.asarray(np.concatenate([tri, tri, tri], axis=1), BF16)
    hh = HG_HEADS
    blk = (1, HG_TOK, HG_DK)
    head_cols = lambda off: pl.BlockSpec(blk, lambda b, h, t: (b, t, off // HG_DK + h))
    return pl.pallas_call(
        functools.partial(_hgrn_kernel, n_chunks=n_chunks),
        grid=(bsz, hh, seq // HG_TOK),
        in_specs=[head_cols(P_HQ), head_cols(P_HI), head_cols(P_HG), head_cols(P_HF),
                  pl.BlockSpec((2, HG_DK), lambda b, h, t: (0, h)),
                  pl.BlockSpec((1, HG_DV), lambda b, h, t: (0, 0)),
                  pl.BlockSpec((HG_CHUNK, 3 * HG_CHUNK), lambda b, h, t: (0, 0))],
        out_specs=pl.BlockSpec(blk, lambda b, h, t: (b, t, h)),
        out_shape=jax.ShapeDtypeStruct((bsz, seq, HG_WIDTH), BF16),
        scratch_shapes=[pltpu.VMEM((HG_DV, HG_DK), F32)],
        compiler_params=_cparams(("parallel", "parallel", "arbitrary")),
    )(proj, proj, proj, proj, lb_logits, gnorm, tri3)


def _rope_kernel(x_ref, c_ref, s1_ref, s2_ref, o_ref, *, scale):
    width = x_ref.shape[-1]
    c = c_ref[...]
    s1 = s1_ref[...]
    s2 = s2_ref[...]
    for j in range(width // LANES):
        cols = slice(j * LANES, (j + 1) * LANES)
        x = x_ref[0, :, cols].astype(F32)
        if scale != 1.0:
            x = x * scale
        y = x * c + pltpu.roll(x, LANES - ROPE_DIM // 2, 1) * s1 + pltpu.roll(x, ROPE_DIM // 2, 1) * s2
        o_ref[0, :, cols] = y.astype(o_ref.dtype)


def _rope(src, col_block, width, tables, scale, bsz, seq):
    tt = min(TOK_TILE, seq)
    tab = pl.BlockSpec((tt, LANES), lambda b, t: (t, 0))
    return pl.pallas_call(
        functools.partial(_rope_kernel, scale=scale),
        grid=(bsz, seq // tt),
        in_specs=[pl.BlockSpec((1, tt, width), lambda b, t: (b, t, col_block)), tab, tab, tab],
        out_specs=pl.BlockSpec((1, tt, width), lambda b, t: (b, t, 0)),
        out_shape=jax.ShapeDtypeStruct((bsz, seq, width), BF16),
        compiler_params=_cparams(("parallel", "parallel")),
    )(src, *tables)


def _rope_tables(seq):
    half = ROPE_DIM // 2
    inv = jnp.asarray(ROPE_THETA ** (-np.arange(half) * 2.0 / ROPE_DIM), F32)
    ang = jnp.arange(seq, dtype=F32)[:, None] * inv[None, :]
    cos, sin = jnp.cos(ang), jnp.sin(ang)
    lane = np.arange(LANES) % NSA_DH
    pick = lane % half
    cos_l = cos[:, pick]
    sin_l = sin[:, pick]
    first = jnp.asarray(lane < half)
    second = jnp.asarray((lane >= half) & (lane < ROPE_DIM))
    c = jnp.where(first | second, cos_l, 1.0)
    s1 = jnp.where(first, -sin_l, 0.0)
    s2 = jnp.where(second, sin_l, 0.0)
    return c, s1, s2


def _gelu_tanh(x):
    return 0.5 * x * (1.0 + jnp.tanh(np.sqrt(2.0 / np.pi).astype(np.float32) * (x + 0.044715 * (x * x * x))))


def _compress_kernel(xk_ref, xv_ref, pek_ref, pev_ref, wk1_ref, wk2_ref, wv1_ref, wv2_ref, ok_ref, ov_ref):
    half_w = CMP_STRIDE * NSA_DH

    def one(x_ref, pe_ref, w1_ref, w2_ref):
        x = x_ref[0, 0]
        n_slab = x.shape[0]
        p0 = _dot(x, w1_ref[0:half_w, :])
        p1 = _dot(x, w1_ref[half_w:2 * half_w, :])
        pe = jnp.broadcast_to(pe_ref[...], (8, 2 * half_w)).astype(BF16)
        pw = _dot(pe, w1_ref[...])[0:1, :]
        h = p0 + pltpu.roll(p1, n_slab - 1, 0) + pw
        return _dot(_gelu_tanh(h).astype(BF16), w2_ref[...])

    ok_ref[0, 0] = one(xk_ref, pek_ref, wk1_ref, wk2_ref).astype(ok_ref.dtype)
    ov_ref[0, 0] = one(xv_ref, pev_ref, wv1_ref, wv2_ref).astype(ov_ref.dtype)


def _compress(xk, xv, pek, pev, wk1, wk2, wv1, wv2):
    bsz, grp, n_slab, wdt = xk.shape
    xs = pl.BlockSpec((1, 1, n_slab, wdt), lambda b, g: (b, g, 0, 0))
    full = lambda a: pl.BlockSpec(a.shape, lambda b, g: (0,) * a.ndim)
    osd = jax.ShapeDtypeStruct((bsz, grp, n_slab, NSA_DH), BF16)
    os_ = pl.BlockSpec((1, 1, n_slab, NSA_DH), lambda b, g: (b, g, 0, 0))
    return pl.pallas_call(
        _compress_kernel,
        grid=(bsz, grp),
        in_specs=[xs, xs, full(pek), full(pev), full(wk1), full(wk2), full(wv1), full(wv2)],
        out_specs=[os_, os_],
        out_shape=[osd, osd],
        compiler_params=_cparams(("parallel", "parallel")),
    )(xk, xv, pek, pev, wk1, wk2, wv1, wv2)


def _attn_kernel(nq_ref, qr_ref, kcmp_ref, vcmpt_ref, ks_ref, vst_ref, kw_ref, vwt_ref, ng_ref, ovt_ref,
                 o_ref, gt_scr, qaug_scr, *, seq):
    qn = ATT_Q
    grp = NSA_GROUP
    ln = grp * qn
    n_sel = seq // SEL_BLOCK
    top = min(SEL_TOPK, n_sel)
    g_id = pl.program_id(1)
    q0 = pl.multiple_of(pl.program_id(2) * qn, qn)

    def stack_heads(x):
        return jnp.concatenate([x[:, r * NSA_DH:(r + 1) * NSA_DH] for r in range(grp)], axis=0)

    def add_per_head(s, bias):
        return jnp.concatenate([s[:, r * qn:(r + 1) * qn] + bias for r in range(grp)], axis=1)

    qs = stack_heads(nq_ref[0])
    qrs = stack_heads(qr_ref[0])
    t_lane = q0 + (lax.broadcasted_iota(jnp.int32, (1, ln), 1) & (qn - 1))
    t_q = q0 + lax.broadcasted_iota(jnp.int32, (1, qn), 1)

    n_slab = kcmp_ref.shape[2]
    wlen = WINDOW + qn
    w0 = pl.multiple_of(jnp.maximum(q0 - WINDOW, 0), qn)
    sc = _dot_nt(kcmp_ref[0, 0], qs) * (ATT_SCALE * LOG2E)
    sw = _dot_nt(kw_ref[0, 0, pl.ds(w0, wlen), :], qrs)
    sd = _dot_nt(ks_ref[0, 0, pl.ds(q0, qn), :][:, 0:NSA_DH], qrs)

    cend = lax.broadcasted_iota(jnp.int32, (n_slab, 1), 0) * CMP_STRIDE + (CMP_LEN - 1)
    sm = add_per_head(sc, jnp.where(cend <= t_q, 0.0, NEG))
    e = jnp.exp2(sm - _colmax(sm))
    den = jnp.sum(e, axis=0, keepdims=True)
    p = e * jnp.where(t_lane >= CMP_LEN - 1, 1.0 / den, 0.0)
    oc = _dot(vcmpt_ref[0, 0], p.astype(BF16))

    psum = p[:, 0:qn]
    for r in range(1, grp):
        psum = psum + p[:, r * qn:(r + 1) * qn]
    hi, mid, lo = _split3(psum)
    ovt = ovt_ref[...]
    imp = _dot(ovt, hi) + _dot(ovt, mid) + _dot(ovt, lo)

    diff = t_q - (w0 + lax.broadcasted_iota(jnp.int32, (wlen, 1), 0))
    sw = add_per_head(sw, jnp.where((diff >= 0) & (diff < WINDOW), 0.0, NEG))
    pw = jnp.exp2(sw - _colmax(sw))
    rw = _dot(vwt_ref[0, 0, :, pl.ds(w0, wlen)], pw.astype(BF16))
    ow = rw[0:NSA_DH] / rw[NSA_DH:NSA_DH + 1]

    kpos = q0 + lax.broadcasted_iota(jnp.int32, (qn, 1), 0)
    sd = add_per_head(sd, jnp.where(kpos <= t_q, 0.0, NEG))
    m0 = _colmax(sd)
    acc0 = _dot(vst_ref[0, 0, :, pl.ds(q0, qn)], jnp.exp2(sd - m0).astype(BF16))

    gt_scr[...] = jax.nn.sigmoid(ng_ref[0].astype(F32)).T

    def gate_row(branch):
        return jnp.concatenate(
            [gt_scr[pl.ds(branch * NSA_HEADS + g_id * grp + r, 1), :] for r in range(grp)], axis=1)

    out_cw = gate_row(0) * oc + gate_row(2) * ow
    gate_sel = gate_row(1)

    jidx = lax.broadcasted_iota(jnp.int32, (n_sel, qn), 0)
    tq = q0 + lax.broadcasted_iota(jnp.int32, (n_sel, qn), 1)
    cur = tq // SEL_BLOCK
    forced = (jidx == 0) | (jidx == cur) | (jidx == cur - 1)
    causal_blk = jidx * SEL_BLOCK <= tq
    val = jnp.where(forced, -2.0, imp)
    val = jnp.where(causal_blk, val, -1.0)
    rows8 = 8
    groups = [val[v * rows8:(v + 1) * rows8] for v in range(n_sel // rows8)]
    counts = [jnp.zeros((rows8, qn), F32) for _ in groups]
    jrow = lax.broadcasted_iota(jnp.int32, (rows8, qn), 0)
    for jp in range(n_sel):
        other = jnp.broadcast_to(val[jp:jp + 1, :], (rows8, qn))
        for v, grp_val in enumerate(groups):
            ge = jnp.where(other >= grp_val, 1.0, 0.0)
            gt = jnp.where(other > grp_val, 1.0, 0.0)
            if v * rows8 > jp:
                before = ge
            elif (v + 1) * rows8 <= jp:
                before = gt
            else:
                before = jnp.where(jrow > jp - v * rows8, ge, gt)
            counts[v] = counts[v] + before
    rank = jnp.concatenate(counts, axis=0)
    sel = jnp.where(forced | (rank < float(max(top - 3, 0))), 1.0, 0.0)
    diag_blk = q0 // SEL_BLOCK
    in_diag = (jidx >= diag_blk) & (jidx < diag_blk + qn // SEL_BLOCK)
    selb = jnp.where((sel > 0.0) & causal_blk & jnp.logical_not(in_diag), 0.0, NEG)
    selb_t = jnp.concatenate([selb, jnp.zeros((LANES - n_sel, qn), F32)], axis=0).T
    selb_t = selb_t[:, 0:NSA_DH].astype(BF16)
    qaug_scr[...] = jnp.concatenate([qrs, jnp.concatenate([selb_t] * grp, axis=0)], axis=1)

    kt_len = ATT_KT
    sub = kt_len // ATT_SUBTILES

    def sweep_step(kt, carry):
        m_run, acc = carry
        k0s = [pl.multiple_of(kt * kt_len + j * sub, sub) for j in range(ATT_SUBTILES)]
        score = lambda j: _dot_nt(ks_ref[0, 0, pl.ds(k0s[j], sub), :], qaug_scr[...])
        scores = [score(j) for j in range(ATT_LOOKAHEAD)]
        for j in range(ATT_SUBTILES):
            s = scores[j]
            m_new = jnp.maximum(m_run, _colmax(s))
            alpha = jnp.exp2(m_run - m_new)
            pt = jnp.exp2(s - m_new).astype(BF16)
            acc = alpha * acc + _dot(vst_ref[0, 0, :, pl.ds(k0s[j], sub)], pt)
            m_run = m_new
            if j + ATT_LOOKAHEAD < ATT_SUBTILES:
                scores.append(score(j + ATT_LOOKAHEAD))
        return m_run, acc

    n_tiles = (q0 + kt_len - 1) // kt_len
    _, acc = lax.fori_loop(0, n_tiles, sweep_step, (m0, acc0))
    osel = acc[0:NSA_DH] / acc[NSA_DH:NSA_DH + 1]

    out_t = out_cw + gate_sel * osel
    pieces = []
    for r2 in range(grp // 2):
        pair = jnp.concatenate([out_t[:, (2 * r2) * qn:(2 * r2 + 1) * qn],
                                out_t[:, (2 * r2 + 1) * qn:(2 * r2 + 2) * qn]], axis=0)
        pieces.append(pair.T)
    o_ref[0] = jnp.concatenate(pieces, axis=1).astype(o_ref.dtype)


def _attention(proj, q_rot, k_cmp, v_cmp_t, ks, vs_t, kw, vw_t, ov_t, bsz, seq):
    qn = ATT_Q
    gw = NSA_GROUP * NSA_DH
    n_slab = k_cmp.shape[2]
    n_sel = seq // SEL_BLOCK
    qspec = pl.BlockSpec((1, qn, gw), lambda b, g, i: (b, i, g))
    per_group = lambda a: pl.BlockSpec((1, 1) + a.shape[2:], lambda b, g, i: (b, g, 0, 0))
    return pl.pallas_call(
        functools.partial(_attn_kernel, seq=seq),
        grid=(bsz, NSA_KV, seq // qn),
        in_specs=[pl.BlockSpec((1, qn, gw), lambda b, g, i: (b, i, P_NQ // gw + g)), qspec,
                  per_group(k_cmp), per_group(v_cmp_t), per_group(ks), per_group(vs_t),
                  per_group(kw), per_group(vw_t),
                  pl.BlockSpec((1, qn, LANES), lambda b, g, i: (b, i, P_NG // LANES)),
                  pl.BlockSpec((n_sel, n_slab), lambda b, g, i: (0, 0))],
        out_specs=qspec,
        out_shape=jax.ShapeDtypeStruct((bsz, seq, NSA_WIDTH), BF16),
        scratch_shapes=[pltpu.VMEM((LANES, qn), F32), pltpu.VMEM((NSA_GROUP * qn, 2 * NSA_DH), BF16)],
        compiler_params=_cparams(("parallel", "parallel", "arbitrary")),
    )(proj, q_rot, k_cmp, v_cmp_t, ks, vs_t, kw, vw_t, proj, ov_t)


def _merge_kernel(x_ref, ya_ref, yb_ref, ga_ref, gb_ref, wa_ref, wb_ref, wo_ref, gn_ref, o_ref):
    ga = jax.nn.sigmoid(ga_ref[...].astype(F32))
    gb = jax.nn.sigmoid(gb_ref[...].astype(F32))
    merged = ga * _dot(ya_ref[...], wa_ref[...]) + gb * _dot(yb_ref[...], wb_ref[...])
    mo = _dot(merged.astype(BF16), wo_ref[...])
    o_ref[...] = x_ref[...] + _rms(mo, gn_ref[...])


def _merge(x, ya, yb, gates, wa, wb, wo, gn):
    n, d = x.shape
    tm = min(TOK_TILE, n)
    row = lambda c: pl.BlockSpec((tm, d), lambda i: (i, c))
    wsp = pl.BlockSpec((d, d), lambda i: (0, 0))
    return pl.pallas_call(
        _merge_kernel,
        grid=(n // tm,),
        in_specs=[row(0), row(0), row(0), row(P_GA // d), row(P_GB // d), wsp, wsp, wsp,
                  pl.BlockSpec((1, d), lambda i: (0, 0))],
        out_specs=row(0),
        out_shape=jax.ShapeDtypeStruct((n, d), F32),
        compiler_params=_cparams(("parallel",)),
    )(x, ya, yb, gates, gates, wa, wb, wo, gn)


def _mlp_kernel(h_ref, gpre_ref, wu_ref, wd_ref, gpost_ref, o_ref, v_scr, acc_scr):
    j = pl.program_id(1)

    @pl.when(j == 0)
    def _():
        v_scr[...] = _rms(h_ref[...], gpre_ref[...]).astype(BF16)
        acc_scr[...] = jnp.zeros_like(acc_scr)

    up = _dot(v_scr[...], wu_ref[...])
    act = jnp.square(jnp.maximum(up, 0.0))
    acc_scr[...] += _dot(act.astype(BF16), wd_ref[...])

    @pl.when(j == pl.num_programs(1) - 1)
    def _():
        o_ref[...] = h_ref[...] + _rms(acc_scr[...], gpost_ref[...])


def _mlp(h, gpre, wu, wd, gpost):
    n, d = h.shape
    dff = wu.shape[1]
    tm = min(2 * TOK_TILE, n)
    tf = 1024
    vec = pl.BlockSpec((1, d), lambda i, j: (0, 0))
    return pl.pallas_call(
        _mlp_kernel,
        grid=(n // tm, dff // tf),
        in_specs=[pl.BlockSpec((tm, d), lambda i, j: (i, 0)), vec,
                  pl.BlockSpec((d, tf), lambda i, j: (0, j)),
                  pl.BlockSpec((tf, d), lambda i, j: (j, 0)), vec],
        out_specs=pl.BlockSpec((tm, d), lambda i, j: (i, 0)),
        out_shape=jax.ShapeDtypeStruct((n, d), F32),
        scratch_shapes=[pltpu.VMEM((tm, d), BF16), pltpu.VMEM((tm, d), F32)],
        compiler_params=_cparams(("parallel", "arbitrary")),
    )(h, gpre, wu, wd, gpost)


def _ple_kernel(h_ref, p_ref, wp_ref, wg_ref, gn_ref, o_ref):
    h = h_ref[...]
    e = _dot(p_ref[...].astype(BF16), wp_ref[...]) * jax.nn.sigmoid(_dot(h.astype(BF16), wg_ref[...]))
    o_ref[...] = h + _rms(e, gn_ref[...])


def _ple(h, p, wp, wg, gn):
    n, d = h.shape
    pd = p.shape[1]
    tm = min(TOK_TILE, n)
    return pl.pallas_call(
        _ple_kernel,
        grid=(n // tm,),
        in_specs=[pl.BlockSpec((tm, d), lambda i: (i, 0)),
                  pl.BlockSpec((tm, pd), lambda i: (i, 0)),
                  pl.BlockSpec((pd, d), lambda i: (0, 0)),
                  pl.BlockSpec((d, d), lambda i: (0, 0)),
                  pl.BlockSpec((1, d), lambda i: (0, 0))],
        out_specs=pl.BlockSpec((tm, d), lambda i: (i, 0)),
        out_shape=jax.ShapeDtypeStruct((n, d), F32),
        compiler_params=_cparams(("parallel",)),
    )(h, p, wp, wg, gn)


def _overlap_t(seq):
    n_slab = seq // CMP_STRIDE
    n_cmp = (seq - CMP_LEN) // CMP_STRIDE + 1
    n_sel = seq // SEL_BLOCK
    start = np.arange(n_slab) * CMP_STRIDE
    end = start + CMP_LEN - 1
    sel_start = np.arange(n_sel) * SEL_BLOCK
    ov = (start[None, :] < sel_start[:, None] + SEL_BLOCK) & (end[None, :] >= sel_start[:, None])
    ov = ov & (np.arange(n_slab)[None, :] < n_cmp)
    return jnp.asarray(ov.astype(np.float32), BF16)


def kernel(x, p, w_in, w_branch_a, w_branch_b, w_out, norm_pre_mix, norm_post_mix, norm_pre_mlp,
           norm_post_mlp, hg_lb_logits, hg_gnorm, cmp_pe_k, cmp_pe_v, cmp_wk1, cmp_wk2, cmp_wv1, cmp_wv2,
           w_up, w_down, w_ple, w_ple_gate, norm_ple):
    bsz, seq, d = x.shape
    depth = w_in.shape[0]
    assert depth == 1 and hg_lb_logits.shape[0] == 2 and d == D_MODEL
    assert seq % ATT_KT == 0 and seq >= WINDOW + ATT_Q and seq // SEL_BLOCK <= NSA_DH
    assert ATT_Q in (SEL_BLOCK, 2 * SEL_BLOCK)
    n = bsz * seq
    x2 = x.reshape(n, d)
    row = lambda a: a.reshape(1, -1)

    w = w_in[0]
    s_hq, s_hf, s_hi, s_hg = 0, HG_WIDTH, 2 * HG_WIDTH, 3 * HG_WIDTH
    s_nq = 4 * HG_WIDTH
    s_ng = s_nq + NSA_WIDTH + 6 * NSA_KVW
    s_ga = s_ng + 3 * NSA_HEADS
    col = lambda a, b: w[:, a:b]
    w_all = jnp.concatenate(
        [col(s_hq, s_hf), col(s_hi, s_hg), col(s_hg, s_nq), col(s_hf, s_hi), col(s_nq, s_ga),
         jnp.zeros((d, P_GA - P_NG - 3 * NSA_HEADS), F32), col(s_ga, s_ga + 2 * D_MODEL)], axis=1).astype(BF16)
    assert w_all.shape[1] == P_WIDTH
    proj2 = _rms_matmul(x2, row(norm_pre_mix[0]), w_all, BF16, 1024)
    proj = proj2.reshape(bsz, seq, P_WIDTH)

    y_a = _hgrn(proj, hg_lb_logits, row(hg_gnorm[0]), bsz, seq)

    tables = _rope_tables(seq)
    kvb = P_KV // NSA_KVW
    q_rot = _rope(proj, P_NQ // NSA_WIDTH, NSA_WIDTH, tables, ATT_SCALE * LOG2E, bsz, seq)
    ks_rot = _rope(proj, kvb + 2, NSA_KVW, tables, 1.0, bsz, seq)
    kw_rot = _rope(proj, kvb + 4, NSA_KVW, tables, 1.0, bsz, seq)
    kv_slice = lambda i: proj[:, :, P_KV + i * NSA_KVW: P_KV + (i + 1) * NSA_KVW]
    heads = lambda t: t.reshape(bsz, seq, NSA_KV, NSA_DH).transpose(0, 2, 1, 3)
    heads_t = lambda t: t.reshape(bsz, seq, NSA_KV, NSA_DH).transpose(0, 2, 3, 1)
    slabs = lambda t: heads(t).reshape(bsz, NSA_KV, seq // CMP_STRIDE, CMP_STRIDE * NSA_DH)
    k_cmp, v_cmp = _compress(
        slabs(kv_slice(0)), slabs(kv_slice(1)), cmp_pe_k[0].reshape(1, -1), cmp_pe_v[0].reshape(1, -1),
        cmp_wk1[0].astype(BF16), cmp_wk2[0].astype(BF16), cmp_wv1[0].astype(BF16), cmp_wv2[0].astype(BF16))
    onehot = jnp.asarray(np.arange(seq)[:, None] // SEL_BLOCK == np.arange(NSA_DH)[None, :], BF16)
    ks_aug = jnp.concatenate([heads(ks_rot), jnp.broadcast_to(onehot, (bsz, NSA_KV, seq, NSA_DH))], axis=3)
    ones_rows = jnp.ones((bsz, NSA_KV, SUBLANES_BF16, seq), BF16)
    with_ones = lambda t: jnp.concatenate([heads_t(t), ones_rows], axis=2)
    y_b = _attention(proj, q_rot, k_cmp, v_cmp.transpose(0, 1, 3, 2), ks_aug, with_ones(kv_slice(3)),
                     heads(kw_rot), with_ones(kv_slice(5)), _overlap_t(seq), bsz, seq)

    h1 = _merge(x2, y_a.reshape(n, -1), y_b.reshape(n, -1), proj2, w_branch_a[0].astype(BF16),
                w_branch_b[0].astype(BF16), w_out[0].astype(BF16), row(norm_post_mix[0]))
    h2 = _mlp(h1, row(norm_pre_mlp[0]), w_up[0].astype(BF16), w_down[0].astype(BF16), row(norm_post_mlp[0]))
    h3o = _ple(h2, p[0].reshape(n, -1), w_ple[0].astype(BF16), w_ple_gate[0].astype(BF16), row(norm_ple[0]))
    return h3o.reshape(bsz, seq, d)
```

```python
import functools

import numpy as np
import jax
import jax.numpy as jnp
from jax import lax
from jax.experimental import pallas as pl
from jax.experimental.pallas import tpu as pltpu

F32 = jnp.float32
BF16 = jnp.bfloat16

D_MODEL = 1024
HG_HEADS = 8
HG_DK = 128
HG_DV = 128
HG_WIDTH = HG_HEADS * HG_DK
HG_CHUNK = 64
NSA_HEADS = 16
NSA_KV = 4
NSA_GROUP = NSA_HEADS // NSA_KV
NSA_DH = 64
NSA_WIDTH = NSA_HEADS * NSA_DH
NSA_KVW = NSA_KV * NSA_DH
CMP_LEN = 32
CMP_STRIDE = 16
CMP_HIDDEN = 256
SEL_BLOCK = 64
SEL_TOPK = 16
WINDOW = 512
ROPE_THETA = 500000.0
ROPE_DIM = NSA_DH // 4
D_FF = 4 * D_MODEL
EPS = 1e-6
NEG = -1e30
ATT_SCALE = NSA_DH ** -0.5
LOG2E = float(np.log2(np.e))

LANES = 128
SUBLANES_BF16 = 16
VMEM_LIMIT = 56 * 1024 * 1024

TOK_TILE = 512
ATT_Q = 128
ATT_KT = 1024
ATT_SUBTILES = 8
ATT_LOOKAHEAD = 7
HG_TOK = 512

P_HQ, P_HI, P_HG, P_HF = 0, HG_WIDTH, 2 * HG_WIDTH, 3 * HG_WIDTH
P_NQ = 4 * HG_WIDTH
P_KV = P_NQ + NSA_WIDTH
P_NG = P_KV + 6 * NSA_KVW
P_GA = 7 * D_MODEL
P_GB = 8 * D_MODEL
P_WIDTH = 9 * D_MODEL
HG_EXP_CLAMP = 80.0


def _cparams(sem):
    return pltpu.CompilerParams(dimension_semantics=sem, vmem_limit_bytes=VMEM_LIMIT)


def _rms(x, gain):
    ms = jnp.mean(x * x, axis=-1, keepdims=True)
    return (x * lax.rsqrt(ms + EPS)) * gain


def _dot(a, b):
    return jnp.dot(a, b, preferred_element_type=F32)


def _dot_nt(a, b):
    return lax.dot_general(a, b, (((1,), (1,)), ((), ())), preferred_element_type=F32)


def _dot_tn(a, b):
    return lax.dot_general(a, b, (((0,), (0,)), ((), ())), preferred_element_type=F32)


def _colmax(s):
    parts = [s[i:i + 64] for i in range(0, s.shape[0], 64)] if s.shape[0] % 64 == 0 else [s]
    while len(parts) > 1:
        parts = [jnp.maximum(parts[i], parts[i + 1]) if i + 1 < len(parts) else parts[i]
                 for i in range(0, len(parts), 2)]
    return jnp.max(parts[0], axis=0, keepdims=True)


def _split3(x):
    hi = x.astype(BF16)
    r1 = x - hi.astype(F32)
    mid = r1.astype(BF16)
    lo = (r1 - mid.astype(F32)).astype(BF16)
    return hi, mid, lo


def _rms_matmul_kernel(x_ref, g_ref, w_ref, o_ref, u_scr):
    @pl.when(pl.program_id(1) == 0)
    def _():
        u_scr[...] = _rms(x_ref[...], g_ref[...]).astype(BF16)

    o_ref[...] = _dot(u_scr[...], w_ref[...]).astype(o_ref.dtype)


def _rms_matmul(x, gain, w, out_dtype, tn):
    n, d = x.shape
    wd = w.shape[1]
    tm = min(2 * TOK_TILE, n)
    return pl.pallas_call(
        _rms_matmul_kernel,
        grid=(n // tm, wd // tn),
        in_specs=[pl.BlockSpec((tm, d), lambda i, j: (i, 0)),
                  pl.BlockSpec((1, d), lambda i, j: (0, 0)),
                  pl.BlockSpec((d, tn), lambda i, j: (0, j))],
        out_specs=pl.BlockSpec((tm, tn), lambda i, j: (i, j)),
        out_shape=jax.ShapeDtypeStruct((n, wd), out_dtype),
        scratch_shapes=[pltpu.VMEM((tm, d), BF16)],
        compiler_params=_cparams(("parallel", "arbitrary")),
    )(x, gain, w)


def _hgrn_levels():
    return [HG_CHUNK >> (i + 1) for i in range(HG_CHUNK.bit_length() - 1)]


def _hgrn_intra_exact(qf, kk, b, gsel_ref):
    c_len = HG_CHUNK
    row = lax.broadcasted_iota(jnp.int32, (c_len, c_len), 0)
    col = lax.broadcasted_iota(jnp.int32, (c_len, c_len), 1)
    trow = lax.broadcasted_iota(jnp.int32, (c_len, HG_DK), 0)
    a = jnp.where(row == col, _dot_nt(qf.astype(BF16), kk.astype(BF16)), 0.0)
    b3 = jnp.concatenate(_split3(b), axis=0)
    for lvl, h in enumerate(_hgrn_levels()):
        b_at_ref = _dot(gsel_ref[lvl], b3)
        upper = (trow & (2 * h - 1)) >= h
        qe = jnp.where(upper, qf * jnp.exp(jnp.minimum(b - b_at_ref, 0.0)), 0.0)
        ke = jnp.where(upper, 0.0, kk * jnp.exp(jnp.minimum(b_at_ref - b, 0.0)))
        same_block = (row & -(2 * h)) == (col & -(2 * h))
        block = same_block & ((row & (2 * h - 1)) >= h) & ((col & (2 * h - 1)) < h)
        a = a + jnp.where(block, _dot_nt(qe.astype(BF16), ke.astype(BF16)), 0.0)
    return a


def _hgrn_kernel(q_ref, i_ref, g_ref, f_ref, lbl_ref, gn_ref, tri_ref, gsel_ref, o_ref, st_ref, a_scr, *,
                 n_chunks):
    c_len = HG_CHUNK
    half = c_len // 2

    @pl.when(pl.program_id(2) == 0)
    def _():
        st_ref[...] = jnp.zeros_like(st_ref)

    lg = lbl_ref[...]
    mx = jnp.max(lg, axis=0, keepdims=True)
    ex = jnp.exp(lg - mx)
    lb = ex[0:1, :] / jnp.sum(ex, axis=0, keepdims=True)

    row = lax.broadcasted_iota(jnp.int32, (c_len, c_len), 0)
    col = lax.broadcasted_iota(jnp.int32, (c_len, c_len), 1)
    causal = col <= row
    tri3 = tri_ref[...]

    chunks = [slice(c * c_len, (c + 1) * c_len) for c in range(n_chunks)]
    qf, kk, lf = [], [], []
    for rows in chunks:
        qx = q_ref[0, rows, :].astype(F32)
        qf.append(qx * jax.nn.sigmoid(qx) * (HG_DK ** -0.5))
        forget = lb + (1.0 - lb) * jax.nn.sigmoid(f_ref[0, rows, :].astype(F32))
        kk.append(1.0 - forget)
        lf.append(jnp.log(forget))
    bb = [_dot(tri3, jnp.concatenate(_split3(x), axis=0)) for x in lf]
    qe, ke, qb, kd, dlast = [], [], [], [], []
    for c in range(n_chunks):
        b = bb[c]
        b_mid = b[half - 1:half, :]
        b_last = b[c_len - 1:c_len, :]
        qe.append((qf[c] * jnp.exp(jnp.minimum(b - b_mid, HG_EXP_CLAMP))).astype(BF16))
        ke.append((kk[c] * jnp.exp(jnp.minimum(b_mid - b, HG_EXP_CLAMP))).astype(BF16))
        qb.append((qf[c] * jnp.exp(b)).astype(BF16))
        kd.append((kk[c] * jnp.exp(b_last - b)).astype(BF16))
        dlast.append(jnp.exp(b_last))
    vv = [i_ref[0, rows, :] for rows in chunks]
    for c in range(n_chunks):
        a_scr[c] = _dot_nt(qe[c], ke[c])
    kv = [_dot_tn(vv[c], kd[c]) for c in range(n_chunks)]
    states = [st_ref[...]]
    for c in range(n_chunks):
        states.append(states[c] * dlast[c] + kv[c])
    o_inter = [_dot_nt(qb[c], states[c].astype(BF16)) for c in range(n_chunks)]
    st_ref[...] = states[n_chunks]

    need = jnp.zeros((1, HG_DK), F32)
    for b in bb:
        b_mid = b[half - 1:half, :]
        need = jnp.maximum(need, jnp.maximum(b[0:1, :] - b_mid, b_mid - b[c_len - 1:c_len, :]))

    @pl.when(jnp.max(need) > HG_EXP_CLAMP)
    def _():
        for c in range(n_chunks):
            a_scr[c] = _hgrn_intra_exact(qf[c], kk[c], bb[c], gsel_ref)

    for c, rows in enumerate(chunks):
        o_intra = _dot(jnp.where(causal, a_scr[c], 0.0).astype(BF16), vv[c])
        gx = g_ref[0, rows, :].astype(F32)
        y = _rms(o_intra + o_inter[c], gn_ref[...]) * (gx * jax.nn.sigmoid(gx))
        o_ref[0, rows, :] = y.astype(o_ref.dtype)


def _hgrn(proj, lb_logits, gnorm, bsz, seq):
    n_chunks = HG_TOK // HG_CHUNK
    tri = np.tril(np.ones((HG_CHUNK, HG_CHUNK), np.float32))
    tri3 = jnp.asarray(np.concatenate([tri, tri, tri], axis=1), BF16)
    t_idx = np.arange(HG_CHUNK)
    gsel = np.zeros((len(_hgrn_levels()), HG_CHUNK, HG_CHUNK), np.float32)
    for lvl, h in enumerate(_hgrn_levels()):
        gsel[lvl, t_idx, (t_idx // (2 * h)) * (2 * h) + h - 1] = 1.0
    gsel3 = jnp.asarray(np.concatenate([gsel, gsel, gsel], axis=2), BF16)
    hh = HG_HEADS
    blk = (1, HG_TOK, HG_DK)
    head_cols = lambda off: pl.BlockSpec(blk, lambda b, h, t: (b, t, off // HG_DK + h))
    return pl.pallas_call(
        functools.partial(_hgrn_kernel, n_chunks=n_chunks),
        grid=(bsz, hh, seq // HG_TOK),
        in_specs=[head_cols(P_HQ), head_cols(P_HI), head_cols(P_HG), head_cols(P_HF),
                  pl.BlockSpec((2, HG_DK), lambda b, h, t: (0, h)),
                  pl.BlockSpec((1, HG_DV), lambda b, h, t: (0, 0)),
                  pl.BlockSpec((HG_CHUNK, 3 * HG_CHUNK), lambda b, h, t: (0, 0)),
                  pl.BlockSpec(gsel3.shape, lambda b, h, t: (0, 0, 0))],
        out_specs=pl.BlockSpec(blk, lambda b, h, t: (b, t, h)),
        out_shape=jax.ShapeDtypeStruct((bsz, seq, HG_WIDTH), BF16),
        scratch_shapes=[pltpu.VMEM((HG_DV, HG_DK), F32), pltpu.VMEM((n_chunks, HG_CHUNK, HG_CHUNK), F32)],
        compiler_params=_cparams(("parallel", "parallel", "arbitrary")),
    )(proj, proj, proj, proj, lb_logits, gnorm, tri3, gsel3)


def _rope_kernel(x_ref, c_ref, s1_ref, s2_ref, o_ref, *, scale):
    width = x_ref.shape[-1]
    c = c_ref[...]
    s1 = s1_ref[...]
    s2 = s2_ref[...]
    for j in range(width // LANES):
        cols = slice(j * LANES, (j + 1) * LANES)
        x = x_ref[0, :, cols].astype(F32)
        if scale != 1.0:
            x = x * scale
        o_ref[0, :, cols] = _rope_lanes(x, c, s1, s2).astype(o_ref.dtype)


def _rope(src, col_block, width, tables, scale, bsz, seq):
    tt = min(TOK_TILE, seq)
    tab = pl.BlockSpec((tt, LANES), lambda b, t: (t, 0))
    return pl.pallas_call(
        functools.partial(_rope_kernel, scale=scale),
        grid=(bsz, seq // tt),
        in_specs=[pl.BlockSpec((1, tt, width), lambda b, t: (b, t, col_block)), tab, tab, tab],
        out_specs=pl.BlockSpec((1, tt, width), lambda b, t: (b, t, 0)),
        out_shape=jax.ShapeDtypeStruct((bsz, seq, width), BF16),
        compiler_params=_cparams(("parallel", "parallel")),
    )(src, *tables)


def _rope_lanes(x, c, s1, s2):
    return x * c + pltpu.roll(x, LANES - ROPE_DIM // 2, 1) * s1 + pltpu.roll(x, ROPE_DIM // 2, 1) * s2


def _kv_prep_kernel(k_ref, v_ref, c_ref, s1_ref, s2_ref, ko_ref, vo_ref, *, pad_blocks, with_block_id):
    tt = k_ref.shape[1]
    step = pl.program_id(1)
    is_pad = step < pad_blocks
    pos = (step - pad_blocks) * tt + lax.broadcasted_iota(jnp.int32, (tt, NSA_DH), 0)
    lane = lax.broadcasted_iota(jnp.int32, (tt, NSA_DH), 1)
    if with_block_id:
        extra = jnp.where(pos // SEL_BLOCK == lane, 1.0, 0.0)
    else:
        extra = jnp.where((lane == 0) & is_pad, 1.0, 0.0)
    c, s1, s2 = c_ref[...], s1_ref[...], s2_ref[...]
    vt = v_ref[0].astype(F32).T
    for j in range(NSA_KVW // LANES):
        y = _rope_lanes(k_ref[0, :, j * LANES:(j + 1) * LANES].astype(F32), c, s1, s2)
        y = jnp.where(is_pad, 0.0, y)
        for g2 in range(LANES // NSA_DH):
            g = j * (LANES // NSA_DH) + g2
            ko_ref[0, g] = jnp.concatenate([y[:, g2 * NSA_DH:(g2 + 1) * NSA_DH], extra], axis=1).astype(BF16)
    for g in range(NSA_KV):
        vo_ref[0, g, 0:NSA_DH, :] = jnp.where(is_pad, 0.0, vt[g * NSA_DH:(g + 1) * NSA_DH]).astype(BF16)
        vo_ref[0, g, NSA_DH:NSA_DH + SUBLANES_BF16, :] = jnp.ones((SUBLANES_BF16, tt), BF16)


def _kv_prep(proj, k_block, v_block, tables, pad_rows, with_block_id, bsz, seq):
    tt = min(TOK_TILE, seq)
    pad_blocks = pad_rows // tt
    assert pad_blocks * tt == pad_rows
    src_t = lambda t: jnp.maximum(t - pad_blocks, 0)
    tab = pl.BlockSpec((tt, LANES), lambda b, t: (src_t(t), 0))
    rows = seq + pad_rows
    return pl.pallas_call(
        functools.partial(_kv_prep_kernel, pad_blocks=pad_blocks, with_block_id=with_block_id),
        grid=(bsz, rows // tt),
        in_specs=[pl.BlockSpec((1, tt, NSA_KVW), lambda b, t: (b, src_t(t), k_block)),
                  pl.BlockSpec((1, tt, NSA_KVW), lambda b, t: (b, src_t(t), v_block)), tab, tab, tab],
        out_specs=[pl.BlockSpec((1, NSA_KV, tt, 2 * NSA_DH), lambda b, t: (b, 0, t, 0)),
                   pl.BlockSpec((1, NSA_KV, NSA_DH + SUBLANES_BF16, tt), lambda b, t: (b, 0, 0, t))],
        out_shape=[jax.ShapeDtypeStruct((bsz, NSA_KV, rows, 2 * NSA_DH), BF16),
                   jax.ShapeDtypeStruct((bsz, NSA_KV, NSA_DH + SUBLANES_BF16, rows), BF16)],
        compiler_params=_cparams(("parallel", "parallel")),
    )(proj, proj, *tables)


def _rope_tables(seq):
    half = ROPE_DIM // 2
    inv = jnp.asarray(ROPE_THETA ** (-np.arange(half) * 2.0 / ROPE_DIM), F32)
    ang = jnp.arange(seq, dtype=F32)[:, None] * inv[None, :]
    cos, sin = jnp.cos(ang), jnp.sin(ang)
    lane = np.arange(LANES) % NSA_DH
    pick = lane % half
    cos_l = cos[:, pick]
    sin_l = sin[:, pick]
    first = jnp.asarray(lane < half)
    second = jnp.asarray((lane >= half) & (lane < ROPE_DIM))
    c = jnp.where(first | second, cos_l, 1.0)
    s1 = jnp.where(first, -sin_l, 0.0)
    s2 = jnp.where(second, sin_l, 0.0)
    return c, s1, s2


def _gelu_tanh(x):
    return 0.5 * x * (1.0 + jnp.tanh(np.sqrt(2.0 / np.pi).astype(np.float32) * (x + 0.044715 * (x * x * x))))


def _compress_kernel(xk_ref, xv_ref, pek_ref, pev_ref, wk1_ref, wk2_ref, wv1_ref, wv2_ref, ok_ref, ov_ref):
    half_w = CMP_STRIDE * NSA_DH

    def one(x_ref, pe_ref, w1_ref, w2_ref):
        x = x_ref[0, 0]
        n_slab = x.shape[0]
        p0 = _dot(x, w1_ref[0:half_w, :])
        p1 = _dot(x, w1_ref[half_w:2 * half_w, :])
        pe = jnp.broadcast_to(pe_ref[...], (8, 2 * half_w)).astype(BF16)
        pw = _dot(pe, w1_ref[...])[0:1, :]
        h = p0 + pltpu.roll(p1, n_slab - 1, 0) + pw
        return _dot(_gelu_tanh(h).astype(BF16), w2_ref[...])

    ok_ref[0, 0] = one(xk_ref, pek_ref, wk1_ref, wk2_ref).astype(ok_ref.dtype)
    ov_ref[0, 0] = one(xv_ref, pev_ref, wv1_ref, wv2_ref).astype(ov_ref.dtype)


def _compress(xk, xv, pek, pev, wk1, wk2, wv1, wv2):
    bsz, grp, n_slab, wdt = xk.shape
    xs = pl.BlockSpec((1, 1, n_slab, wdt), lambda b, g: (b, g, 0, 0))
    full = lambda a: pl.BlockSpec(a.shape, lambda b, g: (0,) * a.ndim)
    osd = jax.ShapeDtypeStruct((bsz, grp, n_slab, NSA_DH), BF16)
    os_ = pl.BlockSpec((1, 1, n_slab, NSA_DH), lambda b, g: (b, g, 0, 0))
    return pl.pallas_call(
        _compress_kernel,
        grid=(bsz, grp),
        in_specs=[xs, xs, full(pek), full(pev), full(wk1), full(wk2), full(wv1), full(wv2)],
        out_specs=[os_, os_],
        out_shape=[osd, osd],
        compiler_params=_cparams(("parallel", "parallel")),
    )(xk, xv, pek, pev, wk1, wk2, wv1, wv2)


def _attn_kernel(nq_ref, qr_ref, kcmp_ref, vcmpt_ref, ks_ref, vst_ref, kw_ref, vwt_ref, ng_ref, ovt_ref,
                 o_ref, gt_scr, qaug_scr, *, seq):
    qn = ATT_Q
    grp = NSA_GROUP
    ln = grp * qn
    n_sel = seq // SEL_BLOCK
    top = min(SEL_TOPK, n_sel)
    g_id = pl.program_id(1)
    q0 = pl.multiple_of(pl.program_id(2) * qn, qn)

    def stack_heads(x):
        return jnp.concatenate([x[:, r * NSA_DH:(r + 1) * NSA_DH] for r in range(grp)], axis=0)

    def add_per_head(s, bias):
        return jnp.concatenate([s[:, r * qn:(r + 1) * qn] + bias for r in range(grp)], axis=1)

    qs = stack_heads(nq_ref[0])
    qrs = stack_heads(qr_ref[0])
    t_lane = q0 + (lax.broadcasted_iota(jnp.int32, (1, ln), 1) & (qn - 1))
    t_q = q0 + lax.broadcasted_iota(jnp.int32, (1, qn), 1)

    n_slab = kcmp_ref.shape[2]
    wlen = WINDOW + qn
    pad_col = jnp.where(lax.broadcasted_iota(jnp.int32, (ln, NSA_DH), 1) == 0, NEG, 0.0).astype(BF16)
    sc = _dot_nt(kcmp_ref[0, 0], qs) * (ATT_SCALE * LOG2E)
    sw = _dot_nt(kw_ref[0, 0, pl.ds(q0, wlen), :], jnp.concatenate([qrs, pad_col], axis=1))
    sd = _dot_nt(ks_ref[0, 0, pl.ds(q0, qn), :][:, 0:NSA_DH], qrs)

    cend = lax.broadcasted_iota(jnp.int32, (n_slab, 1), 0) * CMP_STRIDE + (CMP_LEN - 1)
    sm = add_per_head(sc, jnp.where(cend <= t_q, 0.0, NEG))
    e = jnp.exp2(sm - _colmax(sm))
    den = jnp.sum(e, axis=0, keepdims=True)
    p = e * jnp.where(t_lane >= CMP_LEN - 1, 1.0 / den, 0.0)
    oc = _dot(vcmpt_ref[0, 0], p.astype(BF16))

    psum = p[:, 0:qn]
    for r in range(1, grp):
        psum = psum + p[:, r * qn:(r + 1) * qn]
    hi, mid, lo = _split3(psum)
    ovt = ovt_ref[...]
    imp = _dot(ovt, hi) + _dot(ovt, mid) + _dot(ovt, lo)

    i_loc = lax.broadcasted_iota(jnp.int32, (qn, 1), 0)
    j_loc = lax.broadcasted_iota(jnp.int32, (1, qn), 1)
    sw = jnp.concatenate([add_per_head(sw[0:qn], jnp.where(i_loc > j_loc, 0.0, NEG)),
                          sw[qn:WINDOW],
                          add_per_head(sw[WINDOW:wlen], jnp.where(i_loc <= j_loc, 0.0, NEG))], axis=0)
    pw = jnp.exp2(sw - _colmax(sw))
    rw = _dot(vwt_ref[0, 0, :, pl.ds(q0, wlen)], pw.astype(BF16))
    ow = rw[0:NSA_DH] / rw[NSA_DH:NSA_DH + 1]

    kpos = q0 + lax.broadcasted_iota(jnp.int32, (qn, 1), 0)
    sd = add_per_head(sd, jnp.where(kpos <= t_q, 0.0, NEG))
    m0 = _colmax(sd)
    acc0 = _dot(vst_ref[0, 0, :, pl.ds(q0, qn)], jnp.exp2(sd - m0).astype(BF16))

    gt_scr[...] = jax.nn.sigmoid(ng_ref[0].astype(F32)).T

    def gate_row(branch):
        return jnp.concatenate(
            [gt_scr[pl.ds(branch * NSA_HEADS + g_id * grp + r, 1), :] for r in range(grp)], axis=1)

    out_cw = gate_row(0) * oc + gate_row(2) * ow
    gate_sel = gate_row(1)

    jidx = lax.broadcasted_iota(jnp.int32, (n_sel, qn), 0)
    tq = q0 + lax.broadcasted_iota(jnp.int32, (n_sel, qn), 1)
    cur = tq // SEL_BLOCK
    forced = (jidx == 0) | (jidx == cur) | (jidx == cur - 1)
    causal_blk = jidx * SEL_BLOCK <= tq
    val = jnp.where(forced, -2.0, imp)
    val = jnp.where(causal_blk, val, -1.0)
    rows8 = 8
    groups = [val[v * rows8:(v + 1) * rows8] for v in range(n_sel // rows8)]
    counts = [jnp.zeros((rows8, qn), F32) for _ in groups]
    jrow = lax.broadcasted_iota(jnp.int32, (rows8, qn), 0)
    for jp in range(n_sel):
        other = jnp.broadcast_to(val[jp:jp + 1, :], (rows8, qn))
        for v, grp_val in enumerate(groups):
            ge = jnp.where(other >= grp_val, 1.0, 0.0)
            gt = jnp.where(other > grp_val, 1.0, 0.0)
            if v * rows8 > jp:
                before = ge
            elif (v + 1) * rows8 <= jp:
                before = gt
            else:
                before = jnp.where(jrow > jp - v * rows8, ge, gt)
            counts[v] = counts[v] + before
    rank = jnp.concatenate(counts, axis=0)
    sel = jnp.where(forced | (rank < float(max(top - 3, 0))), 1.0, 0.0)
    diag_blk = q0 // SEL_BLOCK
    in_diag = (jidx >= diag_blk) & (jidx < diag_blk + qn // SEL_BLOCK)
    selb = jnp.where((sel > 0.0) & causal_blk & jnp.logical_not(in_diag), 0.0, NEG)
    selb_t = jnp.concatenate([selb, jnp.zeros((LANES - n_sel, qn), F32)], axis=0).T
    selb_t = selb_t[:, 0:NSA_DH].astype(BF16)
    qaug_scr[...] = jnp.concatenate([qrs, jnp.concatenate([selb_t] * grp, axis=0)], axis=1)

    kt_len = ATT_KT
    sub = kt_len // ATT_SUBTILES

    def sweep_step(kt, carry):
        m_run, acc = carry
        k0s = [pl.multiple_of(kt * kt_len + j * sub, sub) for j in range(ATT_SUBTILES)]
        score = lambda j: _dot_nt(ks_ref[0, 0, pl.ds(k0s[j], sub), :], qaug_scr[...])
        scores = [score(j) for j in range(ATT_LOOKAHEAD)]
        for j in range(ATT_SUBTILES):
            s = scores[j]
            m_new = jnp.maximum(m_run, _colmax(s))
            alpha = jnp.exp2(m_run - m_new)
            pt = jnp.exp2(s - m_new).astype(BF16)
            acc = alpha * acc + _dot(vst_ref[0, 0, :, pl.ds(k0s[j], sub)], pt)
            m_run = m_new
            if j + ATT_LOOKAHEAD < ATT_SUBTILES:
                scores.append(score(j + ATT_LOOKAHEAD))
        return m_run, acc

    n_tiles = (q0 + kt_len - 1) // kt_len
    _, acc = lax.fori_loop(0, n_tiles, sweep_step, (m0, acc0))
    osel = acc[0:NSA_DH] / acc[NSA_DH:NSA_DH + 1]

    out_t = out_cw + gate_sel * osel
    pieces = []
    for r2 in range(grp // 2):
        pair = jnp.concatenate([out_t[:, (2 * r2) * qn:(2 * r2 + 1) * qn],
                                out_t[:, (2 * r2 + 1) * qn:(2 * r2 + 2) * qn]], axis=0)
        pieces.append(pair.T)
    o_ref[0] = jnp.concatenate(pieces, axis=1).astype(o_ref.dtype)


def _attention(proj, q_rot, k_cmp, v_cmp_t, ks, vs_t, kw, vw_t, ov_t, bsz, seq):
    qn = ATT_Q
    gw = NSA_GROUP * NSA_DH
    n_slab = k_cmp.shape[2]
    n_sel = seq // SEL_BLOCK
    qspec = pl.BlockSpec((1, qn, gw), lambda b, g, i: (b, i, g))
    per_group = lambda a: pl.BlockSpec((1, 1) + a.shape[2:], lambda b, g, i: (b, g, 0, 0))
    return pl.pallas_call(
        functools.partial(_attn_kernel, seq=seq),
        grid=(bsz, NSA_KV, seq // qn),
        in_specs=[pl.BlockSpec((1, qn, gw), lambda b, g, i: (b, i, P_NQ // gw + g)), qspec,
                  per_group(k_cmp), per_group(v_cmp_t), per_group(ks), per_group(vs_t),
                  per_group(kw), per_group(vw_t),
                  pl.BlockSpec((1, qn, LANES), lambda b, g, i: (b, i, P_NG // LANES)),
                  pl.BlockSpec((n_sel, n_slab), lambda b, g, i: (0, 0))],
        out_specs=qspec,
        out_shape=jax.ShapeDtypeStruct((bsz, seq, NSA_WIDTH), BF16),
        scratch_shapes=[pltpu.VMEM((LANES, qn), F32), pltpu.VMEM((NSA_GROUP * qn, 2 * NSA_DH), BF16)],
        compiler_params=_cparams(("parallel", "parallel", "arbitrary")),
    )(proj, q_rot, k_cmp, v_cmp_t, ks, vs_t, kw, vw_t, proj, ov_t)


def _merge_kernel(x_ref, ya_ref, yb_ref, ga_ref, gb_ref, wa_ref, wb_ref, wo_ref, gn_ref, o_ref):
    ga = jax.nn.sigmoid(ga_ref[...].astype(F32))
    gb = jax.nn.sigmoid(gb_ref[...].astype(F32))
    merged = ga * _dot(ya_ref[...], wa_ref[...]) + gb * _dot(yb_ref[...], wb_ref[...])
    mo = _dot(merged.astype(BF16), wo_ref[...])
    o_ref[...] = x_ref[...] + _rms(mo, gn_ref[...])


def _merge(x, ya, yb, gates, wa, wb, wo, gn):
    n, d = x.shape
    tm = min(TOK_TILE, n)
    row = lambda c: pl.BlockSpec((tm, d), lambda i: (i, c))
    wsp = pl.BlockSpec((d, d), lambda i: (0, 0))
    return pl.pallas_call(
        _merge_kernel,
        grid=(n // tm,),
        in_specs=[row(0), row(0), row(0), row(P_GA // d), row(P_GB // d), wsp, wsp, wsp,
                  pl.BlockSpec((1, d), lambda i: (0, 0))],
        out_specs=row(0),
        out_shape=jax.ShapeDtypeStruct((n, d), F32),
        compiler_params=_cparams(("parallel",)),
    )(x, ya, yb, gates, gates, wa, wb, wo, gn)


def _mlp_kernel(h_ref, gpre_ref, wu_ref, wd_ref, gpost_ref, o_ref, v_scr, acc_scr):
    j = pl.program_id(1)

    @pl.when(j == 0)
    def _():
        v_scr[...] = _rms(h_ref[...], gpre_ref[...]).astype(BF16)
        acc_scr[...] = jnp.zeros_like(acc_scr)

    up = _dot(v_scr[...], wu_ref[...])
    act = jnp.square(jnp.maximum(up, 0.0))
    acc_scr[...] += _dot(act.astype(BF16), wd_ref[...])

    @pl.when(j == pl.num_programs(1) - 1)
    def _():
        o_ref[...] = h_ref[...] + _rms(acc_scr[...], gpost_ref[...])


def _mlp(h, gpre, wu, wd, gpost):
    n, d = h.shape
    dff = wu.shape[1]
    tm = min(2 * TOK_TILE, n)
    tf = 1024
    vec = pl.BlockSpec((1, d), lambda i, j: (0, 0))
    return pl.pallas_call(
        _mlp_kernel,
        grid=(n // tm, dff // tf),
        in_specs=[pl.BlockSpec((tm, d), lambda i, j: (i, 0)), vec,
                  pl.BlockSpec((d, tf), lambda i, j: (0, j)),
                  pl.BlockSpec((tf, d), lambda i, j: (j, 0)), vec],
        out_specs=pl.BlockSpec((tm, d), lambda i, j: (i, 0)),
        out_shape=jax.ShapeDtypeStruct((n, d), F32),
        scratch_shapes=[pltpu.VMEM((tm, d), BF16), pltpu.VMEM((tm, d), F32)],
        compiler_params=_cparams(("parallel", "arbitrary")),
    )(h, gpre, wu, wd, gpost)


def _ple_kernel(h_ref, p_ref, wp_ref, wg_ref, gn_ref, o_ref):
    h = h_ref[...]
    e = _dot(p_ref[...].astype(BF16), wp_ref[...]) * jax.nn.sigmoid(_dot(h.astype(BF16), wg_ref[...]))
    o_ref[...] = h + _rms(e, gn_ref[...])


def _ple(h, p, wp, wg, gn):
    n, d = h.shape
    pd = p.shape[1]
    tm = min(TOK_TILE, n)
    return pl.pallas_call(
        _ple_kernel,
        grid=(n // tm,),
        in_specs=[pl.BlockSpec((tm, d), lambda i: (i, 0)),
                  pl.BlockSpec((tm, pd), lambda i: (i, 0)),
                  pl.BlockSpec((pd, d), lambda i: (0, 0)),
                  pl.BlockSpec((d, d), lambda i: (0, 0)),
                  pl.BlockSpec((1, d), lambda i: (0, 0))],
        out_specs=pl.BlockSpec((tm, d), lambda i: (i, 0)),
        out_shape=jax.ShapeDtypeStruct((n, d), F32),
        compiler_params=_cparams(("parallel",)),
    )(h, p, wp, wg, gn)


def _overlap_t(seq):
    n_slab = seq // CMP_STRIDE
    n_cmp = (seq - CMP_LEN) // CMP_STRIDE + 1
    n_sel = seq // SEL_BLOCK
    start = np.arange(n_slab) * CMP_STRIDE
    end = start + CMP_LEN - 1
    sel_start = np.arange(n_sel) * SEL_BLOCK
    ov = (start[None, :] < sel_start[:, None] + SEL_BLOCK) & (end[None, :] >= sel_start[:, None])
    ov = ov & (np.arange(n_slab)[None, :] < n_cmp)
    return jnp.asarray(ov.astype(np.float32), BF16)


def kernel(x, p, w_in, w_branch_a, w_branch_b, w_out, norm_pre_mix, norm_post_mix, norm_pre_mlp,
           norm_post_mlp, hg_lb_logits, hg_gnorm, cmp_pe_k, cmp_pe_v, cmp_wk1, cmp_wk2, cmp_wv1, cmp_wv2,
           w_up, w_down, w_ple, w_ple_gate, norm_ple):
    bsz, seq, d = x.shape
    depth = w_in.shape[0]
    assert depth == 1 and hg_lb_logits.shape[0] == 2 and d == D_MODEL
    assert seq % ATT_KT == 0 and seq >= WINDOW + ATT_Q and seq // SEL_BLOCK <= NSA_DH
    assert ATT_Q in (SEL_BLOCK, 2 * SEL_BLOCK)
    n = bsz * seq
    x2 = x.reshape(n, d)
    row = lambda a: a.reshape(1, -1)

    w = w_in[0]
    s_hq, s_hf, s_hi, s_hg = 0, HG_WIDTH, 2 * HG_WIDTH, 3 * HG_WIDTH
    s_nq = 4 * HG_WIDTH
    s_ng = s_nq + NSA_WIDTH + 6 * NSA_KVW
    s_ga = s_ng + 3 * NSA_HEADS
    col = lambda a, b: w[:, a:b]
    w_all = jnp.concatenate(
        [col(s_hq, s_hf), col(s_hi, s_hg), col(s_hg, s_nq), col(s_hf, s_hi), col(s_nq, s_ga),
         jnp.zeros((d, P_GA - P_NG - 3 * NSA_HEADS), F32), col(s_ga, s_ga + 2 * D_MODEL)], axis=1).astype(BF16)
    assert w_all.shape[1] == P_WIDTH
    proj2 = _rms_matmul(x2, row(norm_pre_mix[0]), w_all, BF16, 1024)
    proj = proj2.reshape(bsz, seq, P_WIDTH)

    y_a = _hgrn(proj, hg_lb_logits, row(hg_gnorm[0]), bsz, seq)

    tables = _rope_tables(seq)
    kvb = P_KV // NSA_KVW
    q_rot = _rope(proj, P_NQ // NSA_WIDTH, NSA_WIDTH, tables, ATT_SCALE * LOG2E, bsz, seq)
    ks_aug, vs_t = _kv_prep(proj, kvb + 2, kvb + 3, tables, 0, True, bsz, seq)
    kw_aug, vw_t = _kv_prep(proj, kvb + 4, kvb + 5, tables, WINDOW, False, bsz, seq)
    kv_slice = lambda i: proj[:, :, P_KV + i * NSA_KVW: P_KV + (i + 1) * NSA_KVW]
    heads = lambda t: t.reshape(bsz, seq, NSA_KV, NSA_DH).transpose(0, 2, 1, 3)
    slabs = lambda t: heads(t).reshape(bsz, NSA_KV, seq // CMP_STRIDE, CMP_STRIDE * NSA_DH)
    k_cmp, v_cmp = _compress(
        slabs(kv_slice(0)), slabs(kv_slice(1)), cmp_pe_k[0].reshape(1, -1), cmp_pe_v[0].reshape(1, -1),
        cmp_wk1[0].astype(BF16), cmp_wk2[0].astype(BF16), cmp_wv1[0].astype(BF16), cmp_wv2[0].astype(BF16))
    y_b = _attention(proj, q_rot, k_cmp, v_cmp.transpose(0, 1, 3, 2), ks_aug, vs_t, kw_aug, vw_t,
                     _overlap_t(seq), bsz, seq)

    h1 = _merge(x2, y_a.reshape(n, -1), y_b.reshape(n, -1), proj2, w_branch_a[0].astype(BF16),
                w_branch_b[0].astype(BF16), w_out[0].astype(BF16), row(norm_post_mix[0]))
    h2 = _mlp(h1, row(norm_pre_mlp[0]), w_up[0].astype(BF16), w_down[0].astype(BF16), row(norm_post_mlp[0]))
    h3o = _ple(h2, p[0].reshape(n, -1), w_ple[0].astype(BF16), w_ple_gate[0].astype(BF16), row(norm_ple[0]))
    return h3o.reshape(bsz, seq, d)
```

```python
import functools

import numpy as np
import jax
import jax.numpy as jnp
from jax import lax
from jax.experimental import pallas as pl
from jax.experimental.pallas import tpu as pltpu

F32 = jnp.float32
BF16 = jnp.bfloat16

D_MODEL = 1024
HG_HEADS = 8
HG_DK = 128
HG_DV = 128
HG_WIDTH = HG_HEADS * HG_DK
HG_CHUNK = 64
NSA_HEADS = 16
NSA_KV = 4
NSA_GROUP = NSA_HEADS // NSA_KV
NSA_DH = 64
NSA_WIDTH = NSA_HEADS * NSA_DH
NSA_KVW = NSA_KV * NSA_DH
CMP_LEN = 32
CMP_STRIDE = 16
CMP_HIDDEN = 256
SEL_BLOCK = 64
SEL_TOPK = 16
WINDOW = 512
ROPE_THETA = 500000.0
ROPE_DIM = NSA_DH // 4
D_FF = 4 * D_MODEL
EPS = 1e-6
NEG = -1e30
ATT_SCALE = NSA_DH ** -0.5
LOG2E = float(np.log2(np.e))

LANES = 128
SUBLANES_BF16 = 16
VMEM_LIMIT = 56 * 1024 * 1024

TOK_TILE = 512
ATT_Q = 128
ATT_KT = 1024
ATT_SUBTILES = 8
ATT_LOOKAHEAD = 7
HG_TOK = 512

P_HQ, P_HI, P_HG, P_HF = 0, HG_WIDTH, 2 * HG_WIDTH, 3 * HG_WIDTH
P_NQ = 4 * HG_WIDTH
P_KV = P_NQ + NSA_WIDTH
P_NG = P_KV + 6 * NSA_KVW
P_GA = 7 * D_MODEL
P_GB = 8 * D_MODEL
P_WIDTH = 9 * D_MODEL
HG_EXP2_LIMIT = 115.0


def _cparams(sem):
    return pltpu.CompilerParams(dimension_semantics=sem, vmem_limit_bytes=VMEM_LIMIT)


def _rms(x, gain):
    ms = jnp.mean(x * x, axis=-1, keepdims=True)
    return (x * lax.rsqrt(ms + EPS)) * gain


def _silu(x):
    return x * (0.5 * jnp.tanh(0.5 * x) + 0.5)


def _dot(a, b):
    return jnp.dot(a, b, preferred_element_type=F32)


def _dot_nt(a, b):
    return lax.dot_general(a, b, (((1,), (1,)), ((), ())), preferred_element_type=F32)


def _dot_tn(a, b):
    return lax.dot_general(a, b, (((0,), (0,)), ((), ())), preferred_element_type=F32)


def _colmax(s):
    parts = [s[i:i + 64] for i in range(0, s.shape[0], 64)] if s.shape[0] % 64 == 0 else [s]
    while len(parts) > 1:
        parts = [jnp.maximum(parts[i], parts[i + 1]) if i + 1 < len(parts) else parts[i]
                 for i in range(0, len(parts), 2)]
    return jnp.max(parts[0], axis=0, keepdims=True)


def _split3(x):
    hi = x.astype(BF16)
    r1 = x - hi.astype(F32)
    mid = r1.astype(BF16)
    lo = (r1 - mid.astype(F32)).astype(BF16)
    return hi, mid, lo


def _rms_matmul_kernel(x_ref, g_ref, w_ref, o_ref, u_scr):
    @pl.when(pl.program_id(1) == 0)
    def _():
        u_scr[...] = _rms(x_ref[...], g_ref[...]).astype(BF16)

    o_ref[...] = _dot(u_scr[...], w_ref[...]).astype(o_ref.dtype)


def _rms_matmul(x, gain, w, out_dtype, tn):
    n, d = x.shape
    wd = w.shape[1]
    tm = min(2 * TOK_TILE, n)
    return pl.pallas_call(
        _rms_matmul_kernel,
        grid=(n // tm, wd // tn),
        in_specs=[pl.BlockSpec((tm, d), lambda i, j: (i, 0)),
                  pl.BlockSpec((1, d), lambda i, j: (0, 0)),
                  pl.BlockSpec((d, tn), lambda i, j: (0, j))],
        out_specs=pl.BlockSpec((tm, tn), lambda i, j: (i, j)),
        out_shape=jax.ShapeDtypeStruct((n, wd), out_dtype),
        scratch_shapes=[pltpu.VMEM((tm, d), BF16)],
        compiler_params=_cparams(("parallel", "arbitrary")),
    )(x, gain, w)


def _hgrn_levels():
    return [HG_CHUNK >> (i + 1) for i in range(HG_CHUNK.bit_length() - 1)]


def _hgrn_intra_exact(qf, kk, b, gsel_ref):
    c_len = HG_CHUNK
    row = lax.broadcasted_iota(jnp.int32, (c_len, c_len), 0)
    col = lax.broadcasted_iota(jnp.int32, (c_len, c_len), 1)
    trow = lax.broadcasted_iota(jnp.int32, (c_len, HG_DK), 0)
    a = jnp.where(row == col, _dot_nt(qf.astype(BF16), kk.astype(BF16)), 0.0)
    b3 = jnp.concatenate(_split3(b), axis=0)
    for lvl, h in enumerate(_hgrn_levels()):
        b_at_ref = _dot(gsel_ref[lvl], b3)
        upper = (trow & (2 * h - 1)) >= h
        qe = jnp.where(upper, qf * jnp.exp2(jnp.minimum(b - b_at_ref, 0.0)), 0.0)
        ke = jnp.where(upper, 0.0, kk * jnp.exp2(jnp.minimum(b_at_ref - b, 0.0)))
        same_block = (row & -(2 * h)) == (col & -(2 * h))
        block = same_block & ((row & (2 * h - 1)) >= h) & ((col & (2 * h - 1)) < h)
        a = a + jnp.where(block, _dot_nt(qe.astype(BF16), ke.astype(BF16)), 0.0)
    return a


def _hgrn_kernel(q_ref, i_ref, g_ref, f_ref, lbl_ref, gn_ref, tri_ref, gsel_ref, o_ref, st_ref, a_scr, *,
                 n_chunks):
    c_len = HG_CHUNK
    half = c_len // 2

    @pl.when(pl.program_id(2) == 0)
    def _():
        st_ref[...] = jnp.zeros_like(st_ref)

    lg = lbl_ref[...]
    mx = jnp.max(lg, axis=0, keepdims=True)
    ex = jnp.exp(lg - mx)
    lb = ex[0:1, :] / jnp.sum(ex, axis=0, keepdims=True)

    row = lax.broadcasted_iota(jnp.int32, (c_len, c_len), 0)
    col = lax.broadcasted_iota(jnp.int32, (c_len, c_len), 1)
    causal = col <= row
    tri3 = tri_ref[...]

    chunks = [slice(c * c_len, (c + 1) * c_len) for c in range(n_chunks)]
    qf, kk, lf = [], [], []
    for rows in chunks:
        qx = q_ref[0, rows, :].astype(F32)
        qf.append(_silu(qx) * (HG_DK ** -0.5))
        forget = lb + (1.0 - lb) * jax.nn.sigmoid(f_ref[0, rows, :].astype(F32))
        kk.append(1.0 - forget)
        lf.append(jnp.log2(forget))
    bb = [_dot(tri3, jnp.concatenate(_split3(x), axis=0)) for x in lf]
    qe, ke, qb, kd, dlast = [], [], [], [], []
    for c in range(n_chunks):
        b = bb[c]
        b_mid = b[half - 1:half, :]
        b_last = b[c_len - 1:c_len, :]
        qe.append((qf[c] * jnp.exp2(b - b_mid)).astype(BF16))
        ke.append((kk[c] * jnp.exp2(b_mid - b)).astype(BF16))
        qb.append((qf[c] * jnp.exp2(b)).astype(BF16))
        kd.append((kk[c] * jnp.exp2(b_last - b)).astype(BF16))
        dlast.append(jnp.exp2(b_last))
    vv = [i_ref[0, rows, :] for rows in chunks]
    for c in range(n_chunks):
        a_scr[c] = _dot_nt(qe[c], ke[c])
    kv = [_dot_tn(vv[c], kd[c]) for c in range(n_chunks)]
    states = [st_ref[...]]
    for c in range(n_chunks):
        states.append(states[c] * dlast[c] + kv[c])
    o_inter = [_dot_nt(qb[c], states[c].astype(BF16)) for c in range(n_chunks)]
    st_ref[...] = states[n_chunks]

    need = jnp.zeros((1, HG_DK), F32)
    for b in bb:
        b_mid = b[half - 1:half, :]
        need = jnp.maximum(need, jnp.maximum(b[0:1, :] - b_mid, b_mid - b[c_len - 1:c_len, :]))

    @pl.when(jnp.max(need) > HG_EXP2_LIMIT)
    def _():
        for c in range(n_chunks):
            a_scr[c] = _hgrn_intra_exact(qf[c], kk[c], bb[c], gsel_ref)

    for c, rows in enumerate(chunks):
        o_intra = _dot(jnp.where(causal, a_scr[c], 0.0).astype(BF16), vv[c])
        gx = g_ref[0, rows, :].astype(F32)
        y = _rms(o_intra + o_inter[c], gn_ref[...]) * _silu(gx)
        o_ref[0, rows, :] = y.astype(o_ref.dtype)


def _hgrn(proj, lb_logits, gnorm, bsz, seq):
    n_chunks = HG_TOK // HG_CHUNK
    tri = np.tril(np.ones((HG_CHUNK, HG_CHUNK), np.float32))
    tri3 = jnp.asarray(np.concatenate([tri, tri, tri], axis=1), BF16)
    t_idx = np.arange(HG_CHUNK)
    gsel = np.zeros((len(_hgrn_levels()), HG_CHUNK, HG_CHUNK), np.float32)
    for lvl, h in enumerate(_hgrn_levels()):
        gsel[lvl, t_idx, (t_idx // (2 * h)) * (2 * h) + h - 1] = 1.0
    gsel3 = jnp.asarray(np.concatenate([gsel, gsel, gsel], axis=2), BF16)
    hh = HG_HEADS
    blk = (1, HG_TOK, HG_DK)
    head_cols = lambda off: pl.BlockSpec(blk, lambda b, h, t: (b, t, off // HG_DK + h))
    return pl.pallas_call(
        functools.partial(_hgrn_kernel, n_chunks=n_chunks),
        grid=(bsz, hh, seq // HG_TOK),
        in_specs=[head_cols(P_HQ), head_cols(P_HI), head_cols(P_HG), head_cols(P_HF),
                  pl.BlockSpec((2, HG_DK), lambda b, h, t: (0, h)),
                  pl.BlockSpec((1, HG_DV), lambda b, h, t: (0, 0)),
                  pl.BlockSpec((HG_CHUNK, 3 * HG_CHUNK), lambda b, h, t: (0, 0)),
                  pl.BlockSpec(gsel3.shape, lambda b, h, t: (0, 0, 0))],
        out_specs=pl.BlockSpec(blk, lambda b, h, t: (b, t, h)),
        out_shape=jax.ShapeDtypeStruct((bsz, seq, HG_WIDTH), BF16),
        scratch_shapes=[pltpu.VMEM((HG_DV, HG_DK), F32), pltpu.VMEM((n_chunks, HG_CHUNK, HG_CHUNK), F32)],
        compiler_params=_cparams(("parallel", "parallel", "arbitrary")),
    )(proj, proj, proj, proj, lb_logits, gnorm, tri3, gsel3)


def _rope_kernel(x_ref, c_ref, s1_ref, s2_ref, o_ref, *, scale):
    width = x_ref.shape[-1]
    c = c_ref[...]
    s1 = s1_ref[...]
    s2 = s2_ref[...]
    for j in range(width // LANES):
        cols = slice(j * LANES, (j + 1) * LANES)
        x = x_ref[0, :, cols].astype(F32)
        if scale != 1.0:
            x = x * scale
        o_ref[0, :, cols] = _rope_lanes(x, c, s1, s2).astype(o_ref.dtype)


def _rope(src, col_block, width, tables, scale, bsz, seq):
    tt = min(TOK_TILE, seq)
    tab = pl.BlockSpec((tt, LANES), lambda b, t: (t, 0))
    return pl.pallas_call(
        functools.partial(_rope_kernel, scale=scale),
        grid=(bsz, seq // tt),
        in_specs=[pl.BlockSpec((1, tt, width), lambda b, t: (b, t, col_block)), tab, tab, tab],
        out_specs=pl.BlockSpec((1, tt, width), lambda b, t: (b, t, 0)),
        out_shape=jax.ShapeDtypeStruct((bsz, seq, width), BF16),
        compiler_params=_cparams(("parallel", "parallel")),
    )(src, *tables)


def _rope_lanes(x, c, s1, s2):
    return x * c + pltpu.roll(x, LANES - ROPE_DIM // 2, 1) * s1 + pltpu.roll(x, ROPE_DIM // 2, 1) * s2


def _kv_prep_kernel(k_ref, v_ref, c_ref, s1_ref, s2_ref, ko_ref, vo_ref, *, pad_blocks, with_block_id):
    tt = k_ref.shape[1]
    step = pl.program_id(1)
    is_pad = step < pad_blocks
    pos = (step - pad_blocks) * tt + lax.broadcasted_iota(jnp.int32, (tt, NSA_DH), 0)
    lane = lax.broadcasted_iota(jnp.int32, (tt, NSA_DH), 1)
    if with_block_id:
        extra = jnp.where(pos // SEL_BLOCK == lane, 1.0, 0.0)
    else:
        extra = jnp.where((lane == 0) & is_pad, 1.0, 0.0)
    c, s1, s2 = c_ref[...], s1_ref[...], s2_ref[...]
    vt = v_ref[0].astype(F32).T
    for j in range(NSA_KVW // LANES):
        y = _rope_lanes(k_ref[0, :, j * LANES:(j + 1) * LANES].astype(F32), c, s1, s2)
        y = jnp.where(is_pad, 0.0, y)
        for g2 in range(LANES // NSA_DH):
            g = j * (LANES // NSA_DH) + g2
            ko_ref[0, g] = jnp.concatenate([y[:, g2 * NSA_DH:(g2 + 1) * NSA_DH], extra], axis=1).astype(BF16)
    for g in range(NSA_KV):
        vo_ref[0, g, 0:NSA_DH, :] = jnp.where(is_pad, 0.0, vt[g * NSA_DH:(g + 1) * NSA_DH]).astype(BF16)
        vo_ref[0, g, NSA_DH:NSA_DH + SUBLANES_BF16, :] = jnp.ones((SUBLANES_BF16, tt), BF16)


def _kv_prep(proj, k_block, v_block, tables, pad_rows, with_block_id, bsz, seq):
    tt = min(TOK_TILE, seq)
    pad_blocks = pad_rows // tt
    assert pad_blocks * tt == pad_rows
    src_t = lambda t: jnp.maximum(t - pad_blocks, 0)
    tab = pl.BlockSpec((tt, LANES), lambda b, t: (src_t(t), 0))
    rows = seq + pad_rows
    return pl.pallas_call(
        functools.partial(_kv_prep_kernel, pad_blocks=pad_blocks, with_block_id=with_block_id),
        grid=(bsz, rows // tt),
        in_specs=[pl.BlockSpec((1, tt, NSA_KVW), lambda b, t: (b, src_t(t), k_block)),
                  pl.BlockSpec((1, tt, NSA_KVW), lambda b, t: (b, src_t(t), v_block)), tab, tab, tab],
        out_specs=[pl.BlockSpec((1, NSA_KV, tt, 2 * NSA_DH), lambda b, t: (b, 0, t, 0)),
                   pl.BlockSpec((1, NSA_KV, NSA_DH + SUBLANES_BF16, tt), lambda b, t: (b, 0, 0, t))],
        out_shape=[jax.ShapeDtypeStruct((bsz, NSA_KV, rows, 2 * NSA_DH), BF16),
                   jax.ShapeDtypeStruct((bsz, NSA_KV, NSA_DH + SUBLANES_BF16, rows), BF16)],
        compiler_params=_cparams(("parallel", "parallel")),
    )(proj, proj, *tables)


def _rope_tables(seq):
    half = ROPE_DIM // 2
    inv = jnp.asarray(ROPE_THETA ** (-np.arange(half) * 2.0 / ROPE_DIM), F32)
    ang = jnp.arange(seq, dtype=F32)[:, None] * inv[None, :]
    cos, sin = jnp.cos(ang), jnp.sin(ang)
    lane = np.arange(LANES) % NSA_DH
    pick = lane % half
    cos_l = cos[:, pick]
    sin_l = sin[:, pick]
    first = jnp.asarray(lane < half)
    second = jnp.asarray((lane >= half) & (lane < ROPE_DIM))
    c = jnp.where(first | second, cos_l, 1.0)
    s1 = jnp.where(first, -sin_l, 0.0)
    s2 = jnp.where(second, sin_l, 0.0)
    return c, s1, s2


def _gelu_tanh(x):
    return 0.5 * x * (1.0 + jnp.tanh(np.sqrt(2.0 / np.pi).astype(np.float32) * (x + 0.044715 * (x * x * x))))


def _compress_kernel(xk_ref, xv_ref, pek_ref, pev_ref, wk1_ref, wk2_ref, wv1_ref, wv2_ref, ok_ref, ov_ref):
    half_w = CMP_STRIDE * NSA_DH

    def one(x_ref, pe_ref, w1_ref, w2_ref):
        x = x_ref[0, 0]
        n_slab = x.shape[0]
        p0 = _dot(x, w1_ref[0:half_w, :])
        p1 = _dot(x, w1_ref[half_w:2 * half_w, :])
        pe = jnp.broadcast_to(pe_ref[...], (8, 2 * half_w)).astype(BF16)
        pw = _dot(pe, w1_ref[...])[0:1, :]
        h = p0 + pltpu.roll(p1, n_slab - 1, 0) + pw
        return _dot(_gelu_tanh(h).astype(BF16), w2_ref[...])

    ok_ref[0, 0] = one(xk_ref, pek_ref, wk1_ref, wk2_ref).astype(ok_ref.dtype)
    ov_ref[0, 0] = one(xv_ref, pev_ref, wv1_ref, wv2_ref).astype(ov_ref.dtype)


def _compress(xk, xv, pek, pev, wk1, wk2, wv1, wv2):
    bsz, grp, n_slab, wdt = xk.shape
    xs = pl.BlockSpec((1, 1, n_slab, wdt), lambda b, g: (b, g, 0, 0))
    full = lambda a: pl.BlockSpec(a.shape, lambda b, g: (0,) * a.ndim)
    osd = jax.ShapeDtypeStruct((bsz, grp, n_slab, NSA_DH), BF16)
    os_ = pl.BlockSpec((1, 1, n_slab, NSA_DH), lambda b, g: (b, g, 0, 0))
    return pl.pallas_call(
        _compress_kernel,
        grid=(bsz, grp),
        in_specs=[xs, xs, full(pek), full(pev), full(wk1), full(wk2), full(wv1), full(wv2)],
        out_specs=[os_, os_],
        out_shape=[osd, osd],
        compiler_params=_cparams(("parallel", "parallel")),
    )(xk, xv, pek, pev, wk1, wk2, wv1, wv2)


def _attn_kernel(nq_ref, qr_ref, kcmp_ref, vcmpt_ref, ks_ref, vst_ref, kw_ref, vwt_ref, ng_ref, ovt_ref,
                 o_ref, gt_scr, qaug_scr, *, seq):
    qn = ATT_Q
    grp = NSA_GROUP
    ln = grp * qn
    n_sel = seq // SEL_BLOCK
    top = min(SEL_TOPK, n_sel)
    g_id = pl.program_id(1)
    q0 = pl.multiple_of(pl.program_id(2) * qn, qn)

    def stack_heads(x):
        return jnp.concatenate([x[:, r * NSA_DH:(r + 1) * NSA_DH] for r in range(grp)], axis=0)

    def add_per_head(s, bias):
        return jnp.concatenate([s[:, r * qn:(r + 1) * qn] + bias for r in range(grp)], axis=1)

    qs = stack_heads(nq_ref[0])
    qrs = stack_heads(qr_ref[0])
    t_lane = q0 + (lax.broadcasted_iota(jnp.int32, (1, ln), 1) & (qn - 1))
    t_q = q0 + lax.broadcasted_iota(jnp.int32, (1, qn), 1)

    n_slab = kcmp_ref.shape[2]
    wlen = WINDOW + qn
    pad_col = jnp.where(lax.broadcasted_iota(jnp.int32, (ln, NSA_DH), 1) == 0, NEG, 0.0).astype(BF16)
    sc = _dot_nt(kcmp_ref[0, 0], qs) * (ATT_SCALE * LOG2E)
    sw = _dot_nt(kw_ref[0, 0, pl.ds(q0, wlen), :], jnp.concatenate([qrs, pad_col], axis=1))
    sd = _dot_nt(ks_ref[0, 0, pl.ds(q0, qn), :][:, 0:NSA_DH], qrs)

    cend = lax.broadcasted_iota(jnp.int32, (n_slab, 1), 0) * CMP_STRIDE + (CMP_LEN - 1)
    sm = add_per_head(sc, jnp.where(cend <= t_q, 0.0, NEG))
    e = jnp.exp2(sm - _colmax(sm))
    den = jnp.sum(e, axis=0, keepdims=True)
    p = e * jnp.where(t_lane >= CMP_LEN - 1, 1.0 / den, 0.0)
    oc = _dot(vcmpt_ref[0, 0], p.astype(BF16))

    psum = p[:, 0:qn]
    for r in range(1, grp):
        psum = psum + p[:, r * qn:(r + 1) * qn]
    hi, mid, lo = _split3(psum)
    ovt = ovt_ref[...]
    imp = _dot(ovt, hi) + _dot(ovt, mid) + _dot(ovt, lo)

    i_loc = lax.broadcasted_iota(jnp.int32, (qn, 1), 0)
    j_loc = lax.broadcasted_iota(jnp.int32, (1, qn), 1)
    sw = jnp.concatenate([add_per_head(sw[0:qn], jnp.where(i_loc > j_loc, 0.0, NEG)),
                          sw[qn:WINDOW],
                          add_per_head(sw[WINDOW:wlen], jnp.where(i_loc <= j_loc, 0.0, NEG))], axis=0)
    pw = jnp.exp2(sw - _colmax(sw))
    rw = _dot(vwt_ref[0, 0, :, pl.ds(q0, wlen)], pw.astype(BF16))
    ow = rw[0:NSA_DH] / rw[NSA_DH:NSA_DH + 1]

    kpos = q0 + lax.broadcasted_iota(jnp.int32, (qn, 1), 0)
    sd = add_per_head(sd, jnp.where(kpos <= t_q, 0.0, NEG))
    m0 = _colmax(sd)
    acc0 = _dot(vst_ref[0, 0, :, pl.ds(q0, qn)], jnp.exp2(sd - m0).astype(BF16))

    gt_scr[...] = jax.nn.sigmoid(ng_ref[0].astype(F32)).T

    def gate_row(branch):
        return jnp.concatenate(
            [gt_scr[pl.ds(branch * NSA_HEADS + g_id * grp + r, 1), :] for r in range(grp)], axis=1)

    out_cw = gate_row(0) * oc + gate_row(2) * ow
    gate_sel = gate_row(1)

    jidx = lax.broadcasted_iota(jnp.int32, (n_sel, qn), 0)
    tq = q0 + lax.broadcasted_iota(jnp.int32, (n_sel, qn), 1)
    cur = tq // SEL_BLOCK
    forced = (jidx == 0) | (jidx == cur) | (jidx == cur - 1)
    causal_blk = jidx * SEL_BLOCK <= tq
    val = jnp.where(forced, -2.0, imp)
    val = jnp.where(causal_blk, val, -1.0)
    rows8 = 8
    groups = [val[v * rows8:(v + 1) * rows8] for v in range(n_sel // rows8)]
    counts = [jnp.zeros((rows8, qn), F32) for _ in groups]
    jrow = lax.broadcasted_iota(jnp.int32, (rows8, qn), 0)
    for jp in range(n_sel):
        other = jnp.broadcast_to(val[jp:jp + 1, :], (rows8, qn))
        for v, grp_val in enumerate(groups):
            ge = jnp.where(other >= grp_val, 1.0, 0.0)
            gt = jnp.where(other > grp_val, 1.0, 0.0)
            if v * rows8 > jp:
                before = ge
            elif (v + 1) * rows8 <= jp:
                before = gt
            else:
                before = jnp.where(jrow > jp - v * rows8, ge, gt)
            counts[v] = counts[v] + before
    rank = jnp.concatenate(counts, axis=0)
    sel = jnp.where(forced | (rank < float(max(top - 3, 0))), 1.0, 0.0)
    diag_blk = q0 // SEL_BLOCK
    in_diag = (jidx >= diag_blk) & (jidx < diag_blk + qn // SEL_BLOCK)
    selb = jnp.where((sel > 0.0) & causal_blk & jnp.logical_not(in_diag), 0.0, NEG)
    selb_t = jnp.concatenate([selb, jnp.zeros((LANES - n_sel, qn), F32)], axis=0).T
    selb_t = selb_t[:, 0:NSA_DH].astype(BF16)
    qaug_scr[...] = jnp.concatenate([qrs, jnp.concatenate([selb_t] * grp, axis=0)], axis=1)

    kt_len = ATT_KT
    sub = kt_len // ATT_SUBTILES

    def sweep_block(k_base, n_sub, carry):
        m_run, acc = carry
        k0s = [pl.multiple_of(k_base + j * sub, sub) for j in range(n_sub)]
        score = lambda j: _dot_nt(ks_ref[0, 0, pl.ds(k0s[j], sub), :], qaug_scr[...])
        ahead = min(ATT_LOOKAHEAD, n_sub)
        scores = [score(j) for j in range(ahead)]
        for j in range(n_sub):
            s = scores[j]
            m_new = jnp.maximum(m_run, _colmax(s))
            alpha = jnp.exp2(m_run - m_new)
            pt = jnp.exp2(s - m_new).astype(BF16)
            acc = alpha * acc + _dot(vst_ref[0, 0, :, pl.ds(k0s[j], sub)], pt)
            m_run = m_new
            if j + ahead < n_sub:
                scores.append(score(j + ahead))
        return m_run, acc

    n_whole = q0 // kt_len
    carry = lax.fori_loop(0, n_whole, lambda kt, cy: sweep_block(kt * kt_len, ATT_SUBTILES, cy), (m0, acc0))
    tails = [functools.partial(sweep_block, n_whole * kt_len, r) for r in range(ATT_SUBTILES)]
    _, acc = lax.switch((q0 - n_whole * kt_len) // sub, tails, carry)
    osel = acc[0:NSA_DH] / acc[NSA_DH:NSA_DH + 1]

    out_t = out_cw + gate_sel * osel
    pieces = []
    for r2 in range(grp // 2):
        pair = jnp.concatenate([out_t[:, (2 * r2) * qn:(2 * r2 + 1) * qn],
                                out_t[:, (2 * r2 + 1) * qn:(2 * r2 + 2) * qn]], axis=0)
        pieces.append(pair.T)
    o_ref[0] = jnp.concatenate(pieces, axis=1).astype(o_ref.dtype)


def _attention(proj, q_rot, k_cmp, v_cmp_t, ks, vs_t, kw, vw_t, ov_t, bsz, seq):
    qn = ATT_Q
    gw = NSA_GROUP * NSA_DH
    n_slab = k_cmp.shape[2]
    n_sel = seq // SEL_BLOCK
    qspec = pl.BlockSpec((1, qn, gw), lambda b, g, i: (b, i, g))
    per_group = lambda a: pl.BlockSpec((1, 1) + a.shape[2:], lambda b, g, i: (b, g, 0, 0))
    return pl.pallas_call(
        functools.partial(_attn_kernel, seq=seq),
        grid=(bsz, NSA_KV, seq // qn),
        in_specs=[pl.BlockSpec((1, qn, gw), lambda b, g, i: (b, i, P_NQ // gw + g)), qspec,
                  per_group(k_cmp), per_group(v_cmp_t), per_group(ks), per_group(vs_t),
                  per_group(kw), per_group(vw_t),
                  pl.BlockSpec((1, qn, LANES), lambda b, g, i: (b, i, P_NG // LANES)),
                  pl.BlockSpec((n_sel, n_slab), lambda b, g, i: (0, 0))],
        out_specs=qspec,
        out_shape=jax.ShapeDtypeStruct((bsz, seq, NSA_WIDTH), BF16),
        scratch_shapes=[pltpu.VMEM((LANES, qn), F32), pltpu.VMEM((NSA_GROUP * qn, 2 * NSA_DH), BF16)],
        compiler_params=_cparams(("parallel", "parallel", "arbitrary")),
    )(proj, q_rot, k_cmp, v_cmp_t, ks, vs_t, kw, vw_t, proj, ov_t)


def _merge_kernel(x_ref, ya_ref, yb_ref, ga_ref, gb_ref, wa_ref, wb_ref, wo_ref, gn_ref, o_ref):
    ga = jax.nn.sigmoid(ga_ref[...].astype(F32))
    gb = jax.nn.sigmoid(gb_ref[...].astype(F32))
    merged = ga * _dot(ya_ref[...], wa_ref[...]) + gb * _dot(yb_ref[...], wb_ref[...])
    mo = _dot(merged.astype(BF16), wo_ref[...])
    o_ref[...] = x_ref[...] + _rms(mo, gn_ref[...])


def _merge(x, ya, yb, gates, wa, wb, wo, gn):
    n, d = x.shape
    tm = min(TOK_TILE, n)
    row = lambda c: pl.BlockSpec((tm, d), lambda i: (i, c))
    wsp = pl.BlockSpec((d, d), lambda i: (0, 0))
    return pl.pallas_call(
        _merge_kernel,
        grid=(n // tm,),
        in_specs=[row(0), row(0), row(0), row(P_GA // d), row(P_GB // d), wsp, wsp, wsp,
                  pl.BlockSpec((1, d), lambda i: (0, 0))],
        out_specs=row(0),
        out_shape=jax.ShapeDtypeStruct((n, d), F32),
        compiler_params=_cparams(("parallel",)),
    )(x, ya, yb, gates, gates, wa, wb, wo, gn)


def _mlp_kernel(h_ref, gpre_ref, wu_ref, wd_ref, gpost_ref, o_ref, v_scr, acc_scr):
    j = pl.program_id(1)

    @pl.when(j == 0)
    def _():
        v_scr[...] = _rms(h_ref[...], gpre_ref[...]).astype(BF16)
        acc_scr[...] = jnp.zeros_like(acc_scr)

    up = _dot(v_scr[...], wu_ref[...])
    act = jnp.square(jnp.maximum(up, 0.0))
    acc_scr[...] += _dot(act.astype(BF16), wd_ref[...])

    @pl.when(j == pl.num_programs(1) - 1)
    def _():
        o_ref[...] = h_ref[...] + _rms(acc_scr[...], gpost_ref[...])


def _mlp(h, gpre, wu, wd, gpost):
    n, d = h.shape
    dff = wu.shape[1]
    tm = min(2 * TOK_TILE, n)
    tf = 1024
    vec = pl.BlockSpec((1, d), lambda i, j: (0, 0))
    return pl.pallas_call(
        _mlp_kernel,
        grid=(n // tm, dff // tf),
        in_specs=[pl.BlockSpec((tm, d), lambda i, j: (i, 0)), vec,
                  pl.BlockSpec((d, tf), lambda i, j: (0, j)),
                  pl.BlockSpec((tf, d), lambda i, j: (j, 0)), vec],
        out_specs=pl.BlockSpec((tm, d), lambda i, j: (i, 0)),
        out_shape=jax.ShapeDtypeStruct((n, d), F32),
        scratch_shapes=[pltpu.VMEM((tm, d), BF16), pltpu.VMEM((tm, d), F32)],
        compiler_params=_cparams(("parallel", "arbitrary")),
    )(h, gpre, wu, wd, gpost)


def _ple_kernel(h_ref, p_ref, wp_ref, wg_ref, gn_ref, o_ref):
    h = h_ref[...]
    e = _dot(p_ref[...].astype(BF16), wp_ref[...]) * jax.nn.sigmoid(_dot(h.astype(BF16), wg_ref[...]))
    o_ref[...] = h + _rms(e, gn_ref[...])


def _ple(h, p, wp, wg, gn):
    n, d = h.shape
    pd = p.shape[1]
    tm = min(TOK_TILE, n)
    return pl.pallas_call(
        _ple_kernel,
        grid=(n // tm,),
        in_specs=[pl.BlockSpec((tm, d), lambda i: (i, 0)),
                  pl.BlockSpec((tm, pd), lambda i: (i, 0)),
                  pl.BlockSpec((pd, d), lambda i: (0, 0)),
                  pl.BlockSpec((d, d), lambda i: (0, 0)),
                  pl.BlockSpec((1, d), lambda i: (0, 0))],
        out_specs=pl.BlockSpec((tm, d), lambda i: (i, 0)),
        out_shape=jax.ShapeDtypeStruct((n, d), F32),
        compiler_params=_cparams(("parallel",)),
    )(h, p, wp, wg, gn)


def _overlap_t(seq):
    n_slab = seq // CMP_STRIDE
    n_cmp = (seq - CMP_LEN) // CMP_STRIDE + 1
    n_sel = seq // SEL_BLOCK
    start = np.arange(n_slab) * CMP_STRIDE
    end = start + CMP_LEN - 1
    sel_start = np.arange(n_sel) * SEL_BLOCK
    ov = (start[None, :] < sel_start[:, None] + SEL_BLOCK) & (end[None, :] >= sel_start[:, None])
    ov = ov & (np.arange(n_slab)[None, :] < n_cmp)
    return jnp.asarray(ov.astype(np.float32), BF16)


def kernel(x, p, w_in, w_branch_a, w_branch_b, w_out, norm_pre_mix, norm_post_mix, norm_pre_mlp,
           norm_post_mlp, hg_lb_logits, hg_gnorm, cmp_pe_k, cmp_pe_v, cmp_wk1, cmp_wk2, cmp_wv1, cmp_wv2,
           w_up, w_down, w_ple, w_ple_gate, norm_ple):
    bsz, seq, d = x.shape
    depth = w_in.shape[0]
    assert depth == 1 and hg_lb_logits.shape[0] == 2 and d == D_MODEL
    assert seq % ATT_KT == 0 and seq >= WINDOW + ATT_Q and seq // SEL_BLOCK <= NSA_DH
    assert ATT_Q in (SEL_BLOCK, 2 * SEL_BLOCK)
    assert ATT_Q % (ATT_KT // ATT_SUBTILES) == 0
    n = bsz * seq
    x2 = x.reshape(n, d)
    row = lambda a: a.reshape(1, -1)

    w = w_in[0]
    s_hq, s_hf, s_hi, s_hg = 0, HG_WIDTH, 2 * HG_WIDTH, 3 * HG_WIDTH
    s_nq = 4 * HG_WIDTH
    s_ng = s_nq + NSA_WIDTH + 6 * NSA_KVW
    s_ga = s_ng + 3 * NSA_HEADS
    col = lambda a, b: w[:, a:b]
    w_all = jnp.concatenate(
        [col(s_hq, s_hf), col(s_hi, s_hg), col(s_hg, s_nq), col(s_hf, s_hi), col(s_nq, s_ga),
         jnp.zeros((d, P_GA - P_NG - 3 * NSA_HEADS), F32), col(s_ga, s_ga + 2 * D_MODEL)], axis=1).astype(BF16)
    assert w_all.shape[1] == P_WIDTH
    proj2 = _rms_matmul(x2, row(norm_pre_mix[0]), w_all, BF16, 1024)
    proj = proj2.reshape(bsz, seq, P_WIDTH)

    y_a = _hgrn(proj, hg_lb_logits, row(hg_gnorm[0]), bsz, seq)

    tables = _rope_tables(seq)
    kvb = P_KV // NSA_KVW
    q_rot = _rope(proj, P_NQ // NSA_WIDTH, NSA_WIDTH, tables, ATT_SCALE * LOG2E, bsz, seq)
    ks_aug, vs_t = _kv_prep(proj, kvb + 2, kvb + 3, tables, 0, True, bsz, seq)
    kw_aug, vw_t = _kv_prep(proj, kvb + 4, kvb + 5, tables, WINDOW, False, bsz, seq)
    kv_slice = lambda i: proj[:, :, P_KV + i * NSA_KVW: P_KV + (i + 1) * NSA_KVW]
    heads = lambda t: t.reshape(bsz, seq, NSA_KV, NSA_DH).transpose(0, 2, 1, 3)
    slabs = lambda t: heads(t).reshape(bsz, NSA_KV, seq // CMP_STRIDE, CMP_STRIDE * NSA_DH)
    k_cmp, v_cmp = _compress(
        slabs(kv_slice(0)), slabs(kv_slice(1)), cmp_pe_k[0].reshape(1, -1), cmp_pe_v[0].reshape(1, -1),
        cmp_wk1[0].astype(BF16), cmp_wk2[0].astype(BF16), cmp_wv1[0].astype(BF16), cmp_wv2[0].astype(BF16))
    y_b = _attention(proj, q_rot, k_cmp, v_cmp.transpose(0, 1, 3, 2), ks_aug, vs_t, kw_aug, vw_t,
                     _overlap_t(seq), bsz, seq)

    h1 = _merge(x2, y_a.reshape(n, -1), y_b.reshape(n, -1), proj2, w_branch_a[0].astype(BF16),
                w_branch_b[0].astype(BF16), w_out[0].astype(BF16), row(norm_post_mix[0]))
    h2 = _mlp(h1, row(norm_pre_mlp[0]), w_up[0].astype(BF16), w_down[0].astype(BF16), row(norm_post_mlp[0]))
    h3o = _ple(h2, p[0].reshape(n, -1), w_ple[0].astype(BF16), w_ple_gate[0].astype(BF16), row(norm_ple[0]))
    return h3o.reshape(bsz, seq, d)
```

```python
import functools

import numpy as np
import jax
import jax.numpy as jnp
from jax import lax
from jax.experimental import pallas as pl
from jax.experimental.pallas import tpu as pltpu

F32 = jnp.float32
BF16 = jnp.bfloat16

D_MODEL = 1024
HG_HEADS = 8
HG_DK = 128
HG_DV = 128
HG_WIDTH = HG_HEADS * HG_DK
HG_CHUNK = 64
NSA_HEADS = 16
NSA_KV = 4
NSA_GROUP = NSA_HEADS // NSA_KV
NSA_DH = 64
NSA_WIDTH = NSA_HEADS * NSA_DH
NSA_KVW = NSA_KV * NSA_DH
CMP_LEN = 32
CMP_STRIDE = 16
CMP_HIDDEN = 256
SEL_BLOCK = 64
SEL_TOPK = 16
WINDOW = 512
ROPE_THETA = 500000.0
ROPE_DIM = NSA_DH // 4
D_FF = 4 * D_MODEL
EPS = 1e-6
NEG = -1e30
ATT_SCALE = NSA_DH ** -0.5
LOG2E = float(np.log2(np.e))

LANES = 128
SUBLANES_BF16 = 16
VMEM_LIMIT = 56 * 1024 * 1024

TOK_TILE = 512
ATT_Q = 128
ATT_KT = 1024
ATT_SUBTILES = 8
ATT_LOOKAHEAD = 7
HG_TOK = 1024
HG_GROUP = 16

P_HQ, P_HI, P_HG, P_HF = 0, HG_WIDTH, 2 * HG_WIDTH, 3 * HG_WIDTH
P_NQ = 4 * HG_WIDTH
P_KV = P_NQ + NSA_WIDTH
P_NG = P_KV + 6 * NSA_KVW
P_GA = 7 * D_MODEL
P_GB = 8 * D_MODEL
P_WIDTH = 9 * D_MODEL
HG_EXP2_LIMIT = 115.0


def _cparams(sem):
    return pltpu.CompilerParams(dimension_semantics=sem, vmem_limit_bytes=VMEM_LIMIT)


def _rms(x, gain):
    ms = jnp.mean(x * x, axis=-1, keepdims=True)
    return (x * lax.rsqrt(ms + EPS)) * gain


def _silu(x):
    return x * (0.5 * jnp.tanh(0.5 * x) + 0.5)


def _dot(a, b):
    return jnp.dot(a, b, preferred_element_type=F32)


def _dot_nt(a, b):
    return lax.dot_general(a, b, (((1,), (1,)), ((), ())), preferred_element_type=F32)


def _dot_tn(a, b):
    return lax.dot_general(a, b, (((0,), (0,)), ((), ())), preferred_element_type=F32)


def _colmax(s):
    parts = [s[i:i + 64] for i in range(0, s.shape[0], 64)] if s.shape[0] % 64 == 0 else [s]
    while len(parts) > 1:
        parts = [jnp.maximum(parts[i], parts[i + 1]) if i + 1 < len(parts) else parts[i]
                 for i in range(0, len(parts), 2)]
    return jnp.max(parts[0], axis=0, keepdims=True)


def _split3(x):
    hi = x.astype(BF16)
    r1 = x - hi.astype(F32)
    mid = r1.astype(BF16)
    lo = (r1 - mid.astype(F32)).astype(BF16)
    return hi, mid, lo


def _rms_matmul_kernel(x_ref, g_ref, w_ref, o_ref, u_scr):
    @pl.when(pl.program_id(1) == 0)
    def _():
        u_scr[...] = _rms(x_ref[...], g_ref[...]).astype(BF16)

    o_ref[...] = _dot(u_scr[...], w_ref[...]).astype(o_ref.dtype)


def _rms_matmul(x, gain, w, out_dtype, tn):
    n, d = x.shape
    wd = w.shape[1]
    tm = min(2 * TOK_TILE, n)
    return pl.pallas_call(
        _rms_matmul_kernel,
        grid=(n // tm, wd // tn),
        in_specs=[pl.BlockSpec((tm, d), lambda i, j: (i, 0)),
                  pl.BlockSpec((1, d), lambda i, j: (0, 0)),
                  pl.BlockSpec((d, tn), lambda i, j: (0, j))],
        out_specs=pl.BlockSpec((tm, tn), lambda i, j: (i, j)),
        out_shape=jax.ShapeDtypeStruct((n, wd), out_dtype),
        scratch_shapes=[pltpu.VMEM((tm, d), BF16)],
        compiler_params=_cparams(("parallel", "arbitrary")),
    )(x, gain, w)


def _hgrn_levels():
    return [HG_CHUNK >> (i + 1) for i in range(HG_CHUNK.bit_length() - 1)]


def _hgrn_intra_exact(qf, kk, b, gsel_ref):
    c_len = HG_CHUNK
    row = lax.broadcasted_iota(jnp.int32, (c_len, c_len), 0)
    col = lax.broadcasted_iota(jnp.int32, (c_len, c_len), 1)
    trow = lax.broadcasted_iota(jnp.int32, (c_len, HG_DK), 0)
    a = jnp.where(row == col, _dot_nt(qf.astype(BF16), kk.astype(BF16)), 0.0)
    b3 = jnp.concatenate(_split3(b), axis=0)
    for lvl, h in enumerate(_hgrn_levels()):
        b_at_ref = _dot(gsel_ref[lvl], b3)
        upper = (trow & (2 * h - 1)) >= h
        qe = jnp.where(upper, qf * jnp.exp2(jnp.minimum(b - b_at_ref, 0.0)), 0.0)
        ke = jnp.where(upper, 0.0, kk * jnp.exp2(jnp.minimum(b_at_ref - b, 0.0)))
        same_block = (row & -(2 * h)) == (col & -(2 * h))
        block = same_block & ((row & (2 * h - 1)) >= h) & ((col & (2 * h - 1)) < h)
        a = a + jnp.where(block, _dot_nt(qe.astype(BF16), ke.astype(BF16)), 0.0)
    return a


def _hgrn_kernel(q_ref, i_ref, g_ref, f_ref, lbl_ref, gn_ref, tri_ref, gsel_ref, o_ref, st_ref, st0_scr, *,
                 n_chunks):
    c_len = HG_CHUNK
    half = c_len // 2

    @pl.when(pl.program_id(2) == 0)
    def _():
        st_ref[...] = jnp.zeros_like(st_ref)

    lg = lbl_ref[...]
    mx = jnp.max(lg, axis=0, keepdims=True)
    ex = jnp.exp(lg - mx)
    lb = ex[0:1, :] / jnp.sum(ex, axis=0, keepdims=True)

    row = lax.broadcasted_iota(jnp.int32, (c_len, c_len), 0)
    col = lax.broadcasted_iota(jnp.int32, (c_len, c_len), 1)
    causal = col <= row
    tri3 = tri_ref[...]

    def run_group(first, state, exact):
        chunks = [slice((first + c) * c_len, (first + c + 1) * c_len) for c in range(HG_GROUP)]
        qf, kk, lf = [], [], []
        for rows in chunks:
            qx = q_ref[0, rows, :].astype(F32)
            qf.append(_silu(qx) * (HG_DK ** -0.5))
            forget = lb + (1.0 - lb) * jax.nn.sigmoid(f_ref[0, rows, :].astype(F32))
            kk.append(1.0 - forget)
            lf.append(jnp.log2(forget))
        bb = [_dot(tri3, jnp.concatenate(_split3(x), axis=0)) for x in lf]
        qe, ke, qb, kd, dlast = [], [], [], [], []
        need = jnp.zeros((1, HG_DK), F32)
        for c in range(HG_GROUP):
            b = bb[c]
            b_mid = b[half - 1:half, :]
            b_last = b[c_len - 1:c_len, :]
            if not exact:
                qe.append((qf[c] * jnp.exp2(b - b_mid)).astype(BF16))
                ke.append((kk[c] * jnp.exp2(b_mid - b)).astype(BF16))
                need = jnp.maximum(need, jnp.maximum(b[0:1, :] - b_mid, b_mid - b_last))
            qb.append((qf[c] * jnp.exp2(b)).astype(BF16))
            kd.append((kk[c] * jnp.exp2(b_last - b)).astype(BF16))
            dlast.append(jnp.exp2(b_last))
        vv = [i_ref[0, rows, :] for rows in chunks]
        if exact:
            aa = [_hgrn_intra_exact(qf[c], kk[c], bb[c], gsel_ref) for c in range(HG_GROUP)]
        else:
            aa = [_dot_nt(qe[c], ke[c]) for c in range(HG_GROUP)]
        kv = [_dot_tn(vv[c], kd[c]) for c in range(HG_GROUP)]
        o_intra = [_dot(jnp.where(causal, aa[c], 0.0).astype(BF16), vv[c]) for c in range(HG_GROUP)]
        states = [state]
        for c in range(HG_GROUP):
            states.append(states[c] * dlast[c] + kv[c])
        o_inter = [_dot_nt(qb[c], states[c].astype(BF16)) for c in range(HG_GROUP)]
        for c, rows in enumerate(chunks):
            gx = g_ref[0, rows, :].astype(F32)
            y = _rms(o_intra[c] + o_inter[c], gn_ref[...]) * _silu(gx)
            o_ref[0, rows, :] = y.astype(o_ref.dtype)
        return states[HG_GROUP], need

    def run_all(exact):
        state = st0_scr[...]
        need = jnp.zeros((1, HG_DK), F32)
        for first in range(0, n_chunks, HG_GROUP):
            state, need_g = run_group(first, state, exact)
            need = jnp.maximum(need, need_g)
        return state, need

    st0_scr[...] = st_ref[...]
    state, need = run_all(exact=False)
    st_ref[...] = state

    @pl.when(jnp.max(need) > HG_EXP2_LIMIT)
    def _():
        run_all(exact=True)


def _hgrn(proj, lb_logits, gnorm, bsz, seq):
    n_chunks = HG_TOK // HG_CHUNK
    tri = np.tril(np.ones((HG_CHUNK, HG_CHUNK), np.float32))
    tri3 = jnp.asarray(np.concatenate([tri, tri, tri], axis=1), BF16)
    t_idx = np.arange(HG_CHUNK)
    gsel = np.zeros((len(_hgrn_levels()), HG_CHUNK, HG_CHUNK), np.float32)
    for lvl, h in enumerate(_hgrn_levels()):
        gsel[lvl, t_idx, (t_idx // (2 * h)) * (2 * h) + h - 1] = 1.0
    gsel3 = jnp.asarray(np.concatenate([gsel, gsel, gsel], axis=2), BF16)
    hh = HG_HEADS
    blk = (1, HG_TOK, HG_DK)
    head_cols = lambda off: pl.BlockSpec(blk, lambda b, h, t: (b, t, off // HG_DK + h))
    return pl.pallas_call(
        functools.partial(_hgrn_kernel, n_chunks=n_chunks),
        grid=(bsz, hh, seq // HG_TOK),
        in_specs=[head_cols(P_HQ), head_cols(P_HI), head_cols(P_HG), head_cols(P_HF),
                  pl.BlockSpec((2, HG_DK), lambda b, h, t: (0, h)),
                  pl.BlockSpec((1, HG_DV), lambda b, h, t: (0, 0)),
                  pl.BlockSpec((HG_CHUNK, 3 * HG_CHUNK), lambda b, h, t: (0, 0)),
                  pl.BlockSpec(gsel3.shape, lambda b, h, t: (0, 0, 0))],
        out_specs=pl.BlockSpec(blk, lambda b, h, t: (b, t, h)),
        out_shape=jax.ShapeDtypeStruct((bsz, seq, HG_WIDTH), BF16),
        scratch_shapes=[pltpu.VMEM((HG_DV, HG_DK), F32), pltpu.VMEM((HG_DV, HG_DK), F32)],
        compiler_params=_cparams(("parallel", "parallel", "arbitrary")),
    )(proj, proj, proj, proj, lb_logits, gnorm, tri3, gsel3)


def _rope_kernel(x_ref, c_ref, s1_ref, s2_ref, o_ref, *, scale):
    width = x_ref.shape[-1]
    c = c_ref[...]
    s1 = s1_ref[...]
    s2 = s2_ref[...]
    for j in range(width // LANES):
        cols = slice(j * LANES, (j + 1) * LANES)
        x = x_ref[0, :, cols].astype(F32)
        if scale != 1.0:
            x = x * scale
        o_ref[0, :, cols] = _rope_lanes(x, c, s1, s2).astype(o_ref.dtype)


def _rope(src, col_block, width, tables, scale, bsz, seq):
    tt = min(TOK_TILE, seq)
    tab = pl.BlockSpec((tt, LANES), lambda b, t: (t, 0))
    return pl.pallas_call(
        functools.partial(_rope_kernel, scale=scale),
        grid=(bsz, seq // tt),
        in_specs=[pl.BlockSpec((1, tt, width), lambda b, t: (b, t, col_block)), tab, tab, tab],
        out_specs=pl.BlockSpec((1, tt, width), lambda b, t: (b, t, 0)),
        out_shape=jax.ShapeDtypeStruct((bsz, seq, width), BF16),
        compiler_params=_cparams(("parallel", "parallel")),
    )(src, *tables)


def _rope_lanes(x, c, s1, s2):
    return x * c + pltpu.roll(x, LANES - ROPE_DIM // 2, 1) * s1 + pltpu.roll(x, ROPE_DIM // 2, 1) * s2


def _kv_prep_kernel(k_ref, v_ref, c_ref, s1_ref, s2_ref, ko_ref, vo_ref, *, pad_blocks, with_block_id):
    tt = k_ref.shape[1]
    step = pl.program_id(1)
    is_pad = step < pad_blocks
    pos = (step - pad_blocks) * tt + lax.broadcasted_iota(jnp.int32, (tt, NSA_DH), 0)
    lane = lax.broadcasted_iota(jnp.int32, (tt, NSA_DH), 1)
    if with_block_id:
        extra = jnp.where(pos // SEL_BLOCK == lane, 1.0, 0.0)
    else:
        extra = jnp.where((lane == 0) & is_pad, 1.0, 0.0)
    c, s1, s2 = c_ref[...], s1_ref[...], s2_ref[...]
    vt = v_ref[0].astype(F32).T
    for j in range(NSA_KVW // LANES):
        y = _rope_lanes(k_ref[0, :, j * LANES:(j + 1) * LANES].astype(F32), c, s1, s2)
        y = jnp.where(is_pad, 0.0, y)
        for g2 in range(LANES // NSA_DH):
            g = j * (LANES // NSA_DH) + g2
            ko_ref[0, g] = jnp.concatenate([y[:, g2 * NSA_DH:(g2 + 1) * NSA_DH], extra], axis=1).astype(BF16)
    for g in range(NSA_KV):
        vo_ref[0, g, 0:NSA_DH, :] = jnp.where(is_pad, 0.0, vt[g * NSA_DH:(g + 1) * NSA_DH]).astype(BF16)
        vo_ref[0, g, NSA_DH:NSA_DH + SUBLANES_BF16, :] = jnp.ones((SUBLANES_BF16, tt), BF16)


def _kv_prep(proj, k_block, v_block, tables, pad_rows, with_block_id, bsz, seq):
    tt = min(TOK_TILE, seq)
    pad_blocks = pad_rows // tt
    assert pad_blocks * tt == pad_rows
    src_t = lambda t: jnp.maximum(t - pad_blocks, 0)
    tab = pl.BlockSpec((tt, LANES), lambda b, t: (src_t(t), 0))
    rows = seq + pad_rows
    return pl.pallas_call(
        functools.partial(_kv_prep_kernel, pad_blocks=pad_blocks, with_block_id=with_block_id),
        grid=(bsz, rows // tt),
        in_specs=[pl.BlockSpec((1, tt, NSA_KVW), lambda b, t: (b, src_t(t), k_block)),
                  pl.BlockSpec((1, tt, NSA_KVW), lambda b, t: (b, src_t(t), v_block)), tab, tab, tab],
        out_specs=[pl.BlockSpec((1, NSA_KV, tt, 2 * NSA_DH), lambda b, t: (b, 0, t, 0)),
                   pl.BlockSpec((1, NSA_KV, NSA_DH + SUBLANES_BF16, tt), lambda b, t: (b, 0, 0, t))],
        out_shape=[jax.ShapeDtypeStruct((bsz, NSA_KV, rows, 2 * NSA_DH), BF16),
                   jax.ShapeDtypeStruct((bsz, NSA_KV, NSA_DH + SUBLANES_BF16, rows), BF16)],
        compiler_params=_cparams(("parallel", "parallel")),
    )(proj, proj, *tables)


def _rope_tables(seq):
    half = ROPE_DIM // 2
    inv = jnp.asarray(ROPE_THETA ** (-np.arange(half) * 2.0 / ROPE_DIM), F32)
    ang = jnp.arange(seq, dtype=F32)[:, None] * inv[None, :]
    cos, sin = jnp.cos(ang), jnp.sin(ang)
    lane = np.arange(LANES) % NSA_DH
    pick = lane % half
    cos_l = cos[:, pick]
    sin_l = sin[:, pick]
    first = jnp.asarray(lane < half)
    second = jnp.asarray((lane >= half) & (lane < ROPE_DIM))
    c = jnp.where(first | second, cos_l, 1.0)
    s1 = jnp.where(first, -sin_l, 0.0)
    s2 = jnp.where(second, sin_l, 0.0)
    return c, s1, s2


def _gelu_tanh(x):
    return 0.5 * x * (1.0 + jnp.tanh(np.sqrt(2.0 / np.pi).astype(np.float32) * (x + 0.044715 * (x * x * x))))


def _compress_kernel(xk_ref, xv_ref, pek_ref, pev_ref, wk1_ref, wk2_ref, wv1_ref, wv2_ref, ok_ref, ov_ref):
    half_w = CMP_STRIDE * NSA_DH

    def one(x_ref, pe_ref, w1_ref, w2_ref):
        x = x_ref[0, 0]
        n_slab = x.shape[0]
        p0 = _dot(x, w1_ref[0:half_w, :])
        p1 = _dot(x, w1_ref[half_w:2 * half_w, :])
        pe = jnp.broadcast_to(pe_ref[...], (8, 2 * half_w)).astype(BF16)
        pw = _dot(pe, w1_ref[...])[0:1, :]
        h = p0 + pltpu.roll(p1, n_slab - 1, 0) + pw
        return _dot(_gelu_tanh(h).astype(BF16), w2_ref[...])

    ok_ref[0, 0] = one(xk_ref, pek_ref, wk1_ref, wk2_ref).astype(ok_ref.dtype)
    ov_ref[0, 0] = one(xv_ref, pev_ref, wv1_ref, wv2_ref).astype(ov_ref.dtype)


def _compress(xk, xv, pek, pev, wk1, wk2, wv1, wv2):
    bsz, grp, n_slab, wdt = xk.shape
    xs = pl.BlockSpec((1, 1, n_slab, wdt), lambda b, g: (b, g, 0, 0))
    full = lambda a: pl.BlockSpec(a.shape, lambda b, g: (0,) * a.ndim)
    osd = jax.ShapeDtypeStruct((bsz, grp, n_slab, NSA_DH), BF16)
    os_ = pl.BlockSpec((1, 1, n_slab, NSA_DH), lambda b, g: (b, g, 0, 0))
    return pl.pallas_call(
        _compress_kernel,
        grid=(bsz, grp),
        in_specs=[xs, xs, full(pek), full(pev), full(wk1), full(wk2), full(wv1), full(wv2)],
        out_specs=[os_, os_],
        out_shape=[osd, osd],
        compiler_params=_cparams(("parallel", "parallel")),
    )(xk, xv, pek, pev, wk1, wk2, wv1, wv2)


def _attn_kernel(nq_ref, qr_ref, kcmp_ref, vcmpt_ref, ks_ref, vst_ref, kw_ref, vwt_ref, ng_ref, ovt_ref,
                 o_ref, gt_scr, qaug_scr, *, seq):
    qn = ATT_Q
    grp = NSA_GROUP
    ln = grp * qn
    n_sel = seq // SEL_BLOCK
    top = min(SEL_TOPK, n_sel)
    g_id = pl.program_id(1)
    q0 = pl.multiple_of(pl.program_id(2) * qn, qn)

    def stack_heads(x):
        return jnp.concatenate([x[:, r * NSA_DH:(r + 1) * NSA_DH] for r in range(grp)], axis=0)

    def add_per_head(s, bias):
        return jnp.concatenate([s[:, r * qn:(r + 1) * qn] + bias for r in range(grp)], axis=1)

    qs = stack_heads(nq_ref[0])
    qrs = stack_heads(qr_ref[0])
    t_lane = q0 + (lax.broadcasted_iota(jnp.int32, (1, ln), 1) & (qn - 1))
    t_q = q0 + lax.broadcasted_iota(jnp.int32, (1, qn), 1)

    n_slab = kcmp_ref.shape[2]
    wlen = WINDOW + qn
    pad_col = jnp.where(lax.broadcasted_iota(jnp.int32, (ln, NSA_DH), 1) == 0, NEG, 0.0).astype(BF16)
    sc = _dot_nt(kcmp_ref[0, 0], qs) * (ATT_SCALE * LOG2E)
    sw = _dot_nt(kw_ref[0, 0, pl.ds(q0, wlen), :], jnp.concatenate([qrs, pad_col], axis=1))
    sd = _dot_nt(ks_ref[0, 0, pl.ds(q0, qn), :][:, 0:NSA_DH], qrs)

    cend = lax.broadcasted_iota(jnp.int32, (n_slab, 1), 0) * CMP_STRIDE + (CMP_LEN - 1)
    sm = add_per_head(sc, jnp.where(cend <= t_q, 0.0, NEG))
    e = jnp.exp2(sm - _colmax(sm))
    den = jnp.sum(e, axis=0, keepdims=True)
    p = e * jnp.where(t_lane >= CMP_LEN - 1, 1.0 / den, 0.0)
    oc = _dot(vcmpt_ref[0, 0], p.astype(BF16))

    psum = p[:, 0:qn]
    for r in range(1, grp):
        psum = psum + p[:, r * qn:(r + 1) * qn]
    hi, mid, lo = _split3(psum)
    ovt = ovt_ref[...]
    imp = _dot(ovt, hi) + _dot(ovt, mid) + _dot(ovt, lo)

    i_loc = lax.broadcasted_iota(jnp.int32, (qn, 1), 0)
    j_loc = lax.broadcasted_iota(jnp.int32, (1, qn), 1)
    sw = jnp.concatenate([add_per_head(sw[0:qn], jnp.where(i_loc > j_loc, 0.0, NEG)),
                          sw[qn:WINDOW],
                          add_per_head(sw[WINDOW:wlen], jnp.where(i_loc <= j_loc, 0.0, NEG))], axis=0)
    pw = jnp.exp2(sw - _colmax(sw))
    rw = _dot(vwt_ref[0, 0, :, pl.ds(q0, wlen)], pw.astype(BF16))
    ow = rw[0:NSA_DH] / rw[NSA_DH:NSA_DH + 1]

    kpos = q0 + lax.broadcasted_iota(jnp.int32, (qn, 1), 0)
    sd = add_per_head(sd, jnp.where(kpos <= t_q, 0.0, NEG))
    m0 = _colmax(sd)
    acc0 = _dot(vst_ref[0, 0, :, pl.ds(q0, qn)], jnp.exp2(sd - m0).astype(BF16))

    gt_scr[...] = jax.nn.sigmoid(ng_ref[0].astype(F32)).T

    def gate_row(branch):
        return jnp.concatenate(
            [gt_scr[pl.ds(branch * NSA_HEADS + g_id * grp + r, 1), :] for r in range(grp)], axis=1)

    out_cw = gate_row(0) * oc + gate_row(2) * ow
    gate_sel = gate_row(1)

    jidx = lax.broadcasted_iota(jnp.int32, (n_sel, qn), 0)
    tq = q0 + lax.broadcasted_iota(jnp.int32, (n_sel, qn), 1)
    cur = tq // SEL_BLOCK
    forced = (jidx == 0) | (jidx == cur) | (jidx == cur - 1)
    causal_blk = jidx * SEL_BLOCK <= tq
    val = jnp.where(forced, -2.0, imp)
    val = jnp.where(causal_blk, val, -1.0)
    rows8 = 8
    groups = [val[v * rows8:(v + 1) * rows8] for v in range(n_sel // rows8)]
    counts = [jnp.zeros((rows8, qn), F32) for _ in groups]
    jrow = lax.broadcasted_iota(jnp.int32, (rows8, qn), 0)
    for jp in range(n_sel):
        other = jnp.broadcast_to(val[jp:jp + 1, :], (rows8, qn))
        for v, grp_val in enumerate(groups):
            ge = jnp.where(other >= grp_val, 1.0, 0.0)
            gt = jnp.where(other > grp_val, 1.0, 0.0)
            if v * rows8 > jp:
                before = ge
            elif (v + 1) * rows8 <= jp:
                before = gt
            else:
                before = jnp.where(jrow > jp - v * rows8, ge, gt)
            counts[v] = counts[v] + before
    rank = jnp.concatenate(counts, axis=0)
    sel = jnp.where(forced | (rank < float(max(top - 3, 0))), 1.0, 0.0)
    diag_blk = q0 // SEL_BLOCK
    in_diag = (jidx >= diag_blk) & (jidx < diag_blk + qn // SEL_BLOCK)
    selb = jnp.where((sel > 0.0) & causal_blk & jnp.logical_not(in_diag), 0.0, NEG)
    selb_t = jnp.concatenate([selb, jnp.zeros((LANES - n_sel, qn), F32)], axis=0).T
    selb_t = selb_t[:, 0:NSA_DH].astype(BF16)
    qaug_scr[...] = jnp.concatenate([qrs, jnp.concatenate([selb_t] * grp, axis=0)], axis=1)

    kt_len = ATT_KT
    sub = kt_len // ATT_SUBTILES

    def sweep_block(k_base, n_sub, carry):
        m_run, acc = carry
        k0s = [pl.multiple_of(k_base + j * sub, sub) for j in range(n_sub)]
        score = lambda j: _dot_nt(ks_ref[0, 0, pl.ds(k0s[j], sub), :], qaug_scr[...])
        ahead = min(ATT_LOOKAHEAD, n_sub)
        scores = [score(j) for j in range(ahead)]
        for j in range(n_sub):
            s = scores[j]
            m_new = jnp.maximum(m_run, _colmax(s))
            alpha = jnp.exp2(m_run - m_new)
            pt = jnp.exp2(s - m_new).astype(BF16)
            acc = alpha * acc + _dot(vst_ref[0, 0, :, pl.ds(k0s[j], sub)], pt)
            m_run = m_new
            if j + ahead < n_sub:
                scores.append(score(j + ahead))
        return m_run, acc

    n_whole = q0 // kt_len
    carry = lax.fori_loop(0, n_whole, lambda kt, cy: sweep_block(kt * kt_len, ATT_SUBTILES, cy), (m0, acc0))
    tails = [functools.partial(sweep_block, n_whole * kt_len, r) for r in range(ATT_SUBTILES)]
    _, acc = lax.switch((q0 - n_whole * kt_len) // sub, tails, carry)
    osel = acc[0:NSA_DH] / acc[NSA_DH:NSA_DH + 1]

    out_t = out_cw + gate_sel * osel
    pieces = []
    for r2 in range(grp // 2):
        pair = jnp.concatenate([out_t[:, (2 * r2) * qn:(2 * r2 + 1) * qn],
                                out_t[:, (2 * r2 + 1) * qn:(2 * r2 + 2) * qn]], axis=0)
        pieces.append(pair.T)
    o_ref[0] = jnp.concatenate(pieces, axis=1).astype(o_ref.dtype)


def _attention(proj, q_rot, k_cmp, v_cmp_t, ks, vs_t, kw, vw_t, ov_t, bsz, seq):
    qn = ATT_Q
    gw = NSA_GROUP * NSA_DH
    n_slab = k_cmp.shape[2]
    n_sel = seq // SEL_BLOCK
    qspec = pl.BlockSpec((1, qn, gw), lambda b, g, i: (b, i, g))
    per_group = lambda a: pl.BlockSpec((1, 1) + a.shape[2:], lambda b, g, i: (b, g, 0, 0))
    return pl.pallas_call(
        functools.partial(_attn_kernel, seq=seq),
        grid=(bsz, NSA_KV, seq // qn),
        in_specs=[pl.BlockSpec((1, qn, gw), lambda b, g, i: (b, i, P_NQ // gw + g)), qspec,
                  per_group(k_cmp), per_group(v_cmp_t), per_group(ks), per_group(vs_t),
                  per_group(kw), per_group(vw_t),
                  pl.BlockSpec((1, qn, LANES), lambda b, g, i: (b, i, P_NG // LANES)),
                  pl.BlockSpec((n_sel, n_slab), lambda b, g, i: (0, 0))],
        out_specs=qspec,
        out_shape=jax.ShapeDtypeStruct((bsz, seq, NSA_WIDTH), BF16),
        scratch_shapes=[pltpu.VMEM((LANES, qn), F32), pltpu.VMEM((NSA_GROUP * qn, 2 * NSA_DH), BF16)],
        compiler_params=_cparams(("parallel", "parallel", "arbitrary")),
    )(proj, q_rot, k_cmp, v_cmp_t, ks, vs_t, kw, vw_t, proj, ov_t)


def _merge_kernel(x_ref, ya_ref, yb_ref, ga_ref, gb_ref, wa_ref, wb_ref, wo_ref, gn_ref, o_ref):
    ga = jax.nn.sigmoid(ga_ref[...].astype(F32))
    gb = jax.nn.sigmoid(gb_ref[...].astype(F32))
    merged = ga * _dot(ya_ref[...], wa_ref[...]) + gb * _dot(yb_ref[...], wb_ref[...])
    mo = _dot(merged.astype(BF16), wo_ref[...])
    o_ref[...] = x_ref[...] + _rms(mo, gn_ref[...])


def _merge(x, ya, yb, gates, wa, wb, wo, gn):
    n, d = x.shape
    tm = min(TOK_TILE, n)
    row = lambda c: pl.BlockSpec((tm, d), lambda i: (i, c))
    wsp = pl.BlockSpec((d, d), lambda i: (0, 0))
    return pl.pallas_call(
        _merge_kernel,
        grid=(n // tm,),
        in_specs=[row(0), row(0), row(0), row(P_GA // d), row(P_GB // d), wsp, wsp, wsp,
                  pl.BlockSpec((1, d), lambda i: (0, 0))],
        out_specs=row(0),
        out_shape=jax.ShapeDtypeStruct((n, d), F32),
        compiler_params=_cparams(("parallel",)),
    )(x, ya, yb, gates, gates, wa, wb, wo, gn)


def _mlp_kernel(h_ref, gpre_ref, wu_ref, wd_ref, gpost_ref, o_ref, v_scr, acc_scr):
    j = pl.program_id(1)

    @pl.when(j == 0)
    def _():
        v_scr[...] = _rms(h_ref[...], gpre_ref[...]).astype(BF16)
        acc_scr[...] = jnp.zeros_like(acc_scr)

    up = _dot(v_scr[...], wu_ref[...])
    act = jnp.square(jnp.maximum(up, 0.0))
    acc_scr[...] += _dot(act.astype(BF16), wd_ref[...])

    @pl.when(j == pl.num_programs(1) - 1)
    def _():
        o_ref[...] = h_ref[...] + _rms(acc_scr[...], gpost_ref[...])


def _mlp(h, gpre, wu, wd, gpost):
    n, d = h.shape
    dff = wu.shape[1]
    tm = min(2 * TOK_TILE, n)
    tf = 1024
    vec = pl.BlockSpec((1, d), lambda i, j: (0, 0))
    return pl.pallas_call(
        _mlp_kernel,
        grid=(n // tm, dff // tf),
        in_specs=[pl.BlockSpec((tm, d), lambda i, j: (i, 0)), vec,
                  pl.BlockSpec((d, tf), lambda i, j: (0, j)),
                  pl.BlockSpec((tf, d), lambda i, j: (j, 0)), vec],
        out_specs=pl.BlockSpec((tm, d), lambda i, j: (i, 0)),
        out_shape=jax.ShapeDtypeStruct((n, d), F32),
        scratch_shapes=[pltpu.VMEM((tm, d), BF16), pltpu.VMEM((tm, d), F32)],
        compiler_params=_cparams(("parallel", "arbitrary")),
    )(h, gpre, wu, wd, gpost)


def _ple_kernel(h_ref, p_ref, wp_ref, wg_ref, gn_ref, o_ref):
    h = h_ref[...]
    e = _dot(p_ref[...].astype(BF16), wp_ref[...]) * jax.nn.sigmoid(_dot(h.astype(BF16), wg_ref[...]))
    o_ref[...] = h + _rms(e, gn_ref[...])


def _ple(h, p, wp, wg, gn):
    n, d = h.shape
    pd = p.shape[1]
    tm = min(TOK_TILE, n)
    return pl.pallas_call(
        _ple_kernel,
        grid=(n // tm,),
        in_specs=[pl.BlockSpec((tm, d), lambda i: (i, 0)),
                  pl.BlockSpec((tm, pd), lambda i: (i, 0)),
                  pl.BlockSpec((pd, d), lambda i: (0, 0)),
                  pl.BlockSpec((d, d), lambda i: (0, 0)),
                  pl.BlockSpec((1, d), lambda i: (0, 0))],
        out_specs=pl.BlockSpec((tm, d), lambda i: (i, 0)),
        out_shape=jax.ShapeDtypeStruct((n, d), F32),
        compiler_params=_cparams(("parallel",)),
    )(h, p, wp, wg, gn)


def _overlap_t(seq):
    n_slab = seq // CMP_STRIDE
    n_cmp = (seq - CMP_LEN) // CMP_STRIDE + 1
    n_sel = seq // SEL_BLOCK
    start = np.arange(n_slab) * CMP_STRIDE
    end = start + CMP_LEN - 1
    sel_start = np.arange(n_sel) * SEL_BLOCK
    ov = (start[None, :] < sel_start[:, None] + SEL_BLOCK) & (end[None, :] >= sel_start[:, None])
    ov = ov & (np.arange(n_slab)[None, :] < n_cmp)
    return jnp.asarray(ov.astype(np.float32), BF16)


def kernel(x, p, w_in, w_branch_a, w_branch_b, w_out, norm_pre_mix, norm_post_mix, norm_pre_mlp,
           norm_post_mlp, hg_lb_logits, hg_gnorm, cmp_pe_k, cmp_pe_v, cmp_wk1, cmp_wk2, cmp_wv1, cmp_wv2,
           w_up, w_down, w_ple, w_ple_gate, norm_ple):
    bsz, seq, d = x.shape
    depth = w_in.shape[0]
    assert depth == 1 and hg_lb_logits.shape[0] == 2 and d == D_MODEL
    assert seq % ATT_KT == 0 and seq >= WINDOW + ATT_Q and seq // SEL_BLOCK <= NSA_DH
    assert seq % HG_TOK == 0 and (HG_TOK // HG_CHUNK) % HG_GROUP == 0
    assert ATT_Q in (SEL_BLOCK, 2 * SEL_BLOCK)
    assert ATT_Q % (ATT_KT // ATT_SUBTILES) == 0
    n = bsz * seq
    x2 = x.reshape(n, d)
    row = lambda a: a.reshape(1, -1)

    w = w_in[0]
    s_hq, s_hf, s_hi, s_hg = 0, HG_WIDTH, 2 * HG_WIDTH, 3 * HG_WIDTH
    s_nq = 4 * HG_WIDTH
    s_ng = s_nq + NSA_WIDTH + 6 * NSA_KVW
    s_ga = s_ng + 3 * NSA_HEADS
    col = lambda a, b: w[:, a:b]
    w_all = jnp.concatenate(
        [col(s_hq, s_hf), col(s_hi, s_hg), col(s_hg, s_nq), col(s_hf, s_hi), col(s_nq, s_ga),
         jnp.zeros((d, P_GA - P_NG - 3 * NSA_HEADS), F32), col(s_ga, s_ga + 2 * D_MODEL)], axis=1).astype(BF16)
    assert w_all.shape[1] == P_WIDTH
    proj2 = _rms_matmul(x2, row(norm_pre_mix[0]), w_all, BF16, 1024)
    proj = proj2.reshape(bsz, seq, P_WIDTH)

    y_a = _hgrn(proj, hg_lb_logits, row(hg_gnorm[0]), bsz, seq)

    tables = _rope_tables(seq)
    kvb = P_KV // NSA_KVW
    q_rot = _rope(proj, P_NQ // NSA_WIDTH, NSA_WIDTH, tables, ATT_SCALE * LOG2E, bsz, seq)
    ks_aug, vs_t = _kv_prep(proj, kvb + 2, kvb + 3, tables, 0, True, bsz, seq)
    kw_aug, vw_t = _kv_prep(proj, kvb + 4, kvb + 5, tables, WINDOW, False, bsz, seq)
    kv_slice = lambda i: proj[:, :, P_KV + i * NSA_KVW: P_KV + (i + 1) * NSA_KVW]
    heads = lambda t: t.reshape(bsz, seq, NSA_KV, NSA_DH).transpose(0, 2, 1, 3)
    slabs = lambda t: heads(t).reshape(bsz, NSA_KV, seq // CMP_STRIDE, CMP_STRIDE * NSA_DH)
    k_cmp, v_cmp = _compress(
        slabs(kv_slice(0)), slabs(kv_slice(1)), cmp_pe_k[0].reshape(1, -1), cmp_pe_v[0].reshape(1, -1),
        cmp_wk1[0].astype(BF16), cmp_wk2[0].astype(BF16), cmp_wv1[0].astype(BF16), cmp_wv2[0].astype(BF16))
    y_b = _attention(proj, q_rot, k_cmp, v_cmp.transpose(0, 1, 3, 2), ks_aug, vs_t, kw_aug, vw_t,
                     _overlap_t(seq), bsz, seq)

    h1 = _merge(x2, y_a.reshape(n, -1), y_b.reshape(n, -1), proj2, w_branch_a[0].astype(BF16),
                w_branch_b[0].astype(BF16), w_out[0].astype(BF16), row(norm_post_mix[0]))
    h2 = _mlp(h1, row(norm_pre_mlp[0]), w_up[0].astype(BF16), w_down[0].astype(BF16), row(norm_post_mlp[0]))
    h3o = _ple(h2, p[0].reshape(n, -1), w_ple[0].astype(BF16), w_ple_gate[0].astype(BF16), row(norm_ple[0]))
    return h3o.reshape(bsz, seq, d)
```

```python
import functools

import numpy as np
import jax
import jax.numpy as jnp
from jax import lax
from jax.experimental import pallas as pl
from jax.experimental.pallas import tpu as pltpu

F32 = jnp.float32
BF16 = jnp.bfloat16

D_MODEL = 1024
HG_HEADS = 8
HG_DK = 128
HG_DV = 128
HG_WIDTH = HG_HEADS * HG_DK
HG_CHUNK = 64
NSA_HEADS = 16
NSA_KV = 4
NSA_GROUP = NSA_HEADS // NSA_KV
NSA_DH = 64
NSA_WIDTH = NSA_HEADS * NSA_DH
NSA_KVW = NSA_KV * NSA_DH
CMP_LEN = 32
CMP_STRIDE = 16
CMP_HIDDEN = 256
SEL_BLOCK = 64
SEL_TOPK = 16
WINDOW = 512
ROPE_THETA = 500000.0
ROPE_DIM = NSA_DH // 4
D_FF = 4 * D_MODEL
EPS = 1e-6
NEG = -1e30
ATT_SCALE = NSA_DH ** -0.5
LOG2E = float(np.log2(np.e))

LANES = 128
SUBLANES_BF16 = 16
VMEM_LIMIT = 56 * 1024 * 1024

TOK_TILE = 512
ATT_Q = 128
ATT_GROUPS = 2
ATT_KT = 1024
ATT_SUBTILES = 8
ATT_LOOKAHEAD = 7
HG_TOK = 1024
HG_GROUP = 16

P_HQ, P_HI, P_HG, P_HF = 0, HG_WIDTH, 2 * HG_WIDTH, 3 * HG_WIDTH
P_NQ = 4 * HG_WIDTH
P_KV = P_NQ + NSA_WIDTH
P_NG = P_KV + 6 * NSA_KVW
P_GA = 7 * D_MODEL
P_GB = 8 * D_MODEL
P_WIDTH = 9 * D_MODEL
HG_EXP2_LIMIT = 115.0


def _cparams(sem):
    return pltpu.CompilerParams(dimension_semantics=sem, vmem_limit_bytes=VMEM_LIMIT)


def _rms(x, gain):
    ms = jnp.mean(x * x, axis=-1, keepdims=True)
    return (x * lax.rsqrt(ms + EPS)) * gain


def _silu(x):
    return x * (0.5 * jnp.tanh(0.5 * x) + 0.5)


def _dot(a, b):
    return jnp.dot(a, b, preferred_element_type=F32)


def _dot_nt(a, b):
    return lax.dot_general(a, b, (((1,), (1,)), ((), ())), preferred_element_type=F32)


def _dot_tn(a, b):
    return lax.dot_general(a, b, (((0,), (0,)), ((), ())), preferred_element_type=F32)


def _colmax(s):
    parts = [s[i:i + 64] for i in range(0, s.shape[0], 64)] if s.shape[0] % 64 == 0 else [s]
    while len(parts) > 1:
        parts = [jnp.maximum(parts[i], parts[i + 1]) if i + 1 < len(parts) else parts[i]
                 for i in range(0, len(parts), 2)]
    return jnp.max(parts[0], axis=0, keepdims=True)


def _split3(x):
    hi = x.astype(BF16)
    r1 = x - hi.astype(F32)
    mid = r1.astype(BF16)
    lo = (r1 - mid.astype(F32)).astype(BF16)
    return hi, mid, lo


def _rms_matmul_kernel(x_ref, g_ref, w_ref, o_ref, u_scr):
    @pl.when(pl.program_id(1) == 0)
    def _():
        u_scr[...] = _rms(x_ref[...], g_ref[...]).astype(BF16)

    o_ref[...] = _dot(u_scr[...], w_ref[...]).astype(o_ref.dtype)


def _rms_matmul(x, gain, w, out_dtype, tn):
    n, d = x.shape
    wd = w.shape[1]
    tm = min(2 * TOK_TILE, n)
    return pl.pallas_call(
        _rms_matmul_kernel,
        grid=(n // tm, wd // tn),
        in_specs=[pl.BlockSpec((tm, d), lambda i, j: (i, 0)),
                  pl.BlockSpec((1, d), lambda i, j: (0, 0)),
                  pl.BlockSpec((d, tn), lambda i, j: (0, j))],
        out_specs=pl.BlockSpec((tm, tn), lambda i, j: (i, j)),
        out_shape=jax.ShapeDtypeStruct((n, wd), out_dtype),
        scratch_shapes=[pltpu.VMEM((tm, d), BF16)],
        compiler_params=_cparams(("parallel", "arbitrary")),
    )(x, gain, w)


def _hgrn_levels():
    return [HG_CHUNK >> (i + 1) for i in range(HG_CHUNK.bit_length() - 1)]


def _hgrn_intra_exact(qf, kk, b, gsel_ref):
    c_len = HG_CHUNK
    row = lax.broadcasted_iota(jnp.int32, (c_len, c_len), 0)
    col = lax.broadcasted_iota(jnp.int32, (c_len, c_len), 1)
    trow = lax.broadcasted_iota(jnp.int32, (c_len, HG_DK), 0)
    a = jnp.where(row == col, _dot_nt(qf.astype(BF16), kk.astype(BF16)), 0.0)
    b3 = jnp.concatenate(_split3(b), axis=0)
    for lvl, h in enumerate(_hgrn_levels()):
        b_at_ref = _dot(gsel_ref[lvl], b3)
        upper = (trow & (2 * h - 1)) >= h
        qe = jnp.where(upper, qf * jnp.exp2(jnp.minimum(b - b_at_ref, 0.0)), 0.0)
        ke = jnp.where(upper, 0.0, kk * jnp.exp2(jnp.minimum(b_at_ref - b, 0.0)))
        same_block = (row & -(2 * h)) == (col & -(2 * h))
        block = same_block & ((row & (2 * h - 1)) >= h) & ((col & (2 * h - 1)) < h)
        a = a + jnp.where(block, _dot_nt(qe.astype(BF16), ke.astype(BF16)), 0.0)
    return a


def _hgrn_kernel(q_ref, i_ref, g_ref, f_ref, lbl_ref, gn_ref, tri_ref, gsel_ref, o_ref, st_ref, st0_scr, *,
                 n_chunks):
    c_len = HG_CHUNK
    half = c_len // 2

    @pl.when(pl.program_id(2) == 0)
    def _():
        st_ref[...] = jnp.zeros_like(st_ref)

    lg = lbl_ref[...]
    mx = jnp.max(lg, axis=0, keepdims=True)
    ex = jnp.exp(lg - mx)
    lb = ex[0:1, :] / jnp.sum(ex, axis=0, keepdims=True)

    row = lax.broadcasted_iota(jnp.int32, (c_len, c_len), 0)
    col = lax.broadcasted_iota(jnp.int32, (c_len, c_len), 1)
    causal = col <= row
    tri3 = tri_ref[...]

    def run_group(first, state, exact):
        chunks = [slice((first + c) * c_len, (first + c + 1) * c_len) for c in range(HG_GROUP)]
        qf, kk, lf = [], [], []
        for rows in chunks:
            qx = q_ref[0, rows, :].astype(F32)
            qf.append(_silu(qx) * (HG_DK ** -0.5))
            forget = lb + (1.0 - lb) * jax.nn.sigmoid(f_ref[0, rows, :].astype(F32))
            kk.append(1.0 - forget)
            lf.append(jnp.log2(forget))
        bb = [_dot(tri3, jnp.concatenate(_split3(x), axis=0)) for x in lf]
        qe, ke, qb, kd, dlast = [], [], [], [], []
        need = jnp.zeros((1, HG_DK), F32)
        for c in range(HG_GROUP):
            b = bb[c]
            b_mid = b[half - 1:half, :]
            b_last = b[c_len - 1:c_len, :]
            if not exact:
                qe.append((qf[c] * jnp.exp2(b - b_mid)).astype(BF16))
                ke.append((kk[c] * jnp.exp2(b_mid - b)).astype(BF16))
                need = jnp.maximum(need, jnp.maximum(b[0:1, :] - b_mid, b_mid - b_last))
            qb.append((qf[c] * jnp.exp2(b)).astype(BF16))
            kd.append((kk[c] * jnp.exp2(b_last - b)).astype(BF16))
            dlast.append(jnp.exp2(b_last))
        vv = [i_ref[0, rows, :] for rows in chunks]
        if exact:
            aa = [_hgrn_intra_exact(qf[c], kk[c], bb[c], gsel_ref) for c in range(HG_GROUP)]
        else:
            aa = [_dot_nt(qe[c], ke[c]) for c in range(HG_GROUP)]
        kv = [_dot_tn(vv[c], kd[c]) for c in range(HG_GROUP)]
        o_intra = [_dot(jnp.where(causal, aa[c], 0.0).astype(BF16), vv[c]) for c in range(HG_GROUP)]
        states = [state]
        for c in range(HG_GROUP):
            states.append(states[c] * dlast[c] + kv[c])
        o_inter = [_dot_nt(qb[c], states[c].astype(BF16)) for c in range(HG_GROUP)]
        for c, rows in enumerate(chunks):
            gx = g_ref[0, rows, :].astype(F32)
            y = _rms(o_intra[c] + o_inter[c], gn_ref[...]) * _silu(gx)
            o_ref[0, rows, :] = y.astype(o_ref.dtype)
        return states[HG_GROUP], need

    def run_all(exact):
        state = st0_scr[...]
        need = jnp.zeros((1, HG_DK), F32)
        for first in range(0, n_chunks, HG_GROUP):
            state, need_g = run_group(first, state, exact)
            need = jnp.maximum(need, need_g)
        return state, need

    st0_scr[...] = st_ref[...]
    state, need = run_all(exact=False)
    st_ref[...] = state

    @pl.when(jnp.max(need) > HG_EXP2_LIMIT)
    def _():
        run_all(exact=True)


def _hgrn(proj, lb_logits, gnorm, bsz, seq):
    n_chunks = HG_TOK // HG_CHUNK
    tri = np.tril(np.ones((HG_CHUNK, HG_CHUNK), np.float32))
    tri3 = jnp.asarray(np.concatenate([tri, tri, tri], axis=1), BF16)
    t_idx = np.arange(HG_CHUNK)
    gsel = np.zeros((len(_hgrn_levels()), HG_CHUNK, HG_CHUNK), np.float32)
    for lvl, h in enumerate(_hgrn_levels()):
        gsel[lvl, t_idx, (t_idx // (2 * h)) * (2 * h) + h - 1] = 1.0
    gsel3 = jnp.asarray(np.concatenate([gsel, gsel, gsel], axis=2), BF16)
    hh = HG_HEADS
    blk = (1, HG_TOK, HG_DK)
    head_cols = lambda off: pl.BlockSpec(blk, lambda b, h, t: (b, t, off // HG_DK + h))
    return pl.pallas_call(
        functools.partial(_hgrn_kernel, n_chunks=n_chunks),
        grid=(bsz, hh, seq // HG_TOK),
        in_specs=[head_cols(P_HQ), head_cols(P_HI), head_cols(P_HG), head_cols(P_HF),
                  pl.BlockSpec((2, HG_DK), lambda b, h, t: (0, h)),
                  pl.BlockSpec((1, HG_DV), lambda b, h, t: (0, 0)),
                  pl.BlockSpec((HG_CHUNK, 3 * HG_CHUNK), lambda b, h, t: (0, 0)),
                  pl.BlockSpec(gsel3.shape, lambda b, h, t: (0, 0, 0))],
        out_specs=pl.BlockSpec(blk, lambda b, h, t: (b, t, h)),
        out_shape=jax.ShapeDtypeStruct((bsz, seq, HG_WIDTH), BF16),
        scratch_shapes=[pltpu.VMEM((HG_DV, HG_DK), F32), pltpu.VMEM((HG_DV, HG_DK), F32)],
        compiler_params=_cparams(("parallel", "parallel", "arbitrary")),
    )(proj, proj, proj, proj, lb_logits, gnorm, tri3, gsel3)


def _rope_kernel(x_ref, c_ref, s1_ref, s2_ref, o_ref, *, scale):
    width = x_ref.shape[-1]
    c = c_ref[...]
    s1 = s1_ref[...]
    s2 = s2_ref[...]
    for j in range(width // LANES):
        cols = slice(j * LANES, (j + 1) * LANES)
        x = x_ref[0, :, cols].astype(F32)
        if scale != 1.0:
            x = x * scale
        o_ref[0, :, cols] = _rope_lanes(x, c, s1, s2).astype(o_ref.dtype)


def _rope(src, col_block, width, tables, scale, bsz, seq):
    tt = min(TOK_TILE, seq)
    tab = pl.BlockSpec((tt, LANES), lambda b, t: (t, 0))
    return pl.pallas_call(
        functools.partial(_rope_kernel, scale=scale),
        grid=(bsz, seq // tt),
        in_specs=[pl.BlockSpec((1, tt, width), lambda b, t: (b, t, col_block)), tab, tab, tab],
        out_specs=pl.BlockSpec((1, tt, width), lambda b, t: (b, t, 0)),
        out_shape=jax.ShapeDtypeStruct((bsz, seq, width), BF16),
        compiler_params=_cparams(("parallel", "parallel")),
    )(src, *tables)


def _rope_lanes(x, c, s1, s2):
    return x * c + pltpu.roll(x, LANES - ROPE_DIM // 2, 1) * s1 + pltpu.roll(x, ROPE_DIM // 2, 1) * s2


def _kv_prep_kernel(k_ref, v_ref, c_ref, s1_ref, s2_ref, ko_ref, vo_ref, *, pad_blocks, with_block_id):
    tt = k_ref.shape[1]
    step = pl.program_id(1)
    is_pad = step < pad_blocks
    pos = (step - pad_blocks) * tt + lax.broadcasted_iota(jnp.int32, (tt, NSA_DH), 0)
    lane = lax.broadcasted_iota(jnp.int32, (tt, NSA_DH), 1)
    if with_block_id:
        extra = jnp.where(pos // SEL_BLOCK == lane, 1.0, 0.0)
    else:
        extra = jnp.where((lane == 0) & is_pad, 1.0, 0.0)
    c, s1, s2 = c_ref[...], s1_ref[...], s2_ref[...]
    vt = v_ref[0].astype(F32).T
    for j in range(NSA_KVW // LANES):
        y = _rope_lanes(k_ref[0, :, j * LANES:(j + 1) * LANES].astype(F32), c, s1, s2)
        y = jnp.where(is_pad, 0.0, y)
        for g2 in range(LANES // NSA_DH):
            g = j * (LANES // NSA_DH) + g2
            ko_ref[0, g] = jnp.concatenate([y[:, g2 * NSA_DH:(g2 + 1) * NSA_DH], extra], axis=1).astype(BF16)
    for g in range(NSA_KV):
        vo_ref[0, g, 0:NSA_DH, :] = jnp.where(is_pad, 0.0, vt[g * NSA_DH:(g + 1) * NSA_DH]).astype(BF16)
        vo_ref[0, g, NSA_DH:NSA_DH + SUBLANES_BF16, :] = jnp.ones((SUBLANES_BF16, tt), BF16)


def _kv_prep(proj, k_block, v_block, tables, pad_rows, with_block_id, bsz, seq):
    tt = min(TOK_TILE, seq)
    pad_blocks = pad_rows // tt
    assert pad_blocks * tt == pad_rows
    src_t = lambda t: jnp.maximum(t - pad_blocks, 0)
    tab = pl.BlockSpec((tt, LANES), lambda b, t: (src_t(t), 0))
    rows = seq + pad_rows
    return pl.pallas_call(
        functools.partial(_kv_prep_kernel, pad_blocks=pad_blocks, with_block_id=with_block_id),
        grid=(bsz, rows // tt),
        in_specs=[pl.BlockSpec((1, tt, NSA_KVW), lambda b, t: (b, src_t(t), k_block)),
                  pl.BlockSpec((1, tt, NSA_KVW), lambda b, t: (b, src_t(t), v_block)), tab, tab, tab],
        out_specs=[pl.BlockSpec((1, NSA_KV, tt, 2 * NSA_DH), lambda b, t: (b, 0, t, 0)),
                   pl.BlockSpec((1, NSA_KV, NSA_DH + SUBLANES_BF16, tt), lambda b, t: (b, 0, 0, t))],
        out_shape=[jax.ShapeDtypeStruct((bsz, NSA_KV, rows, 2 * NSA_DH), BF16),
                   jax.ShapeDtypeStruct((bsz, NSA_KV, NSA_DH + SUBLANES_BF16, rows), BF16)],
        compiler_params=_cparams(("parallel", "parallel")),
    )(proj, proj, *tables)


def _rope_tables(seq):
    half = ROPE_DIM // 2
    inv = jnp.asarray(ROPE_THETA ** (-np.arange(half) * 2.0 / ROPE_DIM), F32)
    ang = jnp.arange(seq, dtype=F32)[:, None] * inv[None, :]
    cos, sin = jnp.cos(ang), jnp.sin(ang)
    lane = np.arange(LANES) % NSA_DH
    pick = lane % half
    cos_l = cos[:, pick]
    sin_l = sin[:, pick]
    first = jnp.asarray(lane < half)
    second = jnp.asarray((lane >= half) & (lane < ROPE_DIM))
    c = jnp.where(first | second, cos_l, 1.0)
    s1 = jnp.where(first, -sin_l, 0.0)
    s2 = jnp.where(second, sin_l, 0.0)
    return c, s1, s2


def _gelu_tanh(x):
    return 0.5 * x * (1.0 + jnp.tanh(np.sqrt(2.0 / np.pi).astype(np.float32) * (x + 0.044715 * (x * x * x))))


def _compress_kernel(xk_ref, xv_ref, pek_ref, pev_ref, wk1_ref, wk2_ref, wv1_ref, wv2_ref, ok_ref, ov_ref):
    half_w = CMP_STRIDE * NSA_DH

    def one(x_ref, pe_ref, w1_ref, w2_ref):
        x = x_ref[0, 0]
        n_slab = x.shape[0]
        p0 = _dot(x, w1_ref[0:half_w, :])
        p1 = _dot(x, w1_ref[half_w:2 * half_w, :])
        pe = jnp.broadcast_to(pe_ref[...], (8, 2 * half_w)).astype(BF16)
        pw = _dot(pe, w1_ref[...])[0:1, :]
        h = p0 + pltpu.roll(p1, n_slab - 1, 0) + pw
        return _dot(_gelu_tanh(h).astype(BF16), w2_ref[...])

    ok_ref[0, 0] = one(xk_ref, pek_ref, wk1_ref, wk2_ref).astype(ok_ref.dtype)
    ov_ref[0, 0] = one(xv_ref, pev_ref, wv1_ref, wv2_ref).astype(ov_ref.dtype)


def _compress(xk, xv, pek, pev, wk1, wk2, wv1, wv2):
    bsz, grp, n_slab, wdt = xk.shape
    xs = pl.BlockSpec((1, 1, n_slab, wdt), lambda b, g: (b, g, 0, 0))
    full = lambda a: pl.BlockSpec(a.shape, lambda b, g: (0,) * a.ndim)
    osd = jax.ShapeDtypeStruct((bsz, grp, n_slab, NSA_DH), BF16)
    os_ = pl.BlockSpec((1, 1, n_slab, NSA_DH), lambda b, g: (b, g, 0, 0))
    return pl.pallas_call(
        _compress_kernel,
        grid=(bsz, grp),
        in_specs=[xs, xs, full(pek), full(pev), full(wk1), full(wk2), full(wv1), full(wv2)],
        out_specs=[os_, os_],
        out_shape=[osd, osd],
        compiler_params=_cparams(("parallel", "parallel")),
    )(xk, xv, pek, pev, wk1, wk2, wv1, wv2)


def _attn_kernel(nq_ref, qr_ref, kcmp_ref, vcmpt_ref, ks_ref, vst_ref, kw_ref, vwt_ref, ng_ref, ovt_ref,
                 o_ref, gt_scr, qaug_scr, *, seq):
    qn = ATT_Q
    grp = NSA_GROUP
    ln = grp * qn
    gw = grp * NSA_DH
    n_sel = seq // SEL_BLOCK
    top = min(SEL_TOPK, n_sel)
    groups = range(ATT_GROUPS)
    g_first = pl.program_id(1) * ATT_GROUPS
    q0 = pl.multiple_of(pl.program_id(2) * qn, qn)

    def stack_heads(x):
        return jnp.concatenate([x[:, r * NSA_DH:(r + 1) * NSA_DH] for r in range(grp)], axis=0)

    def add_per_head(s, bias):
        return jnp.concatenate([s[:, r * qn:(r + 1) * qn] + bias for r in range(grp)], axis=1)

    t_lane = q0 + (lax.broadcasted_iota(jnp.int32, (1, ln), 1) & (qn - 1))
    t_q = q0 + lax.broadcasted_iota(jnp.int32, (1, qn), 1)

    n_slab = kcmp_ref.shape[2]
    wlen = WINDOW + qn
    pad_col = jnp.where(lax.broadcasted_iota(jnp.int32, (ln, NSA_DH), 1) == 0, NEG, 0.0).astype(BF16)
    qs = [stack_heads(nq_ref[0, :, g * gw:(g + 1) * gw]) for g in groups]
    qrs = [stack_heads(qr_ref[0, :, g * gw:(g + 1) * gw]) for g in groups]
    sc = [_dot_nt(kcmp_ref[0, g], qs[g]) * (ATT_SCALE * LOG2E) for g in groups]
    sw = [_dot_nt(kw_ref[0, g, pl.ds(q0, wlen), :], jnp.concatenate([qrs[g], pad_col], axis=1))
          for g in groups]
    sd = [_dot_nt(ks_ref[0, g, pl.ds(q0, qn), :][:, 0:NSA_DH], qrs[g]) for g in groups]

    cend = lax.broadcasted_iota(jnp.int32, (n_slab, 1), 0) * CMP_STRIDE + (CMP_LEN - 1)
    cmp_bias = jnp.where(cend <= t_q, 0.0, NEG)
    ovt = ovt_ref[...]
    oc, imp = [], []
    for g in groups:
        sm = add_per_head(sc[g], cmp_bias)
        e = jnp.exp2(sm - _colmax(sm))
        den = jnp.sum(e, axis=0, keepdims=True)
        p = e * jnp.where(t_lane >= CMP_LEN - 1, 1.0 / den, 0.0)
        oc.append(_dot(vcmpt_ref[0, g], p.astype(BF16)))
        psum = p[:, 0:qn]
        for r in range(1, grp):
            psum = psum + p[:, r * qn:(r + 1) * qn]
        hi, mid, lo = _split3(psum)
        imp.append(_dot(ovt, hi) + _dot(ovt, mid) + _dot(ovt, lo))

    i_loc = lax.broadcasted_iota(jnp.int32, (qn, 1), 0)
    j_loc = lax.broadcasted_iota(jnp.int32, (1, qn), 1)
    win_lo = jnp.where(i_loc > j_loc, 0.0, NEG)
    win_hi = jnp.where(i_loc <= j_loc, 0.0, NEG)
    ow = []
    for g in groups:
        s = jnp.concatenate([add_per_head(sw[g][0:qn], win_lo), sw[g][qn:WINDOW],
                             add_per_head(sw[g][WINDOW:wlen], win_hi)], axis=0)
        pw = jnp.exp2(s - _colmax(s))
        rw = _dot(vwt_ref[0, g, :, pl.ds(q0, wlen)], pw.astype(BF16))
        ow.append(rw[0:NSA_DH] / rw[NSA_DH:NSA_DH + 1])

    kpos = q0 + lax.broadcasted_iota(jnp.int32, (qn, 1), 0)
    diag_bias = jnp.where(kpos <= t_q, 0.0, NEG)
    init = []
    for g in groups:
        s = add_per_head(sd[g], diag_bias)
        m0 = _colmax(s)
        init += [m0, _dot(vst_ref[0, g, :, pl.ds(q0, qn)], jnp.exp2(s - m0).astype(BF16))]

    gt_scr[...] = jax.nn.sigmoid(ng_ref[0].astype(F32)).T

    def gate_row(branch, g):
        return jnp.concatenate(
            [gt_scr[pl.ds(branch * NSA_HEADS + (g_first + g) * grp + r, 1), :] for r in range(grp)], axis=1)

    out_cw = [gate_row(0, g) * oc[g] + gate_row(2, g) * ow[g] for g in groups]
    gate_sel = [gate_row(1, g) for g in groups]

    jidx = lax.broadcasted_iota(jnp.int32, (n_sel, qn), 0)
    tq = q0 + lax.broadcasted_iota(jnp.int32, (n_sel, qn), 1)
    cur = tq // SEL_BLOCK
    forced = (jidx == 0) | (jidx == cur) | (jidx == cur - 1)
    causal_blk = jidx * SEL_BLOCK <= tq
    rows8 = 8
    jrow = lax.broadcasted_iota(jnp.int32, (rows8, qn), 0)
    for g in groups:
        val = jnp.where(forced, -2.0, imp[g])
        val = jnp.where(causal_blk, val, -1.0)
        parts = [val[v * rows8:(v + 1) * rows8] for v in range(n_sel // rows8)]
        counts = [jnp.zeros((rows8, qn), F32) for _ in parts]
        for jp in range(n_sel):
            other = jnp.broadcast_to(val[jp:jp + 1, :], (rows8, qn))
            for v, part in enumerate(parts):
                ge = jnp.where(other >= part, 1.0, 0.0)
                gt = jnp.where(other > part, 1.0, 0.0)
                if v * rows8 > jp:
                    before = ge
                elif (v + 1) * rows8 <= jp:
                    before = gt
                else:
                    before = jnp.where(jrow > jp - v * rows8, ge, gt)
                counts[v] = counts[v] + before
        rank = jnp.concatenate(counts, axis=0)
        chosen = forced | (rank < float(max(top - 3, 0)))
        selb = jnp.where(chosen & causal_blk, 0.0, NEG)
        selb_t = jnp.concatenate([selb, jnp.zeros((LANES - n_sel, qn), F32)], axis=0).T
        selb_t = selb_t[:, 0:NSA_DH].astype(BF16)
        qaug_scr[g] = jnp.concatenate([qrs[g], jnp.concatenate([selb_t] * grp, axis=0)], axis=1)

    kt_len = ATT_KT
    sub = kt_len // ATT_SUBTILES

    def sweep_block(k_base, n_sub, carry):
        state = list(carry)
        items = [(j, g) for j in range(n_sub) for g in groups]
        k0s = [pl.multiple_of(k_base + j * sub, sub) for j in range(n_sub)]
        score = lambda j, g: _dot_nt(ks_ref[0, g, pl.ds(k0s[j], sub), :], qaug_scr[g])
        ahead = min(ATT_LOOKAHEAD * ATT_GROUPS, len(items))
        scores = [score(*it) for it in items[:ahead]]
        for n, (j, g) in enumerate(items):
            s = scores[n]
            m_run, acc = state[2 * g], state[2 * g + 1]
            m_new = jnp.maximum(m_run, _colmax(s))
            alpha = jnp.exp2(m_run - m_new)
            pt = jnp.exp2(s - m_new).astype(BF16)
            state[2 * g + 1] = alpha * acc + _dot(vst_ref[0, g, :, pl.ds(k0s[j], sub)], pt)
            state[2 * g] = m_new
            if n + ahead < len(items):
                scores.append(score(*items[n + ahead]))
        return tuple(state)

    n_whole = q0 // kt_len
    carry = lax.fori_loop(0, n_whole, lambda kt, cy: sweep_block(kt * kt_len, ATT_SUBTILES, cy), tuple(init))
    tails = [functools.partial(sweep_block, n_whole * kt_len, r) for r in range(ATT_SUBTILES)]
    final = lax.switch((q0 - n_whole * kt_len) // sub, tails, carry)

    for g in groups:
        acc = final[2 * g + 1]
        out_t = out_cw[g] + gate_sel[g] * (acc[0:NSA_DH] / acc[NSA_DH:NSA_DH + 1])
        pieces = []
        for r2 in range(grp // 2):
            pair = jnp.concatenate([out_t[:, (2 * r2) * qn:(2 * r2 + 1) * qn],
                                    out_t[:, (2 * r2 + 1) * qn:(2 * r2 + 2) * qn]], axis=0)
            pieces.append(pair.T)
        o_ref[0, :, g * gw:(g + 1) * gw] = jnp.concatenate(pieces, axis=1).astype(o_ref.dtype)


def _attention(proj, q_rot, k_cmp, v_cmp_t, ks, vs_t, kw, vw_t, ov_t, bsz, seq):
    qn = ATT_Q
    gw = ATT_GROUPS * NSA_GROUP * NSA_DH
    n_slab = k_cmp.shape[2]
    n_sel = seq // SEL_BLOCK
    qspec = pl.BlockSpec((1, qn, gw), lambda b, g, i: (b, i, g))
    per_group = lambda a: pl.BlockSpec((1, ATT_GROUPS) + a.shape[2:], lambda b, g, i: (b, g, 0, 0))
    return pl.pallas_call(
        functools.partial(_attn_kernel, seq=seq),
        grid=(bsz, NSA_KV // ATT_GROUPS, seq // qn),
        in_specs=[pl.BlockSpec((1, qn, gw), lambda b, g, i: (b, i, P_NQ // gw + g)), qspec,
                  per_group(k_cmp), per_group(v_cmp_t), per_group(ks), per_group(vs_t),
                  per_group(kw), per_group(vw_t),
                  pl.BlockSpec((1, qn, LANES), lambda b, g, i: (b, i, P_NG // LANES)),
                  pl.BlockSpec((n_sel, n_slab), lambda b, g, i: (0, 0))],
        out_specs=qspec,
        out_shape=jax.ShapeDtypeStruct((bsz, seq, NSA_WIDTH), BF16),
        scratch_shapes=[pltpu.VMEM((LANES, qn), F32),
                        pltpu.VMEM((ATT_GROUPS, NSA_GROUP * qn, 2 * NSA_DH), BF16)],
        compiler_params=_cparams(("parallel", "parallel", "arbitrary")),
    )(proj, q_rot, k_cmp, v_cmp_t, ks, vs_t, kw, vw_t, proj, ov_t)


def _merge_kernel(x_ref, ya_ref, yb_ref, ga_ref, gb_ref, wa_ref, wb_ref, wo_ref, gn_ref, o_ref):
    ga = jax.nn.sigmoid(ga_ref[...].astype(F32))
    gb = jax.nn.sigmoid(gb_ref[...].astype(F32))
    merged = ga * _dot(ya_ref[...], wa_ref[...]) + gb * _dot(yb_ref[...], wb_ref[...])
    mo = _dot(merged.astype(BF16), wo_ref[...])
    o_ref[...] = x_ref[...] + _rms(mo, gn_ref[...])


def _merge(x, ya, yb, gates, wa, wb, wo, gn):
    n, d = x.shape
    tm = min(TOK_TILE, n)
    row = lambda c: pl.BlockSpec((tm, d), lambda i: (i, c))
    wsp = pl.BlockSpec((d, d), lambda i: (0, 0))
    return pl.pallas_call(
        _merge_kernel,
        grid=(n // tm,),
        in_specs=[row(0), row(0), row(0), row(P_GA // d), row(P_GB // d), wsp, wsp, wsp,
                  pl.BlockSpec((1, d), lambda i: (0, 0))],
        out_specs=row(0),
        out_shape=jax.ShapeDtypeStruct((n, d), F32),
        compiler_params=_cparams(("parallel",)),
    )(x, ya, yb, gates, gates, wa, wb, wo, gn)


def _mlp_kernel(h_ref, gpre_ref, wu_ref, wd_ref, gpost_ref, o_ref, v_scr, acc_scr):
    j = pl.program_id(1)

    @pl.when(j == 0)
    def _():
        v_scr[...] = _rms(h_ref[...], gpre_ref[...]).astype(BF16)
        acc_scr[...] = jnp.zeros_like(acc_scr)

    up = _dot(v_scr[...], wu_ref[...])
    act = jnp.square(jnp.maximum(up, 0.0))
    acc_scr[...] += _dot(act.astype(BF16), wd_ref[...])

    @pl.when(j == pl.num_programs(1) - 1)
    def _():
        o_ref[...] = h_ref[...] + _rms(acc_scr[...], gpost_ref[...])


def _mlp(h, gpre, wu, wd, gpost):
    n, d = h.shape
    dff = wu.shape[1]
    tm = min(2 * TOK_TILE, n)
    tf = 1024
    vec = pl.BlockSpec((1, d), lambda i, j: (0, 0))
    return pl.pallas_call(
        _mlp_kernel,
        grid=(n // tm, dff // tf),
        in_specs=[pl.BlockSpec((tm, d), lambda i, j: (i, 0)), vec,
                  pl.BlockSpec((d, tf), lambda i, j: (0, j)),
                  pl.BlockSpec((tf, d), lambda i, j: (j, 0)), vec],
        out_specs=pl.BlockSpec((tm, d), lambda i, j: (i, 0)),
        out_shape=jax.ShapeDtypeStruct((n, d), F32),
        scratch_shapes=[pltpu.VMEM((tm, d), BF16), pltpu.VMEM((tm, d), F32)],
        compiler_params=_cparams(("parallel", "arbitrary")),
    )(h, gpre, wu, wd, gpost)


def _ple_kernel(h_ref, p_ref, wp_ref, wg_ref, gn_ref, o_ref):
    h = h_ref[...]
    e = _dot(p_ref[...].astype(BF16), wp_ref[...]) * jax.nn.sigmoid(_dot(h.astype(BF16), wg_ref[...]))
    o_ref[...] = h + _rms(e, gn_ref[...])


def _ple(h, p, wp, wg, gn):
    n, d = h.shape
    pd = p.shape[1]
    tm = min(TOK_TILE, n)
    return pl.pallas_call(
        _ple_kernel,
        grid=(n // tm,),
        in_specs=[pl.BlockSpec((tm, d), lambda i: (i, 0)),
                  pl.BlockSpec((tm, pd), lambda i: (i, 0)),
                  pl.BlockSpec((pd, d), lambda i: (0, 0)),
                  pl.BlockSpec((d, d), lambda i: (0, 0)),
                  pl.BlockSpec((1, d), lambda i: (0, 0))],
        out_specs=pl.BlockSpec((tm, d), lambda i: (i, 0)),
        out_shape=jax.ShapeDtypeStruct((n, d), F32),
        compiler_params=_cparams(("parallel",)),
    )(h, p, wp, wg, gn)


def _overlap_t(seq):
    n_slab = seq // CMP_STRIDE
    n_cmp = (seq - CMP_LEN) // CMP_STRIDE + 1
    n_sel = seq // SEL_BLOCK
    start = np.arange(n_slab) * CMP_STRIDE
    end = start + CMP_LEN - 1
    sel_start = np.arange(n_sel) * SEL_BLOCK
    ov = (start[None, :] < sel_start[:, None] + SEL_BLOCK) & (end[None, :] >= sel_start[:, None])
    ov = ov & (np.arange(n_slab)[None, :] < n_cmp)
    return jnp.asarray(ov.astype(np.float32), BF16)


def kernel(x, p, w_in, w_branch_a, w_branch_b, w_out, norm_pre_mix, norm_post_mix, norm_pre_mlp,
           norm_post_mlp, hg_lb_logits, hg_gnorm, cmp_pe_k, cmp_pe_v, cmp_wk1, cmp_wk2, cmp_wv1, cmp_wv2,
           w_up, w_down, w_ple, w_ple_gate, norm_ple):
    bsz, seq, d = x.shape
    depth = w_in.shape[0]
    assert depth == 1 and hg_lb_logits.shape[0] == 2 and d == D_MODEL
    assert seq % ATT_KT == 0 and seq >= WINDOW + ATT_Q and seq // SEL_BLOCK <= NSA_DH
    assert seq % HG_TOK == 0 and (HG_TOK // HG_CHUNK) % HG_GROUP == 0
    assert ATT_Q in (SEL_BLOCK, 2 * SEL_BLOCK)
    assert ATT_Q % (ATT_KT // ATT_SUBTILES) == 0
    n = bsz * seq
    x2 = x.reshape(n, d)
    row = lambda a: a.reshape(1, -1)

    w = w_in[0]
    s_hq, s_hf, s_hi, s_hg = 0, HG_WIDTH, 2 * HG_WIDTH, 3 * HG_WIDTH
    s_nq = 4 * HG_WIDTH
    s_ng = s_nq + NSA_WIDTH + 6 * NSA_KVW
    s_ga = s_ng + 3 * NSA_HEADS
    col = lambda a, b: w[:, a:b]
    w_all = jnp.concatenate(
        [col(s_hq, s_hf), col(s_hi, s_hg), col(s_hg, s_nq), col(s_hf, s_hi), col(s_nq, s_ga),
         jnp.zeros((d, P_GA - P_NG - 3 * NSA_HEADS), F32), col(s_ga, s_ga + 2 * D_MODEL)], axis=1).astype(BF16)
    assert w_all.shape[1] == P_WIDTH
    proj2 = _rms_matmul(x2, row(norm_pre_mix[0]), w_all, BF16, 1024)
    proj = proj2.reshape(bsz, seq, P_WIDTH)

    y_a = _hgrn(proj, hg_lb_logits, row(hg_gnorm[0]), bsz, seq)

    tables = _rope_tables(seq)
    kvb = P_KV // NSA_KVW
    q_rot = _rope(proj, P_NQ // NSA_WIDTH, NSA_WIDTH, tables, ATT_SCALE * LOG2E, bsz, seq)
    ks_aug, vs_t = _kv_prep(proj, kvb + 2, kvb + 3, tables, 0, True, bsz, seq)
    kw_aug, vw_t = _kv_prep(proj, kvb + 4, kvb + 5, tables, WINDOW, False, bsz, seq)
    kv_slice = lambda i: proj[:, :, P_KV + i * NSA_KVW: P_KV + (i + 1) * NSA_KVW]
    heads = lambda t: t.reshape(bsz, seq, NSA_KV, NSA_DH).transpose(0, 2, 1, 3)
    slabs = lambda t: heads(t).reshape(bsz, NSA_KV, seq // CMP_STRIDE, CMP_STRIDE * NSA_DH)
    k_cmp, v_cmp = _compress(
        slabs(kv_slice(0)), slabs(kv_slice(1)), cmp_pe_k[0].reshape(1, -1), cmp_pe_v[0].reshape(1, -1),
        cmp_wk1[0].astype(BF16), cmp_wk2[0].astype(BF16), cmp_wv1[0].astype(BF16), cmp_wv2[0].astype(BF16))
    y_b = _attention(proj, q_rot, k_cmp, v_cmp.transpose(0, 1, 3, 2), ks_aug, vs_t, kw_aug, vw_t,
                     _overlap_t(seq), bsz, seq)

    h1 = _merge(x2, y_a.reshape(n, -1), y_b.reshape(n, -1), proj2, w_branch_a[0].astype(BF16),
                w_branch_b[0].astype(BF16), w_out[0].astype(BF16), row(norm_post_mix[0]))
    h2 = _mlp(h1, row(norm_pre_mlp[0]), w_up[0].astype(BF16), w_down[0].astype(BF16), row(norm_post_mlp[0]))
    h3o = _ple(h2, p[0].reshape(n, -1), w_ple[0].astype(BF16), w_ple_gate[0].astype(BF16), row(norm_ple[0]))
    return h3o.reshape(bsz, seq, d)
```

```python
import functools

import numpy as np
import jax
import jax.numpy as jnp
from jax import lax
from jax.experimental import pallas as pl
from jax.experimental.pallas import tpu as pltpu

F32 = jnp.float32
BF16 = jnp.bfloat16

D_MODEL = 1024
HG_HEADS = 8
HG_DK = 128
HG_DV = 128
HG_WIDTH = HG_HEADS * HG_DK
HG_CHUNK = 64
NSA_HEADS = 16
NSA_KV = 4
NSA_GROUP = NSA_HEADS // NSA_KV
NSA_DH = 64
NSA_WIDTH = NSA_HEADS * NSA_DH
NSA_KVW = NSA_KV * NSA_DH
CMP_LEN = 32
CMP_STRIDE = 16
CMP_HIDDEN = 256
SEL_BLOCK = 64
SEL_TOPK = 16
WINDOW = 512
ROPE_THETA = 500000.0
ROPE_DIM = NSA_DH // 4
D_FF = 4 * D_MODEL
EPS = 1e-6
NEG = -1e30
ATT_SCALE = NSA_DH ** -0.5
LOG2E = float(np.log2(np.e))

LANES = 128
SUBLANES_BF16 = 16
VMEM_LIMIT = 56 * 1024 * 1024

TOK_TILE = 512
ATT_Q = 128
ATT_GROUPS = 4
ATT_KT = 1024
ATT_SUBTILES = 8
ATT_LOOKAHEAD = 7
HG_TOK = 1024
HG_GROUP = 16

P_HQ, P_HI, P_HG, P_HF = 0, HG_WIDTH, 2 * HG_WIDTH, 3 * HG_WIDTH
P_NQ = 4 * HG_WIDTH
P_KV = P_NQ + NSA_WIDTH
P_NG = P_KV + 6 * NSA_KVW
P_GA = 7 * D_MODEL
P_GB = 8 * D_MODEL
P_WIDTH = 9 * D_MODEL
HG_EXP2_LIMIT = 115.0


def _cparams(sem):
    return pltpu.CompilerParams(dimension_semantics=sem, vmem_limit_bytes=VMEM_LIMIT)


def _rms(x, gain):
    ms = jnp.mean(x * x, axis=-1, keepdims=True)
    return (x * lax.rsqrt(ms + EPS)) * gain


def _silu(x):
    return x * (0.5 * jnp.tanh(0.5 * x) + 0.5)


def _dot(a, b):
    return jnp.dot(a, b, preferred_element_type=F32)


def _dot_nt(a, b):
    return lax.dot_general(a, b, (((1,), (1,)), ((), ())), preferred_element_type=F32)


def _dot_tn(a, b):
    return lax.dot_general(a, b, (((0,), (0,)), ((), ())), preferred_element_type=F32)


def _colmax(s):
    parts = [s[i:i + 64] for i in range(0, s.shape[0], 64)] if s.shape[0] % 64 == 0 else [s]
    while len(parts) > 1:
        parts = [jnp.maximum(parts[i], parts[i + 1]) if i + 1 < len(parts) else parts[i]
                 for i in range(0, len(parts), 2)]
    return jnp.max(parts[0], axis=0, keepdims=True)


def _split3(x):
    hi = x.astype(BF16)
    r1 = x - hi.astype(F32)
    mid = r1.astype(BF16)
    lo = (r1 - mid.astype(F32)).astype(BF16)
    return hi, mid, lo


def _rms_matmul_kernel(x_ref, g_ref, w_ref, o_ref, u_scr):
    @pl.when(pl.program_id(1) == 0)
    def _():
        u_scr[...] = _rms(x_ref[...], g_ref[...]).astype(BF16)

    o_ref[...] = _dot(u_scr[...], w_ref[...]).astype(o_ref.dtype)


def _rms_matmul(x, gain, w, out_dtype, tn):
    n, d = x.shape
    wd = w.shape[1]
    tm = min(2 * TOK_TILE, n)
    return pl.pallas_call(
        _rms_matmul_kernel,
        grid=(n // tm, wd // tn),
        in_specs=[pl.BlockSpec((tm, d), lambda i, j: (i, 0)),
                  pl.BlockSpec((1, d), lambda i, j: (0, 0)),
                  pl.BlockSpec((d, tn), lambda i, j: (0, j))],
        out_specs=pl.BlockSpec((tm, tn), lambda i, j: (i, j)),
        out_shape=jax.ShapeDtypeStruct((n, wd), out_dtype),
        scratch_shapes=[pltpu.VMEM((tm, d), BF16)],
        compiler_params=_cparams(("parallel", "arbitrary")),
    )(x, gain, w)


def _hgrn_levels():
    return [HG_CHUNK >> (i + 1) for i in range(HG_CHUNK.bit_length() - 1)]


def _hgrn_intra_exact(qf, kk, b, gsel_ref):
    c_len = HG_CHUNK
    row = lax.broadcasted_iota(jnp.int32, (c_len, c_len), 0)
    col = lax.broadcasted_iota(jnp.int32, (c_len, c_len), 1)
    trow = lax.broadcasted_iota(jnp.int32, (c_len, HG_DK), 0)
    a = jnp.where(row == col, _dot_nt(qf.astype(BF16), kk.astype(BF16)), 0.0)
    b3 = jnp.concatenate(_split3(b), axis=0)
    for lvl, h in enumerate(_hgrn_levels()):
        b_at_ref = _dot(gsel_ref[lvl], b3)
        upper = (trow & (2 * h - 1)) >= h
        qe = jnp.where(upper, qf * jnp.exp2(jnp.minimum(b - b_at_ref, 0.0)), 0.0)
        ke = jnp.where(upper, 0.0, kk * jnp.exp2(jnp.minimum(b_at_ref - b, 0.0)))
        same_block = (row & -(2 * h)) == (col & -(2 * h))
        block = same_block & ((row & (2 * h - 1)) >= h) & ((col & (2 * h - 1)) < h)
        a = a + jnp.where(block, _dot_nt(qe.astype(BF16), ke.astype(BF16)), 0.0)
    return a


def _hgrn_kernel(q_ref, i_ref, g_ref, f_ref, lbl_ref, gn_ref, tri_ref, gsel_ref, o_ref, st_ref, st0_scr, *,
                 n_chunks):
    c_len = HG_CHUNK
    half = c_len // 2

    @pl.when(pl.program_id(2) == 0)
    def _():
        st_ref[...] = jnp.zeros_like(st_ref)

    lg = lbl_ref[...]
    mx = jnp.max(lg, axis=0, keepdims=True)
    ex = jnp.exp(lg - mx)
    lb = ex[0:1, :] / jnp.sum(ex, axis=0, keepdims=True)

    row = lax.broadcasted_iota(jnp.int32, (c_len, c_len), 0)
    col = lax.broadcasted_iota(jnp.int32, (c_len, c_len), 1)
    causal = col <= row
    tri3 = tri_ref[...]

    def run_group(first, state, exact):
        chunks = [slice((first + c) * c_len, (first + c + 1) * c_len) for c in range(HG_GROUP)]
        qf, kk, lf = [], [], []
        for rows in chunks:
            qx = q_ref[0, rows, :].astype(F32)
            qf.append(_silu(qx) * (HG_DK ** -0.5))
            forget = lb + (1.0 - lb) * jax.nn.sigmoid(f_ref[0, rows, :].astype(F32))
            kk.append(1.0 - forget)
            lf.append(jnp.log2(forget))
        bb = [_dot(tri3, jnp.concatenate(_split3(x), axis=0)) for x in lf]
        qe, ke, qb, kd, dlast = [], [], [], [], []
        need = jnp.zeros((1, HG_DK), F32)
        for c in range(HG_GROUP):
            b = bb[c]
            b_mid = b[half - 1:half, :]
            b_last = b[c_len - 1:c_len, :]
            if not exact:
                qe.append((qf[c] * jnp.exp2(b - b_mid)).astype(BF16))
                ke.append((kk[c] * jnp.exp2(b_mid - b)).astype(BF16))
                need = jnp.maximum(need, jnp.maximum(b[0:1, :] - b_mid, b_mid - b_last))
            qb.append((qf[c] * jnp.exp2(b)).astype(BF16))
            kd.append((kk[c] * jnp.exp2(b_last - b)).astype(BF16))
            dlast.append(jnp.exp2(b_last))
        vv = [i_ref[0, rows, :] for rows in chunks]
        if exact:
            aa = [_hgrn_intra_exact(qf[c], kk[c], bb[c], gsel_ref) for c in range(HG_GROUP)]
        else:
            aa = [_dot_nt(qe[c], ke[c]) for c in range(HG_GROUP)]
        kv = [_dot_tn(vv[c], kd[c]) for c in range(HG_GROUP)]
        o_intra = [_dot(jnp.where(causal, aa[c], 0.0).astype(BF16), vv[c]) for c in range(HG_GROUP)]
        states = [state]
        for c in range(HG_GROUP):
            states.append(states[c] * dlast[c] + kv[c])
        o_inter = [_dot_nt(qb[c], states[c].astype(BF16)) for c in range(HG_GROUP)]
        for c, rows in enumerate(chunks):
            gx = g_ref[0, rows, :].astype(F32)
            y = _rms(o_intra[c] + o_inter[c], gn_ref[...]) * _silu(gx)
            o_ref[0, rows, :] = y.astype(o_ref.dtype)
        return states[HG_GROUP], need

    def run_all(exact):
        state = st0_scr[...]
        need = jnp.zeros((1, HG_DK), F32)
        for first in range(0, n_chunks, HG_GROUP):
            state, need_g = run_group(first, state, exact)
            need = jnp.maximum(need, need_g)
        return state, need

    st0_scr[...] = st_ref[...]
    state, need = run_all(exact=False)
    st_ref[...] = state

    @pl.when(jnp.max(need) > HG_EXP2_LIMIT)
    def _():
        run_all(exact=True)


def _hgrn(proj, lb_logits, gnorm, bsz, seq):
    n_chunks = HG_TOK // HG_CHUNK
    tri = np.tril(np.ones((HG_CHUNK, HG_CHUNK), np.float32))
    tri3 = jnp.asarray(np.concatenate([tri, tri, tri], axis=1), BF16)
    t_idx = np.arange(HG_CHUNK)
    gsel = np.zeros((len(_hgrn_levels()), HG_CHUNK, HG_CHUNK), np.float32)
    for lvl, h in enumerate(_hgrn_levels()):
        gsel[lvl, t_idx, (t_idx // (2 * h)) * (2 * h) + h - 1] = 1.0
    gsel3 = jnp.asarray(np.concatenate([gsel, gsel, gsel], axis=2), BF16)
    hh = HG_HEADS
    blk = (1, HG_TOK, HG_DK)
    head_cols = lambda off: pl.BlockSpec(blk, lambda b, h, t: (b, t, off // HG_DK + h))
    return pl.pallas_call(
        functools.partial(_hgrn_kernel, n_chunks=n_chunks),
        grid=(bsz, hh, seq // HG_TOK),
        in_specs=[head_cols(P_HQ), head_cols(P_HI), head_cols(P_HG), head_cols(P_HF),
                  pl.BlockSpec((2, HG_DK), lambda b, h, t: (0, h)),
                  pl.BlockSpec((1, HG_DV), lambda b, h, t: (0, 0)),
                  pl.BlockSpec((HG_CHUNK, 3 * HG_CHUNK), lambda b, h, t: (0, 0)),
                  pl.BlockSpec(gsel3.shape, lambda b, h, t: (0, 0, 0))],
        out_specs=pl.BlockSpec(blk, lambda b, h, t: (b, t, h)),
        out_shape=jax.ShapeDtypeStruct((bsz, seq, HG_WIDTH), BF16),
        scratch_shapes=[pltpu.VMEM((HG_DV, HG_DK), F32), pltpu.VMEM((HG_DV, HG_DK), F32)],
        compiler_params=_cparams(("parallel", "parallel", "arbitrary")),
    )(proj, proj, proj, proj, lb_logits, gnorm, tri3, gsel3)


def _rope_kernel(x_ref, c_ref, s1_ref, s2_ref, o_ref, *, scale):
    width = x_ref.shape[-1]
    c = c_ref[...]
    s1 = s1_ref[...]
    s2 = s2_ref[...]
    for j in range(width // LANES):
        cols = slice(j * LANES, (j + 1) * LANES)
        x = x_ref[0, :, cols].astype(F32)
        if scale != 1.0:
            x = x * scale
        o_ref[0, :, cols] = _rope_lanes(x, c, s1, s2).astype(o_ref.dtype)


def _rope(src, col_block, width, tables, scale, bsz, seq):
    tt = min(TOK_TILE, seq)
    tab = pl.BlockSpec((tt, LANES), lambda b, t: (t, 0))
    return pl.pallas_call(
        functools.partial(_rope_kernel, scale=scale),
        grid=(bsz, seq // tt),
        in_specs=[pl.BlockSpec((1, tt, width), lambda b, t: (b, t, col_block)), tab, tab, tab],
        out_specs=pl.BlockSpec((1, tt, width), lambda b, t: (b, t, 0)),
        out_shape=jax.ShapeDtypeStruct((bsz, seq, width), BF16),
        compiler_params=_cparams(("parallel", "parallel")),
    )(src, *tables)


def _rope_lanes(x, c, s1, s2):
    return x * c + pltpu.roll(x, LANES - ROPE_DIM // 2, 1) * s1 + pltpu.roll(x, ROPE_DIM // 2, 1) * s2


def _kv_prep_kernel(k_ref, v_ref, c_ref, s1_ref, s2_ref, ko_ref, vo_ref, *, pad_blocks, with_block_id):
    tt = k_ref.shape[1]
    step = pl.program_id(1)
    is_pad = step < pad_blocks
    pos = (step - pad_blocks) * tt + lax.broadcasted_iota(jnp.int32, (tt, NSA_DH), 0)
    lane = lax.broadcasted_iota(jnp.int32, (tt, NSA_DH), 1)
    if with_block_id:
        extra = jnp.where(pos // SEL_BLOCK == lane, 1.0, 0.0)
    else:
        extra = jnp.where((lane == 0) & is_pad, 1.0, 0.0)
    c, s1, s2 = c_ref[...], s1_ref[...], s2_ref[...]
    vt = v_ref[0].astype(F32).T
    for j in range(NSA_KVW // LANES):
        y = _rope_lanes(k_ref[0, :, j * LANES:(j + 1) * LANES].astype(F32), c, s1, s2)
        y = jnp.where(is_pad, 0.0, y)
        for g2 in range(LANES // NSA_DH):
            g = j * (LANES // NSA_DH) + g2
            ko_ref[0, g] = jnp.concatenate([y[:, g2 * NSA_DH:(g2 + 1) * NSA_DH], extra], axis=1).astype(BF16)
    for g in range(NSA_KV):
        vo_ref[0, g, 0:NSA_DH, :] = jnp.where(is_pad, 0.0, vt[g * NSA_DH:(g + 1) * NSA_DH]).astype(BF16)
        vo_ref[0, g, NSA_DH:NSA_DH + SUBLANES_BF16, :] = jnp.ones((SUBLANES_BF16, tt), BF16)


def _kv_prep(proj, k_block, v_block, tables, pad_rows, with_block_id, bsz, seq):
    tt = min(TOK_TILE, seq)
    pad_blocks = pad_rows // tt
    assert pad_blocks * tt == pad_rows
    src_t = lambda t: jnp.maximum(t - pad_blocks, 0)
    tab = pl.BlockSpec((tt, LANES), lambda b, t: (src_t(t), 0))
    rows = seq + pad_rows
    return pl.pallas_call(
        functools.partial(_kv_prep_kernel, pad_blocks=pad_blocks, with_block_id=with_block_id),
        grid=(bsz, rows // tt),
        in_specs=[pl.BlockSpec((1, tt, NSA_KVW), lambda b, t: (b, src_t(t), k_block)),
                  pl.BlockSpec((1, tt, NSA_KVW), lambda b, t: (b, src_t(t), v_block)), tab, tab, tab],
        out_specs=[pl.BlockSpec((1, NSA_KV, tt, 2 * NSA_DH), lambda b, t: (b, 0, t, 0)),
                   pl.BlockSpec((1, NSA_KV, NSA_DH + SUBLANES_BF16, tt), lambda b, t: (b, 0, 0, t))],
        out_shape=[jax.ShapeDtypeStruct((bsz, NSA_KV, rows, 2 * NSA_DH), BF16),
                   jax.ShapeDtypeStruct((bsz, NSA_KV, NSA_DH + SUBLANES_BF16, rows), BF16)],
        compiler_params=_cparams(("parallel", "parallel")),
    )(proj, proj, *tables)


def _rope_tables(seq):
    half = ROPE_DIM // 2
    inv = jnp.asarray(ROPE_THETA ** (-np.arange(half) * 2.0 / ROPE_DIM), F32)
    ang = jnp.arange(seq, dtype=F32)[:, None] * inv[None, :]
    cos, sin = jnp.cos(ang), jnp.sin(ang)
    lane = np.arange(LANES) % NSA_DH
    pick = lane % half
    cos_l = cos[:, pick]
    sin_l = sin[:, pick]
    first = jnp.asarray(lane < half)
    second = jnp.asarray((lane >= half) & (lane < ROPE_DIM))
    c = jnp.where(first | second, cos_l, 1.0)
    s1 = jnp.where(first, -sin_l, 0.0)
    s2 = jnp.where(second, sin_l, 0.0)
    return c, s1, s2


def _gelu_tanh(x):
    return 0.5 * x * (1.0 + jnp.tanh(np.sqrt(2.0 / np.pi).astype(np.float32) * (x + 0.044715 * (x * x * x))))


def _compress_kernel(xk_ref, xv_ref, pek_ref, pev_ref, wk1_ref, wk2_ref, wv1_ref, wv2_ref, ok_ref, ov_ref):
    half_w = CMP_STRIDE * NSA_DH

    def one(x_ref, pe_ref, w1_ref, w2_ref):
        x = x_ref[0, 0]
        n_slab = x.shape[0]
        p0 = _dot(x, w1_ref[0:half_w, :])
        p1 = _dot(x, w1_ref[half_w:2 * half_w, :])
        pe = jnp.broadcast_to(pe_ref[...], (8, 2 * half_w)).astype(BF16)
        pw = _dot(pe, w1_ref[...])[0:1, :]
        h = p0 + pltpu.roll(p1, n_slab - 1, 0) + pw
        return _dot(_gelu_tanh(h).astype(BF16), w2_ref[...])

    ok_ref[0, 0] = one(xk_ref, pek_ref, wk1_ref, wk2_ref).astype(ok_ref.dtype)
    ov_ref[0, 0] = one(xv_ref, pev_ref, wv1_ref, wv2_ref).astype(ov_ref.dtype)


def _compress(xk, xv, pek, pev, wk1, wk2, wv1, wv2):
    bsz, grp, n_slab, wdt = xk.shape
    xs = pl.BlockSpec((1, 1, n_slab, wdt), lambda b, g: (b, g, 0, 0))
    full = lambda a: pl.BlockSpec(a.shape, lambda b, g: (0,) * a.ndim)
    osd = jax.ShapeDtypeStruct((bsz, grp, n_slab, NSA_DH), BF16)
    os_ = pl.BlockSpec((1, 1, n_slab, NSA_DH), lambda b, g: (b, g, 0, 0))
    return pl.pallas_call(
        _compress_kernel,
        grid=(bsz, grp),
        in_specs=[xs, xs, full(pek), full(pev), full(wk1), full(wk2), full(wv1), full(wv2)],
        out_specs=[os_, os_],
        out_shape=[osd, osd],
        compiler_params=_cparams(("parallel", "parallel")),
    )(xk, xv, pek, pev, wk1, wk2, wv1, wv2)


def _attn_kernel(nq_ref, qr_ref, kcmp_ref, vcmpt_ref, ks_ref, vst_ref, kw_ref, vwt_ref, ng_ref, ovt_ref,
                 o_ref, gt_scr, qaug_scr, *, seq):
    qn = ATT_Q
    grp = NSA_GROUP
    ln = grp * qn
    gw = grp * NSA_DH
    n_sel = seq // SEL_BLOCK
    top = min(SEL_TOPK, n_sel)
    groups = range(ATT_GROUPS)
    g_first = pl.program_id(1) * ATT_GROUPS
    q0 = pl.multiple_of(pl.program_id(2) * qn, qn)

    def stack_heads(x):
        return jnp.concatenate([x[:, r * NSA_DH:(r + 1) * NSA_DH] for r in range(grp)], axis=0)

    def add_per_head(s, bias):
        return jnp.concatenate([s[:, r * qn:(r + 1) * qn] + bias for r in range(grp)], axis=1)

    t_lane = q0 + (lax.broadcasted_iota(jnp.int32, (1, ln), 1) & (qn - 1))
    t_q = q0 + lax.broadcasted_iota(jnp.int32, (1, qn), 1)

    n_slab = kcmp_ref.shape[2]
    wlen = WINDOW + qn
    pad_col = jnp.where(lax.broadcasted_iota(jnp.int32, (ln, NSA_DH), 1) == 0, NEG, 0.0).astype(BF16)
    qs = [stack_heads(nq_ref[0, :, g * gw:(g + 1) * gw]) for g in groups]
    qrs = [stack_heads(qr_ref[0, :, g * gw:(g + 1) * gw]) for g in groups]
    sc = [_dot_nt(kcmp_ref[0, g], qs[g]) * (ATT_SCALE * LOG2E) for g in groups]
    sw = [_dot_nt(kw_ref[0, g, pl.ds(q0, wlen), :], jnp.concatenate([qrs[g], pad_col], axis=1))
          for g in groups]
    sd = [_dot_nt(ks_ref[0, g, pl.ds(q0, qn), :][:, 0:NSA_DH], qrs[g]) for g in groups]

    cend = lax.broadcasted_iota(jnp.int32, (n_slab, 1), 0) * CMP_STRIDE + (CMP_LEN - 1)
    cmp_bias = jnp.where(cend <= t_q, 0.0, NEG)
    ovt = ovt_ref[...]
    oc, imp = [], []
    for g in groups:
        sm = add_per_head(sc[g], cmp_bias)
        e = jnp.exp2(sm - _colmax(sm))
        den = jnp.sum(e, axis=0, keepdims=True)
        p = e * jnp.where(t_lane >= CMP_LEN - 1, 1.0 / den, 0.0)
        oc.append(_dot(vcmpt_ref[0, g], p.astype(BF16)))
        psum = p[:, 0:qn]
        for r in range(1, grp):
            psum = psum + p[:, r * qn:(r + 1) * qn]
        hi, mid, lo = _split3(psum)
        imp.append(_dot(ovt, hi) + _dot(ovt, mid) + _dot(ovt, lo))

    i_loc = lax.broadcasted_iota(jnp.int32, (qn, 1), 0)
    j_loc = lax.broadcasted_iota(jnp.int32, (1, qn), 1)
    win_lo = jnp.where(i_loc > j_loc, 0.0, NEG)
    win_hi = jnp.where(i_loc <= j_loc, 0.0, NEG)
    ow = []
    for g in groups:
        s = jnp.concatenate([add_per_head(sw[g][0:qn], win_lo), sw[g][qn:WINDOW],
                             add_per_head(sw[g][WINDOW:wlen], win_hi)], axis=0)
        pw = jnp.exp2(s - _colmax(s))
        rw = _dot(vwt_ref[0, g, :, pl.ds(q0, wlen)], pw.astype(BF16))
        ow.append(rw[0:NSA_DH] / rw[NSA_DH:NSA_DH + 1])

    kpos = q0 + lax.broadcasted_iota(jnp.int32, (qn, 1), 0)
    diag_bias = jnp.where(kpos <= t_q, 0.0, NEG)
    init = []
    for g in groups:
        s = add_per_head(sd[g], diag_bias)
        m0 = _colmax(s)
        init += [m0, _dot(vst_ref[0, g, :, pl.ds(q0, qn)], jnp.exp2(s - m0).astype(BF16))]

    gt_scr[...] = jax.nn.sigmoid(ng_ref[0].astype(F32)).T

    def gate_row(branch, g):
        return jnp.concatenate(
            [gt_scr[pl.ds(branch * NSA_HEADS + (g_first + g) * grp + r, 1), :] for r in range(grp)], axis=1)

    out_cw = [gate_row(0, g) * oc[g] + gate_row(2, g) * ow[g] for g in groups]
    gate_sel = [gate_row(1, g) for g in groups]

    jidx = lax.broadcasted_iota(jnp.int32, (n_sel, qn), 0)
    tq = q0 + lax.broadcasted_iota(jnp.int32, (n_sel, qn), 1)
    cur = tq // SEL_BLOCK
    forced = (jidx == 0) | (jidx == cur) | (jidx == cur - 1)
    causal_blk = jidx * SEL_BLOCK <= tq
    rows8 = 8
    jrow = lax.broadcasted_iota(jnp.int32, (rows8, qn), 0)
    for g in groups:
        val = jnp.where(forced, -2.0, imp[g])
        val = jnp.where(causal_blk, val, -1.0)
        parts = [val[v * rows8:(v + 1) * rows8] for v in range(n_sel // rows8)]
        counts = [jnp.zeros((rows8, qn), F32) for _ in parts]
        for jp in range(n_sel):
            other = jnp.broadcast_to(val[jp:jp + 1, :], (rows8, qn))
            for v, part in enumerate(parts):
                ge = jnp.where(other >= part, 1.0, 0.0)
                gt = jnp.where(other > part, 1.0, 0.0)
                if v * rows8 > jp:
                    before = ge
                elif (v + 1) * rows8 <= jp:
                    before = gt
                else:
                    before = jnp.where(jrow > jp - v * rows8, ge, gt)
                counts[v] = counts[v] + before
        rank = jnp.concatenate(counts, axis=0)
        chosen = forced | (rank < float(max(top - 3, 0)))
        selb = jnp.where(chosen & causal_blk, 0.0, NEG)
        selb_t = jnp.concatenate([selb, jnp.zeros((LANES - n_sel, qn), F32)], axis=0).T
        selb_t = selb_t[:, 0:NSA_DH].astype(BF16)
        qaug_scr[g] = jnp.concatenate([qrs[g], jnp.concatenate([selb_t] * grp, axis=0)], axis=1)

    kt_len = ATT_KT
    sub = kt_len // ATT_SUBTILES

    def sweep_block(k_base, n_sub, carry):
        state = list(carry)
        items = [(j, g) for j in range(n_sub) for g in groups]
        k0s = [pl.multiple_of(k_base + j * sub, sub) for j in range(n_sub)]
        score = lambda j, g: _dot_nt(ks_ref[0, g, pl.ds(k0s[j], sub), :], qaug_scr[g])
        ahead = min(ATT_LOOKAHEAD * ATT_GROUPS, len(items))
        scores = [score(*it) for it in items[:ahead]]
        for n, (j, g) in enumerate(items):
            s = scores[n]
            m_run, acc = state[2 * g], state[2 * g + 1]
            m_new = jnp.maximum(m_run, _colmax(s))
            alpha = jnp.exp2(m_run - m_new)
            pt = jnp.exp2(s - m_new).astype(BF16)
            state[2 * g + 1] = alpha * acc + _dot(vst_ref[0, g, :, pl.ds(k0s[j], sub)], pt)
            state[2 * g] = m_new
            if n + ahead < len(items):
                scores.append(score(*items[n + ahead]))
        return tuple(state)

    n_whole = q0 // kt_len
    carry = lax.fori_loop(0, n_whole, lambda kt, cy: sweep_block(kt * kt_len, ATT_SUBTILES, cy), tuple(init))
    tails = [functools.partial(sweep_block, n_whole * kt_len, r) for r in range(ATT_SUBTILES)]
    final = lax.switch((q0 - n_whole * kt_len) // sub, tails, carry)

    for g in groups:
        acc = final[2 * g + 1]
        out_t = out_cw[g] + gate_sel[g] * (acc[0:NSA_DH] / acc[NSA_DH:NSA_DH + 1])
        pieces = []
        for r2 in range(grp // 2):
            pair = jnp.concatenate([out_t[:, (2 * r2) * qn:(2 * r2 + 1) * qn],
                                    out_t[:, (2 * r2 + 1) * qn:(2 * r2 + 2) * qn]], axis=0)
            pieces.append(pair.T)
        o_ref[0, :, g * gw:(g + 1) * gw] = jnp.concatenate(pieces, axis=1).astype(o_ref.dtype)


def _attention(proj, q_rot, k_cmp, v_cmp_t, ks, vs_t, kw, vw_t, ov_t, bsz, seq):
    qn = ATT_Q
    gw = ATT_GROUPS * NSA_GROUP * NSA_DH
    n_slab = k_cmp.shape[2]
    n_sel = seq // SEL_BLOCK
    qspec = pl.BlockSpec((1, qn, gw), lambda b, g, i: (b, i, g))
    per_group = lambda a: pl.BlockSpec((1, ATT_GROUPS) + a.shape[2:], lambda b, g, i: (b, g, 0, 0))
    return pl.pallas_call(
        functools.partial(_attn_kernel, seq=seq),
        grid=(bsz, NSA_KV // ATT_GROUPS, seq // qn),
        in_specs=[pl.BlockSpec((1, qn, gw), lambda b, g, i: (b, i, P_NQ // gw + g)), qspec,
                  per_group(k_cmp), per_group(v_cmp_t), per_group(ks), per_group(vs_t),
                  per_group(kw), per_group(vw_t),
                  pl.BlockSpec((1, qn, LANES), lambda b, g, i: (b, i, P_NG // LANES)),
                  pl.BlockSpec((n_sel, n_slab), lambda b, g, i: (0, 0))],
        out_specs=qspec,
        out_shape=jax.ShapeDtypeStruct((bsz, seq, NSA_WIDTH), BF16),
        scratch_shapes=[pltpu.VMEM((LANES, qn), F32),
                        pltpu.VMEM((ATT_GROUPS, NSA_GROUP * qn, 2 * NSA_DH), BF16)],
        compiler_params=_cparams(("parallel", "parallel", "arbitrary")),
    )(proj, q_rot, k_cmp, v_cmp_t, ks, vs_t, kw, vw_t, proj, ov_t)


def _merge_kernel(x_ref, ya_ref, yb_ref, ga_ref, gb_ref, wa_ref, wb_ref, wo_ref, gn_ref, o_ref):
    ga = jax.nn.sigmoid(ga_ref[...].astype(F32))
    gb = jax.nn.sigmoid(gb_ref[...].astype(F32))
    merged = ga * _dot(ya_ref[...], wa_ref[...]) + gb * _dot(yb_ref[...], wb_ref[...])
    mo = _dot(merged.astype(BF16), wo_ref[...])
    o_ref[...] = x_ref[...] + _rms(mo, gn_ref[...])


def _merge(x, ya, yb, gates, wa, wb, wo, gn):
    n, d = x.shape
    tm = min(TOK_TILE, n)
    row = lambda c: pl.BlockSpec((tm, d), lambda i: (i, c))
    wsp = pl.BlockSpec((d, d), lambda i: (0, 0))
    return pl.pallas_call(
        _merge_kernel,
        grid=(n // tm,),
        in_specs=[row(0), row(0), row(0), row(P_GA // d), row(P_GB // d), wsp, wsp, wsp,
                  pl.BlockSpec((1, d), lambda i: (0, 0))],
        out_specs=row(0),
        out_shape=jax.ShapeDtypeStruct((n, d), F32),
        compiler_params=_cparams(("parallel",)),
    )(x, ya, yb, gates, gates, wa, wb, wo, gn)


def _mlp_kernel(h_ref, gpre_ref, wu_ref, wd_ref, gpost_ref, o_ref, v_scr, acc_scr):
    j = pl.program_id(1)

    @pl.when(j == 0)
    def _():
        v_scr[...] = _rms(h_ref[...], gpre_ref[...]).astype(BF16)
        acc_scr[...] = jnp.zeros_like(acc_scr)

    up = _dot(v_scr[...], wu_ref[...])
    act = jnp.square(jnp.maximum(up, 0.0))
    acc_scr[...] += _dot(act.astype(BF16), wd_ref[...])

    @pl.when(j == pl.num_programs(1) - 1)
    def _():
        o_ref[...] = h_ref[...] + _rms(acc_scr[...], gpost_ref[...])


def _mlp(h, gpre, wu, wd, gpost):
    n, d = h.shape
    dff = wu.shape[1]
    tm = min(2 * TOK_TILE, n)
    tf = 1024
    vec = pl.BlockSpec((1, d), lambda i, j: (0, 0))
    return pl.pallas_call(
        _mlp_kernel,
        grid=(n // tm, dff // tf),
        in_specs=[pl.BlockSpec((tm, d), lambda i, j: (i, 0)), vec,
                  pl.BlockSpec((d, tf), lambda i, j: (0, j)),
                  pl.BlockSpec((tf, d), lambda i, j: (j, 0)), vec],
        out_specs=pl.BlockSpec((tm, d), lambda i, j: (i, 0)),
        out_shape=jax.ShapeDtypeStruct((n, d), F32),
        scratch_shapes=[pltpu.VMEM((tm, d), BF16), pltpu.VMEM((tm, d), F32)],
        compiler_params=_cparams(("parallel", "arbitrary")),
    )(h, gpre, wu, wd, gpost)


def _ple_kernel(h_ref, p_ref, wp_ref, wg_ref, gn_ref, o_ref):
    h = h_ref[...]
    e = _dot(p_ref[...].astype(BF16), wp_ref[...]) * jax.nn.sigmoid(_dot(h.astype(BF16), wg_ref[...]))
    o_ref[...] = h + _rms(e, gn_ref[...])


def _ple(h, p, wp, wg, gn):
    n, d = h.shape
    pd = p.shape[1]
    tm = min(TOK_TILE, n)
    return pl.pallas_call(
        _ple_kernel,
        grid=(n // tm,),
        in_specs=[pl.BlockSpec((tm, d), lambda i: (i, 0)),
                  pl.BlockSpec((tm, pd), lambda i: (i, 0)),
                  pl.BlockSpec((pd, d), lambda i: (0, 0)),
                  pl.BlockSpec((d, d), lambda i: (0, 0)),
                  pl.BlockSpec((1, d), lambda i: (0, 0))],
        out_specs=pl.BlockSpec((tm, d), lambda i: (i, 0)),
        out_shape=jax.ShapeDtypeStruct((n, d), F32),
        compiler_params=_cparams(("parallel",)),
    )(h, p, wp, wg, gn)


def _overlap_t(seq):
    n_slab = seq // CMP_STRIDE
    n_cmp = (seq - CMP_LEN) // CMP_STRIDE + 1
    n_sel = seq // SEL_BLOCK
    start = np.arange(n_slab) * CMP_STRIDE
    end = start + CMP_LEN - 1
    sel_start = np.arange(n_sel) * SEL_BLOCK
    ov = (start[None, :] < sel_start[:, None] + SEL_BLOCK) & (end[None, :] >= sel_start[:, None])
    ov = ov & (np.arange(n_slab)[None, :] < n_cmp)
    return jnp.asarray(ov.astype(np.float32), BF16)


def kernel(x, p, w_in, w_branch_a, w_branch_b, w_out, norm_pre_mix, norm_post_mix, norm_pre_mlp,
           norm_post_mlp, hg_lb_logits, hg_gnorm, cmp_pe_k, cmp_pe_v, cmp_wk1, cmp_wk2, cmp_wv1, cmp_wv2,
           w_up, w_down, w_ple, w_ple_gate, norm_ple):
    bsz, seq, d = x.shape
    depth = w_in.shape[0]
    assert depth == 1 and hg_lb_logits.shape[0] == 2 and d == D_MODEL
    assert seq % ATT_KT == 0 and seq >= WINDOW + ATT_Q and seq // SEL_BLOCK <= NSA_DH
    assert seq % HG_TOK == 0 and (HG_TOK // HG_CHUNK) % HG_GROUP == 0
    assert ATT_Q in (SEL_BLOCK, 2 * SEL_BLOCK)
    assert ATT_Q % (ATT_KT // ATT_SUBTILES) == 0
    n = bsz * seq
    x2 = x.reshape(n, d)
    row = lambda a: a.reshape(1, -1)

    w = w_in[0]
    s_hq, s_hf, s_hi, s_hg = 0, HG_WIDTH, 2 * HG_WIDTH, 3 * HG_WIDTH
    s_nq = 4 * HG_WIDTH
    s_ng = s_nq + NSA_WIDTH + 6 * NSA_KVW
    s_ga = s_ng + 3 * NSA_HEADS
    col = lambda a, b: w[:, a:b]
    w_all = jnp.concatenate(
        [col(s_hq, s_hf), col(s_hi, s_hg), col(s_hg, s_nq), col(s_hf, s_hi), col(s_nq, s_ga),
         jnp.zeros((d, P_GA - P_NG - 3 * NSA_HEADS), F32), col(s_ga, s_ga + 2 * D_MODEL)], axis=1).astype(BF16)
    assert w_all.shape[1] == P_WIDTH
    proj2 = _rms_matmul(x2, row(norm_pre_mix[0]), w_all, BF16, P_WIDTH // 4)
    proj = proj2.reshape(bsz, seq, P_WIDTH)

    y_a = _hgrn(proj, hg_lb_logits, row(hg_gnorm[0]), bsz, seq)

    tables = _rope_tables(seq)
    kvb = P_KV // NSA_KVW
    q_rot = _rope(proj, P_NQ // NSA_WIDTH, NSA_WIDTH, tables, ATT_SCALE * LOG2E, bsz, seq)
    ks_aug, vs_t = _kv_prep(proj, kvb + 2, kvb + 3, tables, 0, True, bsz, seq)
    kw_aug, vw_t = _kv_prep(proj, kvb + 4, kvb + 5, tables, WINDOW, False, bsz, seq)
    kv_slice = lambda i: proj[:, :, P_KV + i * NSA_KVW: P_KV + (i + 1) * NSA_KVW]
    heads = lambda t: t.reshape(bsz, seq, NSA_KV, NSA_DH).transpose(0, 2, 1, 3)
    slabs = lambda t: heads(t).reshape(bsz, NSA_KV, seq // CMP_STRIDE, CMP_STRIDE * NSA_DH)
    k_cmp, v_cmp = _compress(
        slabs(kv_slice(0)), slabs(kv_slice(1)), cmp_pe_k[0].reshape(1, -1), cmp_pe_v[0].reshape(1, -1),
        cmp_wk1[0].astype(BF16), cmp_wk2[0].astype(BF16), cmp_wv1[0].astype(BF16), cmp_wv2[0].astype(BF16))
    y_b = _attention(proj, q_rot, k_cmp, v_cmp.transpose(0, 1, 3, 2), ks_aug, vs_t, kw_aug, vw_t,
                     _overlap_t(seq), bsz, seq)

    h1 = _merge(x2, y_a.reshape(n, -1), y_b.reshape(n, -1), proj2, w_branch_a[0].astype(BF16),
                w_branch_b[0].astype(BF16), w_out[0].astype(BF16), row(norm_post_mix[0]))
    h2 = _mlp(h1, row(norm_pre_mlp[0]), w_up[0].astype(BF16), w_down[0].astype(BF16), row(norm_post_mlp[0]))
    h3o = _ple(h2, p[0].reshape(n, -1), w_ple[0].astype(BF16), w_ple_gate[0].astype(BF16), row(norm_ple[0]))
    return h3o.reshape(bsz, seq, d)
```

```python
import functools

import numpy as np
import jax
import jax.numpy as jnp
from jax import lax
from jax.experimental import pallas as pl
from jax.experimental.pallas import tpu as pltpu

F32 = jnp.float32
BF16 = jnp.bfloat16

D_MODEL = 1024
HG_HEADS = 8
HG_DK = 128
HG_DV = 128
HG_WIDTH = HG_HEADS * HG_DK
HG_CHUNK = 64
NSA_HEADS = 16
NSA_KV = 4
NSA_GROUP = NSA_HEADS // NSA_KV
NSA_DH = 64
NSA_WIDTH = NSA_HEADS * NSA_DH
NSA_KVW = NSA_KV * NSA_DH
CMP_LEN = 32
CMP_STRIDE = 16
CMP_HIDDEN = 256
SEL_BLOCK = 64
SEL_TOPK = 16
WINDOW = 512
ROPE_THETA = 500000.0
ROPE_DIM = NSA_DH // 4
D_FF = 4 * D_MODEL
EPS = 1e-6
NEG = -1e30
ATT_SCALE = NSA_DH ** -0.5
LOG2E = float(np.log2(np.e))

LANES = 128
SUBLANES_BF16 = 16
VMEM_LIMIT = 56 * 1024 * 1024

TOK_TILE = 512
ATT_Q = 128
ATT_GROUPS = 4
ATT_KT = 1024
ATT_SUBTILES = 8
ATT_LOOKAHEAD = 7
HG_TOK = 1024
HG_GROUP = 16

P_GA, P_GB = 0, D_MODEL
P_HQ, P_HI, P_HG, P_HF = (2 * D_MODEL + i * HG_WIDTH for i in range(4))
P_NQ = P_HF + HG_WIDTH
P_KV = P_NQ + NSA_WIDTH
P_NG = P_KV + 6 * NSA_KVW
P_WIDTH = P_NG + LANES
P_COL_TILES = 3
HG_EXP2_LIMIT = 115.0


def _cparams(sem):
    return pltpu.CompilerParams(dimension_semantics=sem, vmem_limit_bytes=VMEM_LIMIT)


def _rms(x, gain):
    ms = jnp.mean(x * x, axis=-1, keepdims=True)
    return (x * lax.rsqrt(ms + EPS)) * gain


def _silu(x):
    return x * (0.5 * jnp.tanh(0.5 * x) + 0.5)


def _dot(a, b):
    return jnp.dot(a, b, preferred_element_type=F32)


def _dot_nt(a, b):
    return lax.dot_general(a, b, (((1,), (1,)), ((), ())), preferred_element_type=F32)


def _dot_tn(a, b):
    return lax.dot_general(a, b, (((0,), (0,)), ((), ())), preferred_element_type=F32)


def _colmax(s):
    parts = [s[i:i + 64] for i in range(0, s.shape[0], 64)] if s.shape[0] % 64 == 0 else [s]
    while len(parts) > 1:
        parts = [jnp.maximum(parts[i], parts[i + 1]) if i + 1 < len(parts) else parts[i]
                 for i in range(0, len(parts), 2)]
    return jnp.max(parts[0], axis=0, keepdims=True)


def _split3(x):
    hi = x.astype(BF16)
    r1 = x - hi.astype(F32)
    mid = r1.astype(BF16)
    lo = (r1 - mid.astype(F32)).astype(BF16)
    return hi, mid, lo


def _rms_matmul_kernel(x_ref, g_ref, w_ref, o_ref, u_scr):
    @pl.when(pl.program_id(1) == 0)
    def _():
        u_scr[...] = _rms(x_ref[...], g_ref[...]).astype(BF16)

    o_ref[...] = _dot(u_scr[...], w_ref[...]).astype(o_ref.dtype)


def _rms_matmul(x, gain, w, out_dtype, tn):
    n, d = x.shape
    wd = w.shape[1]
    tm = min(2 * TOK_TILE, n)
    return pl.pallas_call(
        _rms_matmul_kernel,
        grid=(n // tm, wd // tn),
        in_specs=[pl.BlockSpec((tm, d), lambda i, j: (i, 0)),
                  pl.BlockSpec((1, d), lambda i, j: (0, 0)),
                  pl.BlockSpec((d, tn), lambda i, j: (0, j))],
        out_specs=pl.BlockSpec((tm, tn), lambda i, j: (i, j)),
        out_shape=jax.ShapeDtypeStruct((n, wd), out_dtype),
        scratch_shapes=[pltpu.VMEM((tm, d), BF16)],
        compiler_params=_cparams(("parallel", "arbitrary")),
    )(x, gain, w)


def _hgrn_levels():
    return [HG_CHUNK >> (i + 1) for i in range(HG_CHUNK.bit_length() - 1)]


def _hgrn_intra_exact(qf, kk, b, gsel_ref):
    c_len = HG_CHUNK
    row = lax.broadcasted_iota(jnp.int32, (c_len, c_len), 0)
    col = lax.broadcasted_iota(jnp.int32, (c_len, c_len), 1)
    trow = lax.broadcasted_iota(jnp.int32, (c_len, HG_DK), 0)
    a = jnp.where(row == col, _dot_nt(qf.astype(BF16), kk.astype(BF16)), 0.0)
    b3 = jnp.concatenate(_split3(b), axis=0)
    for lvl, h in enumerate(_hgrn_levels()):
        b_at_ref = _dot(gsel_ref[lvl], b3)
        upper = (trow & (2 * h - 1)) >= h
        qe = jnp.where(upper, qf * jnp.exp2(jnp.minimum(b - b_at_ref, 0.0)), 0.0)
        ke = jnp.where(upper, 0.0, kk * jnp.exp2(jnp.minimum(b_at_ref - b, 0.0)))
        same_block = (row & -(2 * h)) == (col & -(2 * h))
        block = same_block & ((row & (2 * h - 1)) >= h) & ((col & (2 * h - 1)) < h)
        a = a + jnp.where(block, _dot_nt(qe.astype(BF16), ke.astype(BF16)), 0.0)
    return a


def _hgrn_kernel(q_ref, i_ref, g_ref, f_ref, lbl_ref, gn_ref, tri_ref, gsel_ref, o_ref, st_ref, st0_scr, *,
                 n_chunks):
    c_len = HG_CHUNK
    half = c_len // 2

    @pl.when(pl.program_id(2) == 0)
    def _():
        st_ref[...] = jnp.zeros_like(st_ref)

    lg = lbl_ref[...]
    mx = jnp.max(lg, axis=0, keepdims=True)
    ex = jnp.exp(lg - mx)
    lb = ex[0:1, :] / jnp.sum(ex, axis=0, keepdims=True)

    row = lax.broadcasted_iota(jnp.int32, (c_len, c_len), 0)
    col = lax.broadcasted_iota(jnp.int32, (c_len, c_len), 1)
    causal = col <= row
    tri3 = tri_ref[...]

    def run_group(first, state, exact):
        chunks = [slice((first + c) * c_len, (first + c + 1) * c_len) for c in range(HG_GROUP)]
        qf, kk, lf = [], [], []
        for rows in chunks:
            qx = q_ref[0, rows, :].astype(F32)
            qf.append(_silu(qx) * (HG_DK ** -0.5))
            forget = lb + (1.0 - lb) * jax.nn.sigmoid(f_ref[0, rows, :].astype(F32))
            kk.append(1.0 - forget)
            lf.append(jnp.log2(forget))
        bb = [_dot(tri3, jnp.concatenate(_split3(x), axis=0)) for x in lf]
        qe, ke, qb, kd, dlast = [], [], [], [], []
        need = jnp.zeros((1, HG_DK), F32)
        for c in range(HG_GROUP):
            b = bb[c]
            b_mid = b[half - 1:half, :]
            b_last = b[c_len - 1:c_len, :]
            if not exact:
                qe.append((qf[c] * jnp.exp2(b - b_mid)).astype(BF16))
                ke.append((kk[c] * jnp.exp2(b_mid - b)).astype(BF16))
                need = jnp.maximum(need, jnp.maximum(b[0:1, :] - b_mid, b_mid - b_last))
            qb.append((qf[c] * jnp.exp2(b)).astype(BF16))
            kd.append((kk[c] * jnp.exp2(b_last - b)).astype(BF16))
            dlast.append(jnp.exp2(b_last))
        vv = [i_ref[0, rows, :] for rows in chunks]
        if exact:
            aa = [_hgrn_intra_exact(qf[c], kk[c], bb[c], gsel_ref) for c in range(HG_GROUP)]
        else:
            aa = [_dot_nt(qe[c], ke[c]) for c in range(HG_GROUP)]
        kv = [_dot_tn(vv[c], kd[c]) for c in range(HG_GROUP)]
        o_intra = [_dot(jnp.where(causal, aa[c], 0.0).astype(BF16), vv[c]) for c in range(HG_GROUP)]
        states = [state]
        for c in range(HG_GROUP):
            states.append(states[c] * dlast[c] + kv[c])
        o_inter = [_dot_nt(qb[c], states[c].astype(BF16)) for c in range(HG_GROUP)]
        for c, rows in enumerate(chunks):
            gx = g_ref[0, rows, :].astype(F32)
            y = _rms(o_intra[c] + o_inter[c], gn_ref[...]) * _silu(gx)
            o_ref[0, rows, :] = y.astype(o_ref.dtype)
        return states[HG_GROUP], need

    def run_all(exact):
        state = st0_scr[...]
        need = jnp.zeros((1, HG_DK), F32)
        for first in range(0, n_chunks, HG_GROUP):
            state, need_g = run_group(first, state, exact)
            need = jnp.maximum(need, need_g)
        return state, need

    st0_scr[...] = st_ref[...]
    state, need = run_all(exact=False)
    st_ref[...] = state

    @pl.when(jnp.max(need) > HG_EXP2_LIMIT)
    def _():
        run_all(exact=True)


def _hgrn(proj, lb_logits, gnorm, bsz, seq):
    n_chunks = HG_TOK // HG_CHUNK
    tri = np.tril(np.ones((HG_CHUNK, HG_CHUNK), np.float32))
    tri3 = jnp.asarray(np.concatenate([tri, tri, tri], axis=1), BF16)
    t_idx = np.arange(HG_CHUNK)
    gsel = np.zeros((len(_hgrn_levels()), HG_CHUNK, HG_CHUNK), np.float32)
    for lvl, h in enumerate(_hgrn_levels()):
        gsel[lvl, t_idx, (t_idx // (2 * h)) * (2 * h) + h - 1] = 1.0
    gsel3 = jnp.asarray(np.concatenate([gsel, gsel, gsel], axis=2), BF16)
    hh = HG_HEADS
    blk = (1, HG_TOK, HG_DK)
    head_cols = lambda off: pl.BlockSpec(blk, lambda b, h, t: (b, t, off // HG_DK + h))
    return pl.pallas_call(
        functools.partial(_hgrn_kernel, n_chunks=n_chunks),
        grid=(bsz, hh, seq // HG_TOK),
        in_specs=[head_cols(P_HQ), head_cols(P_HI), head_cols(P_HG), head_cols(P_HF),
                  pl.BlockSpec((2, HG_DK), lambda b, h, t: (0, h)),
                  pl.BlockSpec((1, HG_DV), lambda b, h, t: (0, 0)),
                  pl.BlockSpec((HG_CHUNK, 3 * HG_CHUNK), lambda b, h, t: (0, 0)),
                  pl.BlockSpec(gsel3.shape, lambda b, h, t: (0, 0, 0))],
        out_specs=pl.BlockSpec(blk, lambda b, h, t: (b, t, h)),
        out_shape=jax.ShapeDtypeStruct((bsz, seq, HG_WIDTH), BF16),
        scratch_shapes=[pltpu.VMEM((HG_DV, HG_DK), F32), pltpu.VMEM((HG_DV, HG_DK), F32)],
        compiler_params=_cparams(("parallel", "parallel", "arbitrary")),
    )(proj, proj, proj, proj, lb_logits, gnorm, tri3, gsel3)


def _rope_kernel(x_ref, c_ref, s1_ref, s2_ref, o_ref, *, scale):
    width = x_ref.shape[-1]
    c = c_ref[...]
    s1 = s1_ref[...]
    s2 = s2_ref[...]
    for j in range(width // LANES):
        cols = slice(j * LANES, (j + 1) * LANES)
        x = x_ref[0, :, cols].astype(F32)
        if scale != 1.0:
            x = x * scale
        o_ref[0, :, cols] = _rope_lanes(x, c, s1, s2).astype(o_ref.dtype)


def _rope(src, col_block, width, tables, scale, bsz, seq):
    tt = min(TOK_TILE, seq)
    tab = pl.BlockSpec((tt, LANES), lambda b, t: (t, 0))
    return pl.pallas_call(
        functools.partial(_rope_kernel, scale=scale),
        grid=(bsz, seq // tt),
        in_specs=[pl.BlockSpec((1, tt, width), lambda b, t: (b, t, col_block)), tab, tab, tab],
        out_specs=pl.BlockSpec((1, tt, width), lambda b, t: (b, t, 0)),
        out_shape=jax.ShapeDtypeStruct((bsz, seq, width), BF16),
        compiler_params=_cparams(("parallel", "parallel")),
    )(src, *tables)


def _rope_lanes(x, c, s1, s2):
    return x * c + pltpu.roll(x, LANES - ROPE_DIM // 2, 1) * s1 + pltpu.roll(x, ROPE_DIM // 2, 1) * s2


def _kv_prep_kernel(k_ref, v_ref, c_ref, s1_ref, s2_ref, ko_ref, vo_ref, *, pad_blocks, with_block_id):
    tt = k_ref.shape[1]
    step = pl.program_id(1)
    is_pad = step < pad_blocks
    pos = (step - pad_blocks) * tt + lax.broadcasted_iota(jnp.int32, (tt, NSA_DH), 0)
    lane = lax.broadcasted_iota(jnp.int32, (tt, NSA_DH), 1)
    if with_block_id:
        extra = jnp.where(pos // SEL_BLOCK == lane, 1.0, 0.0)
    else:
        extra = jnp.where((lane == 0) & is_pad, 1.0, 0.0)
    c, s1, s2 = c_ref[...], s1_ref[...], s2_ref[...]
    vt = v_ref[0].astype(F32).T
    for j in range(NSA_KVW // LANES):
        y = _rope_lanes(k_ref[0, :, j * LANES:(j + 1) * LANES].astype(F32), c, s1, s2)
        y = jnp.where(is_pad, 0.0, y)
        for g2 in range(LANES // NSA_DH):
            g = j * (LANES // NSA_DH) + g2
            ko_ref[0, g] = jnp.concatenate([y[:, g2 * NSA_DH:(g2 + 1) * NSA_DH], extra], axis=1).astype(BF16)
    for g in range(NSA_KV):
        vo_ref[0, g, 0:NSA_DH, :] = jnp.where(is_pad, 0.0, vt[g * NSA_DH:(g + 1) * NSA_DH]).astype(BF16)
        vo_ref[0, g, NSA_DH:NSA_DH + SUBLANES_BF16, :] = jnp.ones((SUBLANES_BF16, tt), BF16)


def _kv_prep(proj, k_block, v_block, tables, pad_rows, with_block_id, bsz, seq):
    tt = min(TOK_TILE, seq)
    pad_blocks = pad_rows // tt
    assert pad_blocks * tt == pad_rows
    src_t = lambda t: jnp.maximum(t - pad_blocks, 0)
    tab = pl.BlockSpec((tt, LANES), lambda b, t: (src_t(t), 0))
    rows = seq + pad_rows
    return pl.pallas_call(
        functools.partial(_kv_prep_kernel, pad_blocks=pad_blocks, with_block_id=with_block_id),
        grid=(bsz, rows // tt),
        in_specs=[pl.BlockSpec((1, tt, NSA_KVW), lambda b, t: (b, src_t(t), k_block)),
                  pl.BlockSpec((1, tt, NSA_KVW), lambda b, t: (b, src_t(t), v_block)), tab, tab, tab],
        out_specs=[pl.BlockSpec((1, NSA_KV, tt, 2 * NSA_DH), lambda b, t: (b, 0, t, 0)),
                   pl.BlockSpec((1, NSA_KV, NSA_DH + SUBLANES_BF16, tt), lambda b, t: (b, 0, 0, t))],
        out_shape=[jax.ShapeDtypeStruct((bsz, NSA_KV, rows, 2 * NSA_DH), BF16),
                   jax.ShapeDtypeStruct((bsz, NSA_KV, NSA_DH + SUBLANES_BF16, rows), BF16)],
        compiler_params=_cparams(("parallel", "parallel")),
    )(proj, proj, *tables)


def _rope_tables(seq):
    half = ROPE_DIM // 2
    inv = jnp.asarray(ROPE_THETA ** (-np.arange(half) * 2.0 / ROPE_DIM), F32)
    ang = jnp.arange(seq, dtype=F32)[:, None] * inv[None, :]
    cos, sin = jnp.cos(ang), jnp.sin(ang)
    lane = np.arange(LANES) % NSA_DH
    pick = lane % half
    cos_l = cos[:, pick]
    sin_l = sin[:, pick]
    first = jnp.asarray(lane < half)
    second = jnp.asarray((lane >= half) & (lane < ROPE_DIM))
    c = jnp.where(first | second, cos_l, 1.0)
    s1 = jnp.where(first, -sin_l, 0.0)
    s2 = jnp.where(second, sin_l, 0.0)
    return c, s1, s2


def _gelu_tanh(x):
    return 0.5 * x * (1.0 + jnp.tanh(np.sqrt(2.0 / np.pi).astype(np.float32) * (x + 0.044715 * (x * x * x))))


def _compress_kernel(xk_ref, xv_ref, pek_ref, pev_ref, wk1_ref, wk2_ref, wv1_ref, wv2_ref, ok_ref, ov_ref):
    half_w = CMP_STRIDE * NSA_DH

    def one(x_ref, pe_ref, w1_ref, w2_ref):
        x = x_ref[0, 0]
        n_slab = x.shape[0]
        p0 = _dot(x, w1_ref[0:half_w, :])
        p1 = _dot(x, w1_ref[half_w:2 * half_w, :])
        pe = jnp.broadcast_to(pe_ref[...], (8, 2 * half_w)).astype(BF16)
        pw = _dot(pe, w1_ref[...])[0:1, :]
        h = p0 + pltpu.roll(p1, n_slab - 1, 0) + pw
        return _dot(_gelu_tanh(h).astype(BF16), w2_ref[...])

    ok_ref[0, 0] = (one(xk_ref, pek_ref, wk1_ref, wk2_ref) * (ATT_SCALE * LOG2E)).astype(ok_ref.dtype)
    ov_ref[0, 0] = one(xv_ref, pev_ref, wv1_ref, wv2_ref).astype(ov_ref.dtype)


def _compress(xk, xv, pek, pev, wk1, wk2, wv1, wv2):
    bsz, grp, n_slab, wdt = xk.shape
    xs = pl.BlockSpec((1, 1, n_slab, wdt), lambda b, g: (b, g, 0, 0))
    full = lambda a: pl.BlockSpec(a.shape, lambda b, g: (0,) * a.ndim)
    osd = jax.ShapeDtypeStruct((bsz, grp, n_slab, NSA_DH), BF16)
    os_ = pl.BlockSpec((1, 1, n_slab, NSA_DH), lambda b, g: (b, g, 0, 0))
    return pl.pallas_call(
        _compress_kernel,
        grid=(bsz, grp),
        in_specs=[xs, xs, full(pek), full(pev), full(wk1), full(wk2), full(wv1), full(wv2)],
        out_specs=[os_, os_],
        out_shape=[osd, osd],
        compiler_params=_cparams(("parallel", "parallel")),
    )(xk, xv, pek, pev, wk1, wk2, wv1, wv2)


def _attn_kernel(nq_ref, qr_ref, kcmp_ref, vcmpt_ref, ks_ref, vst_ref, kw_ref, vwt_ref, ng_ref, ovt_ref,
                 o_ref, gt_scr, qaug_scr, *, seq):
    qn = ATT_Q
    grp = NSA_GROUP
    ln = grp * qn
    gw = grp * NSA_DH
    n_sel = seq // SEL_BLOCK
    top = min(SEL_TOPK, n_sel)
    groups = range(ATT_GROUPS)
    g_first = pl.program_id(1) * ATT_GROUPS
    q0 = pl.multiple_of(pl.program_id(2) * qn, qn)

    def stack_heads(x):
        return jnp.concatenate([x[:, r * NSA_DH:(r + 1) * NSA_DH] for r in range(grp)], axis=0)

    def add_per_head(s, bias):
        return jnp.concatenate([s[:, r * qn:(r + 1) * qn] + bias for r in range(grp)], axis=1)

    t_lane = q0 + (lax.broadcasted_iota(jnp.int32, (1, ln), 1) & (qn - 1))
    t_q = q0 + lax.broadcasted_iota(jnp.int32, (1, qn), 1)

    n_slab = kcmp_ref.shape[2]
    wlen = WINDOW + qn
    pad_col = jnp.where(lax.broadcasted_iota(jnp.int32, (ln, NSA_DH), 1) == 0, NEG, 0.0).astype(BF16)
    qs = [stack_heads(nq_ref[0, :, g * gw:(g + 1) * gw]) for g in groups]
    qrs = [stack_heads(qr_ref[0, :, g * gw:(g + 1) * gw]) for g in groups]
    sc = [_dot_nt(kcmp_ref[0, g], qs[g]) for g in groups]
    sw = [_dot_nt(kw_ref[0, g, pl.ds(q0, wlen), :], jnp.concatenate([qrs[g], pad_col], axis=1))
          for g in groups]
    sd = [_dot_nt(ks_ref[0, g, pl.ds(q0, qn), :][:, 0:NSA_DH], qrs[g]) for g in groups]

    cend = lax.broadcasted_iota(jnp.int32, (n_slab, 1), 0) * CMP_STRIDE + (CMP_LEN - 1)
    cmp_bias = jnp.where(cend <= t_q, 0.0, NEG)
    ovt = ovt_ref[...]
    oc, imp = [], []
    for g in groups:
        sm = add_per_head(sc[g], cmp_bias)
        e = jnp.exp2(sm - _colmax(sm))
        den = jnp.sum(e, axis=0, keepdims=True)
        p = e * jnp.where(t_lane >= CMP_LEN - 1, 1.0 / den, 0.0)
        oc.append(_dot(vcmpt_ref[0, g], p.astype(BF16)))
        psum = p[:, 0:qn]
        for r in range(1, grp):
            psum = psum + p[:, r * qn:(r + 1) * qn]
        hi, mid, lo = _split3(psum)
        imp.append(_dot(ovt, hi) + _dot(ovt, mid) + _dot(ovt, lo))

    i_loc = lax.broadcasted_iota(jnp.int32, (qn, 1), 0)
    j_loc = lax.broadcasted_iota(jnp.int32, (1, qn), 1)
    win_lo = jnp.where(i_loc > j_loc, 0.0, NEG)
    win_hi = jnp.where(i_loc <= j_loc, 0.0, NEG)
    ow = []
    for g in groups:
        s = jnp.concatenate([add_per_head(sw[g][0:qn], win_lo), sw[g][qn:WINDOW],
                             add_per_head(sw[g][WINDOW:wlen], win_hi)], axis=0)
        pw = jnp.exp2(s - _colmax(s))
        rw = _dot(vwt_ref[0, g, :, pl.ds(q0, wlen)], pw.astype(BF16))
        ow.append(rw[0:NSA_DH] / rw[NSA_DH:NSA_DH + 1])

    kpos = q0 + lax.broadcasted_iota(jnp.int32, (qn, 1), 0)
    diag_bias = jnp.where(kpos <= t_q, 0.0, NEG)
    init = []
    for g in groups:
        s = add_per_head(sd[g], diag_bias)
        m0 = _colmax(s)
        init += [m0, _dot(vst_ref[0, g, :, pl.ds(q0, qn)], jnp.exp2(s - m0).astype(BF16))]

    gt_scr[...] = jax.nn.sigmoid(ng_ref[0].astype(F32)).T

    def gate_row(branch, g):
        return jnp.concatenate(
            [gt_scr[pl.ds(branch * NSA_HEADS + (g_first + g) * grp + r, 1), :] for r in range(grp)], axis=1)

    out_cw = [gate_row(0, g) * oc[g] + gate_row(2, g) * ow[g] for g in groups]
    gate_sel = [gate_row(1, g) for g in groups]

    jidx = lax.broadcasted_iota(jnp.int32, (n_sel, qn), 0)
    tq = q0 + lax.broadcasted_iota(jnp.int32, (n_sel, qn), 1)
    cur = tq // SEL_BLOCK
    forced = (jidx == 0) | (jidx == cur) | (jidx == cur - 1)
    causal_blk = jidx * SEL_BLOCK <= tq
    rows8 = 8
    jrow = lax.broadcasted_iota(jnp.int32, (rows8, qn), 0)
    for g in groups:
        val = jnp.where(forced, -2.0, imp[g])
        val = jnp.where(causal_blk, val, -1.0)
        parts = [val[v * rows8:(v + 1) * rows8] for v in range(n_sel // rows8)]
        counts = [jnp.zeros((rows8, qn), F32) for _ in parts]
        for jp in range(n_sel):
            other = jnp.broadcast_to(val[jp:jp + 1, :], (rows8, qn))
            for v, part in enumerate(parts):
                ge = jnp.where(other >= part, 1.0, 0.0)
                gt = jnp.where(other > part, 1.0, 0.0)
                if v * rows8 > jp:
                    before = ge
                elif (v + 1) * rows8 <= jp:
                    before = gt
                else:
                    before = jnp.where(jrow > jp - v * rows8, ge, gt)
                counts[v] = counts[v] + before
        rank = jnp.concatenate(counts, axis=0)
        chosen = forced | (rank < float(max(top - 3, 0)))
        selb = jnp.where(chosen & causal_blk, 0.0, NEG)
        selb_t = jnp.concatenate([selb, jnp.zeros((LANES - n_sel, qn), F32)], axis=0).T
        selb_t = selb_t[:, 0:NSA_DH].astype(BF16)
        qaug_scr[g] = jnp.concatenate([qrs[g], jnp.concatenate([selb_t] * grp, axis=0)], axis=1)

    kt_len = ATT_KT
    sub = kt_len // ATT_SUBTILES

    def sweep_block(k_base, n_sub, carry):
        state = list(carry)
        items = [(j, g) for j in range(n_sub) for g in groups]
        k0s = [pl.multiple_of(k_base + j * sub, sub) for j in range(n_sub)]
        score = lambda j, g: _dot_nt(ks_ref[0, g, pl.ds(k0s[j], sub), :], qaug_scr[g])
        ahead = min(ATT_LOOKAHEAD * ATT_GROUPS, len(items))
        scores = [score(*it) for it in items[:ahead]]
        for n, (j, g) in enumerate(items):
            s = scores[n]
            m_run, acc = state[2 * g], state[2 * g + 1]
            m_new = jnp.maximum(m_run, _colmax(s))
            alpha = jnp.exp2(m_run - m_new)
            pt = jnp.exp2(s - m_new).astype(BF16)
            state[2 * g + 1] = alpha * acc + _dot(vst_ref[0, g, :, pl.ds(k0s[j], sub)], pt)
            state[2 * g] = m_new
            if n + ahead < len(items):
                scores.append(score(*items[n + ahead]))
        return tuple(state)

    n_whole = q0 // kt_len
    carry = lax.fori_loop(0, n_whole, lambda kt, cy: sweep_block(kt * kt_len, ATT_SUBTILES, cy), tuple(init))
    tails = [functools.partial(sweep_block, n_whole * kt_len, r) for r in range(ATT_SUBTILES)]
    final = lax.switch((q0 - n_whole * kt_len) // sub, tails, carry)

    for g in groups:
        acc = final[2 * g + 1]
        out_t = out_cw[g] + gate_sel[g] * (acc[0:NSA_DH] / acc[NSA_DH:NSA_DH + 1])
        pieces = []
        for r2 in range(grp // 2):
            pair = jnp.concatenate([out_t[:, (2 * r2) * qn:(2 * r2 + 1) * qn],
                                    out_t[:, (2 * r2 + 1) * qn:(2 * r2 + 2) * qn]], axis=0)
            pieces.append(pair.T)
        o_ref[0, :, g * gw:(g + 1) * gw] = jnp.concatenate(pieces, axis=1).astype(o_ref.dtype)


def _attention(proj, q_rot, k_cmp, v_cmp_t, ks, vs_t, kw, vw_t, ov_t, bsz, seq):
    qn = ATT_Q
    gw = ATT_GROUPS * NSA_GROUP * NSA_DH
    n_slab = k_cmp.shape[2]
    n_sel = seq // SEL_BLOCK
    qspec = pl.BlockSpec((1, qn, gw), lambda b, g, i: (b, i, g))
    per_group = lambda a: pl.BlockSpec((1, ATT_GROUPS) + a.shape[2:], lambda b, g, i: (b, g, 0, 0))
    return pl.pallas_call(
        functools.partial(_attn_kernel, seq=seq),
        grid=(bsz, NSA_KV // ATT_GROUPS, seq // qn),
        in_specs=[pl.BlockSpec((1, qn, gw), lambda b, g, i: (b, i, P_NQ // gw + g)), qspec,
                  per_group(k_cmp), per_group(v_cmp_t), per_group(ks), per_group(vs_t),
                  per_group(kw), per_group(vw_t),
                  pl.BlockSpec((1, qn, LANES), lambda b, g, i: (b, i, P_NG // LANES)),
                  pl.BlockSpec((n_sel, n_slab), lambda b, g, i: (0, 0))],
        out_specs=qspec,
        out_shape=jax.ShapeDtypeStruct((bsz, seq, NSA_WIDTH), BF16),
        scratch_shapes=[pltpu.VMEM((LANES, qn), F32),
                        pltpu.VMEM((ATT_GROUPS, NSA_GROUP * qn, 2 * NSA_DH), BF16)],
        compiler_params=_cparams(("parallel", "parallel", "arbitrary")),
    )(proj, q_rot, k_cmp, v_cmp_t, ks, vs_t, kw, vw_t, proj, ov_t)


def _merge_kernel(x_ref, ya_ref, yb_ref, ga_ref, gb_ref, wa_ref, wb_ref, wo_ref, gn_ref, o_ref):
    ga = jax.nn.sigmoid(ga_ref[...].astype(F32))
    gb = jax.nn.sigmoid(gb_ref[...].astype(F32))
    merged = ga * _dot(ya_ref[...], wa_ref[...]) + gb * _dot(yb_ref[...], wb_ref[...])
    mo = _dot(merged.astype(BF16), wo_ref[...])
    o_ref[...] = x_ref[...] + _rms(mo, gn_ref[...])


def _merge(x, ya, yb, gates, wa, wb, wo, gn):
    n, d = x.shape
    tm = min(TOK_TILE, n)
    row = lambda c: pl.BlockSpec((tm, d), lambda i: (i, c))
    wsp = pl.BlockSpec((d, d), lambda i: (0, 0))
    return pl.pallas_call(
        _merge_kernel,
        grid=(n // tm,),
        in_specs=[row(0), row(0), row(0), row(P_GA // d), row(P_GB // d), wsp, wsp, wsp,
                  pl.BlockSpec((1, d), lambda i: (0, 0))],
        out_specs=row(0),
        out_shape=jax.ShapeDtypeStruct((n, d), F32),
        compiler_params=_cparams(("parallel",)),
    )(x, ya, yb, gates, gates, wa, wb, wo, gn)


def _mlp_kernel(h_ref, gpre_ref, wu_ref, wd_ref, gpost_ref, o_ref, v_scr, acc_scr):
    j = pl.program_id(1)

    @pl.when(j == 0)
    def _():
        v_scr[...] = _rms(h_ref[...], gpre_ref[...]).astype(BF16)
        acc_scr[...] = jnp.zeros_like(acc_scr)

    up = _dot(v_scr[...], wu_ref[...])
    act = jnp.square(jnp.maximum(up, 0.0))
    acc_scr[...] += _dot(act.astype(BF16), wd_ref[...])

    @pl.when(j == pl.num_programs(1) - 1)
    def _():
        o_ref[...] = h_ref[...] + _rms(acc_scr[...], gpost_ref[...])


def _mlp(h, gpre, wu, wd, gpost):
    n, d = h.shape
    dff = wu.shape[1]
    tm = min(2 * TOK_TILE, n)
    tf = 1024
    vec = pl.BlockSpec((1, d), lambda i, j: (0, 0))
    return pl.pallas_call(
        _mlp_kernel,
        grid=(n // tm, dff // tf),
        in_specs=[pl.BlockSpec((tm, d), lambda i, j: (i, 0)), vec,
                  pl.BlockSpec((d, tf), lambda i, j: (0, j)),
                  pl.BlockSpec((tf, d), lambda i, j: (j, 0)), vec],
        out_specs=pl.BlockSpec((tm, d), lambda i, j: (i, 0)),
        out_shape=jax.ShapeDtypeStruct((n, d), F32),
        scratch_shapes=[pltpu.VMEM((tm, d), BF16), pltpu.VMEM((tm, d), F32)],
        compiler_params=_cparams(("parallel", "arbitrary")),
    )(h, gpre, wu, wd, gpost)


def _ple_kernel(h_ref, p_ref, wp_ref, wg_ref, gn_ref, o_ref):
    h = h_ref[...]
    e = _dot(p_ref[...].astype(BF16), wp_ref[...]) * jax.nn.sigmoid(_dot(h.astype(BF16), wg_ref[...]))
    o_ref[...] = h + _rms(e, gn_ref[...])


def _ple(h, p, wp, wg, gn):
    n, d = h.shape
    pd = p.shape[1]
    tm = min(2 * TOK_TILE, n)
    return pl.pallas_call(
        _ple_kernel,
        grid=(n // tm,),
        in_specs=[pl.BlockSpec((tm, d), lambda i: (i, 0)),
                  pl.BlockSpec((tm, pd), lambda i: (i, 0)),
                  pl.BlockSpec((pd, d), lambda i: (0, 0)),
                  pl.BlockSpec((d, d), lambda i: (0, 0)),
                  pl.BlockSpec((1, d), lambda i: (0, 0))],
        out_specs=pl.BlockSpec((tm, d), lambda i: (i, 0)),
        out_shape=jax.ShapeDtypeStruct((n, d), F32),
        compiler_params=_cparams(("parallel",)),
    )(h, p, wp, wg, gn)


def _overlap_t(seq):
    n_slab = seq // CMP_STRIDE
    n_cmp = (seq - CMP_LEN) // CMP_STRIDE + 1
    n_sel = seq // SEL_BLOCK
    start = np.arange(n_slab) * CMP_STRIDE
    end = start + CMP_LEN - 1
    sel_start = np.arange(n_sel) * SEL_BLOCK
    ov = (start[None, :] < sel_start[:, None] + SEL_BLOCK) & (end[None, :] >= sel_start[:, None])
    ov = ov & (np.arange(n_slab)[None, :] < n_cmp)
    return jnp.asarray(ov.astype(np.float32), BF16)


def kernel(x, p, w_in, w_branch_a, w_branch_b, w_out, norm_pre_mix, norm_post_mix, norm_pre_mlp,
           norm_post_mlp, hg_lb_logits, hg_gnorm, cmp_pe_k, cmp_pe_v, cmp_wk1, cmp_wk2, cmp_wv1, cmp_wv2,
           w_up, w_down, w_ple, w_ple_gate, norm_ple):
    bsz, seq, d = x.shape
    depth = w_in.shape[0]
    assert depth == 1 and hg_lb_logits.shape[0] == 2 and d == D_MODEL
    assert seq % ATT_KT == 0 and seq >= WINDOW + ATT_Q and seq // SEL_BLOCK <= NSA_DH
    assert seq % HG_TOK == 0 and (HG_TOK // HG_CHUNK) % HG_GROUP == 0
    assert ATT_Q in (SEL_BLOCK, 2 * SEL_BLOCK)
    assert ATT_Q % (ATT_KT // ATT_SUBTILES) == 0
    n = bsz * seq
    x2 = x.reshape(n, d)
    row = lambda a: a.reshape(1, -1)

    w = w_in[0]
    s_hq, s_hf, s_hi, s_hg = 0, HG_WIDTH, 2 * HG_WIDTH, 3 * HG_WIDTH
    s_nq = 4 * HG_WIDTH
    s_ng = s_nq + NSA_WIDTH + 6 * NSA_KVW
    s_ga = s_ng + 3 * NSA_HEADS
    col = lambda a, b: w[:, a:b]
    w_all = jnp.concatenate(
        [col(s_ga, s_ga + 2 * D_MODEL), col(s_hq, s_hf), col(s_hi, s_hg), col(s_hg, s_nq), col(s_hf, s_hi),
         col(s_nq, s_ga), jnp.zeros((d, LANES - 3 * NSA_HEADS), F32)], axis=1).astype(BF16)
    assert w_all.shape[1] == P_WIDTH and P_WIDTH % (P_COL_TILES * LANES) == 0
    proj2 = _rms_matmul(x2, row(norm_pre_mix[0]), w_all, BF16, P_WIDTH // P_COL_TILES)
    proj = proj2.reshape(bsz, seq, P_WIDTH)

    y_a = _hgrn(proj, hg_lb_logits, row(hg_gnorm[0]), bsz, seq)

    tables = _rope_tables(seq)
    kvb = P_KV // NSA_KVW
    q_rot = _rope(proj, P_NQ // NSA_WIDTH, NSA_WIDTH, tables, ATT_SCALE * LOG2E, bsz, seq)
    ks_aug, vs_t = _kv_prep(proj, kvb + 2, kvb + 3, tables, 0, True, bsz, seq)
    kw_aug, vw_t = _kv_prep(proj, kvb + 4, kvb + 5, tables, WINDOW, False, bsz, seq)
    kv_slice = lambda i: proj[:, :, P_KV + i * NSA_KVW: P_KV + (i + 1) * NSA_KVW]
    slabs = lambda t: t.reshape(bsz, seq // CMP_STRIDE, CMP_STRIDE, NSA_KV, NSA_DH).transpose(
        0, 3, 1, 2, 4).reshape(bsz, NSA_KV, seq // CMP_STRIDE, CMP_STRIDE * NSA_DH)
    k_cmp, v_cmp = _compress(
        slabs(kv_slice(0)), slabs(kv_slice(1)), cmp_pe_k[0].reshape(1, -1), cmp_pe_v[0].reshape(1, -1),
        cmp_wk1[0].astype(BF16), cmp_wk2[0].astype(BF16), cmp_wv1[0].astype(BF16), cmp_wv2[0].astype(BF16))
    y_b = _attention(proj, q_rot, k_cmp, v_cmp.transpose(0, 1, 3, 2), ks_aug, vs_t, kw_aug, vw_t,
                     _overlap_t(seq), bsz, seq)

    h1 = _merge(x2, y_a.reshape(n, -1), y_b.reshape(n, -1), proj2, w_branch_a[0].astype(BF16),
                w_branch_b[0].astype(BF16), w_out[0].astype(BF16), row(norm_post_mix[0]))
    h2 = _mlp(h1, row(norm_pre_mlp[0]), w_up[0].astype(BF16), w_down[0].astype(BF16), row(norm_post_mlp[0]))
    h3o = _ple(h2, p[0].reshape(n, -1), w_ple[0].astype(BF16), w_ple_gate[0].astype(BF16), row(norm_ple[0]))
    return h3o.reshape(bsz, seq, d)
```

```python
import functools

import numpy as np
import jax
import jax.numpy as jnp
from jax import lax
from jax.experimental import pallas as pl
from jax.experimental.pallas import tpu as pltpu

F32 = jnp.float32
BF16 = jnp.bfloat16

D_MODEL = 1024
HG_HEADS = 8
HG_DK = 128
HG_DV = 128
HG_WIDTH = HG_HEADS * HG_DK
HG_CHUNK = 64
NSA_HEADS = 16
NSA_KV = 4
NSA_GROUP = NSA_HEADS // NSA_KV
NSA_DH = 64
NSA_WIDTH = NSA_HEADS * NSA_DH
NSA_KVW = NSA_KV * NSA_DH
CMP_LEN = 32
CMP_STRIDE = 16
CMP_HIDDEN = 256
SEL_BLOCK = 64
SEL_TOPK = 16
WINDOW = 512
ROPE_THETA = 500000.0
ROPE_DIM = NSA_DH // 4
D_FF = 4 * D_MODEL
EPS = 1e-6
NEG = -1e30
ATT_SCALE = NSA_DH ** -0.5
LOG2E = float(np.log2(np.e))

LANES = 128
SUBLANES_BF16 = 16
VMEM_LIMIT = 56 * 1024 * 1024

TOK_TILE = 512
ATT_Q = 128
ATT_GROUPS = 4
ATT_KT = 1024
ATT_SUBTILES = 8
ATT_LOOKAHEAD = 7
ATT_SPAN = 1024
HG_TOK = 1024
HG_GROUP = 16

P_GA, P_GB = 0, D_MODEL
P_HQ, P_HI, P_HG, P_HF = (2 * D_MODEL + i * HG_WIDTH for i in range(4))
P_NQ = P_HF + HG_WIDTH
P_KV = P_NQ + NSA_WIDTH
P_NG = P_KV + 6 * NSA_KVW
P_WIDTH = P_NG + LANES
P_COL_TILES = 3
HG_EXP2_LIMIT = 115.0


def _cparams(sem):
    return pltpu.CompilerParams(dimension_semantics=sem, vmem_limit_bytes=VMEM_LIMIT)


def _rms(x, gain):
    ms = jnp.mean(x * x, axis=-1, keepdims=True)
    return (x * lax.rsqrt(ms + EPS)) * gain


def _silu(x):
    return x * (0.5 * jnp.tanh(0.5 * x) + 0.5)


def _dot(a, b):
    return jnp.dot(a, b, preferred_element_type=F32)


def _dot_nt(a, b):
    return lax.dot_general(a, b, (((1,), (1,)), ((), ())), preferred_element_type=F32)


def _dot_tn(a, b):
    return lax.dot_general(a, b, (((0,), (0,)), ((), ())), preferred_element_type=F32)


def _colmax(s):
    parts = [s[i:i + 64] for i in range(0, s.shape[0], 64)] if s.shape[0] % 64 == 0 else [s]
    while len(parts) > 1:
        parts = [jnp.maximum(parts[i], parts[i + 1]) if i + 1 < len(parts) else parts[i]
                 for i in range(0, len(parts), 2)]
    return jnp.max(parts[0], axis=0, keepdims=True)


def _split3(x):
    hi = x.astype(BF16)
    r1 = x - hi.astype(F32)
    mid = r1.astype(BF16)
    lo = (r1 - mid.astype(F32)).astype(BF16)
    return hi, mid, lo


def _rms_matmul_kernel(x_ref, g_ref, w_ref, o_ref, u_scr):
    @pl.when(pl.program_id(1) == 0)
    def _():
        u_scr[...] = _rms(x_ref[...], g_ref[...]).astype(BF16)

    o_ref[...] = _dot(u_scr[...], w_ref[...]).astype(o_ref.dtype)


def _rms_matmul(x, gain, w, out_dtype, tn):
    n, d = x.shape
    wd = w.shape[1]
    tm = min(2 * TOK_TILE, n)
    return pl.pallas_call(
        _rms_matmul_kernel,
        grid=(n // tm, wd // tn),
        in_specs=[pl.BlockSpec((tm, d), lambda i, j: (i, 0)),
                  pl.BlockSpec((1, d), lambda i, j: (0, 0)),
                  pl.BlockSpec((d, tn), lambda i, j: (0, j))],
        out_specs=pl.BlockSpec((tm, tn), lambda i, j: (i, j)),
        out_shape=jax.ShapeDtypeStruct((n, wd), out_dtype),
        scratch_shapes=[pltpu.VMEM((tm, d), BF16)],
        compiler_params=_cparams(("parallel", "arbitrary")),
    )(x, gain, w)


def _hgrn_levels():
    return [HG_CHUNK >> (i + 1) for i in range(HG_CHUNK.bit_length() - 1)]


def _hgrn_intra_exact(qf, kk, b, gsel_ref):
    c_len = HG_CHUNK
    row = lax.broadcasted_iota(jnp.int32, (c_len, c_len), 0)
    col = lax.broadcasted_iota(jnp.int32, (c_len, c_len), 1)
    trow = lax.broadcasted_iota(jnp.int32, (c_len, HG_DK), 0)
    a = jnp.where(row == col, _dot_nt(qf.astype(BF16), kk.astype(BF16)), 0.0)
    b3 = jnp.concatenate(_split3(b), axis=0)
    for lvl, h in enumerate(_hgrn_levels()):
        b_at_ref = _dot(gsel_ref[lvl], b3)
        upper = (trow & (2 * h - 1)) >= h
        qe = jnp.where(upper, qf * jnp.exp2(jnp.minimum(b - b_at_ref, 0.0)), 0.0)
        ke = jnp.where(upper, 0.0, kk * jnp.exp2(jnp.minimum(b_at_ref - b, 0.0)))
        same_block = (row & -(2 * h)) == (col & -(2 * h))
        block = same_block & ((row & (2 * h - 1)) >= h) & ((col & (2 * h - 1)) < h)
        a = a + jnp.where(block, _dot_nt(qe.astype(BF16), ke.astype(BF16)), 0.0)
    return a


def _hgrn_kernel(q_ref, i_ref, g_ref, f_ref, lbl_ref, gn_ref, tri_ref, gsel_ref, o_ref, st_ref, st0_scr, *,
                 n_chunks):
    c_len = HG_CHUNK
    half = c_len // 2

    @pl.when(pl.program_id(2) == 0)
    def _():
        st_ref[...] = jnp.zeros_like(st_ref)

    lg = lbl_ref[...]
    mx = jnp.max(lg, axis=0, keepdims=True)
    ex = jnp.exp(lg - mx)
    lb = ex[0:1, :] / jnp.sum(ex, axis=0, keepdims=True)

    row = lax.broadcasted_iota(jnp.int32, (c_len, c_len), 0)
    col = lax.broadcasted_iota(jnp.int32, (c_len, c_len), 1)
    causal = col <= row
    tri3 = tri_ref[...]

    def run_group(first, state, exact):
        chunks = [slice((first + c) * c_len, (first + c + 1) * c_len) for c in range(HG_GROUP)]
        qf, kk, lf = [], [], []
        for rows in chunks:
            qx = q_ref[0, rows, :].astype(F32)
            qf.append(_silu(qx) * (HG_DK ** -0.5))
            forget = lb + (1.0 - lb) * jax.nn.sigmoid(f_ref[0, rows, :].astype(F32))
            kk.append(1.0 - forget)
            lf.append(jnp.log2(forget))
        bb = [_dot(tri3, jnp.concatenate(_split3(x), axis=0)) for x in lf]
        qe, ke, qb, kd, dlast = [], [], [], [], []
        need = jnp.zeros((1, HG_DK), F32)
        for c in range(HG_GROUP):
            b = bb[c]
            b_mid = b[half - 1:half, :]
            b_last = b[c_len - 1:c_len, :]
            if not exact:
                qe.append((qf[c] * jnp.exp2(b - b_mid)).astype(BF16))
                ke.append((kk[c] * jnp.exp2(b_mid - b)).astype(BF16))
                need = jnp.maximum(need, jnp.maximum(b[0:1, :] - b_mid, b_mid - b_last))
            qb.append((qf[c] * jnp.exp2(b)).astype(BF16))
            kd.append((kk[c] * jnp.exp2(b_last - b)).astype(BF16))
            dlast.append(jnp.exp2(b_last))
        vv = [i_ref[0, rows, :] for rows in chunks]
        if exact:
            aa = [_hgrn_intra_exact(qf[c], kk[c], bb[c], gsel_ref) for c in range(HG_GROUP)]
        else:
            aa = [_dot_nt(qe[c], ke[c]) for c in range(HG_GROUP)]
        kv = [_dot_tn(vv[c], kd[c]) for c in range(HG_GROUP)]
        o_intra = [_dot(jnp.where(causal, aa[c], 0.0).astype(BF16), vv[c]) for c in range(HG_GROUP)]
        states = [state]
        for c in range(HG_GROUP):
            states.append(states[c] * dlast[c] + kv[c])
        o_inter = [_dot_nt(qb[c], states[c].astype(BF16)) for c in range(HG_GROUP)]
        for c, rows in enumerate(chunks):
            gx = g_ref[0, rows, :].astype(F32)
            y = _rms(o_intra[c] + o_inter[c], gn_ref[...]) * _silu(gx)
            o_ref[0, rows, :] = y.astype(o_ref.dtype)
        return states[HG_GROUP], need

    def run_all(exact):
        state = st0_scr[...]
        need = jnp.zeros((1, HG_DK), F32)
        for first in range(0, n_chunks, HG_GROUP):
            state, need_g = run_group(first, state, exact)
            need = jnp.maximum(need, need_g)
        return state, need

    st0_scr[...] = st_ref[...]
    state, need = run_all(exact=False)
    st_ref[...] = state

    @pl.when(jnp.max(need) > HG_EXP2_LIMIT)
    def _():
        run_all(exact=True)


def _hgrn(proj, lb_logits, gnorm, bsz, seq):
    n_chunks = HG_TOK // HG_CHUNK
    tri = np.tril(np.ones((HG_CHUNK, HG_CHUNK), np.float32))
    tri3 = jnp.asarray(np.concatenate([tri, tri, tri], axis=1), BF16)
    t_idx = np.arange(HG_CHUNK)
    gsel = np.zeros((len(_hgrn_levels()), HG_CHUNK, HG_CHUNK), np.float32)
    for lvl, h in enumerate(_hgrn_levels()):
        gsel[lvl, t_idx, (t_idx // (2 * h)) * (2 * h) + h - 1] = 1.0
    gsel3 = jnp.asarray(np.concatenate([gsel, gsel, gsel], axis=2), BF16)
    hh = HG_HEADS
    blk = (1, HG_TOK, HG_DK)
    head_cols = lambda off: pl.BlockSpec(blk, lambda b, h, t: (b, t, off // HG_DK + h))
    return pl.pallas_call(
        functools.partial(_hgrn_kernel, n_chunks=n_chunks),
        grid=(bsz, hh, seq // HG_TOK),
        in_specs=[head_cols(P_HQ), head_cols(P_HI), head_cols(P_HG), head_cols(P_HF),
                  pl.BlockSpec((2, HG_DK), lambda b, h, t: (0, h)),
                  pl.BlockSpec((1, HG_DV), lambda b, h, t: (0, 0)),
                  pl.BlockSpec((HG_CHUNK, 3 * HG_CHUNK), lambda b, h, t: (0, 0)),
                  pl.BlockSpec(gsel3.shape, lambda b, h, t: (0, 0, 0))],
        out_specs=pl.BlockSpec(blk, lambda b, h, t: (b, t, h)),
        out_shape=jax.ShapeDtypeStruct((bsz, seq, HG_WIDTH), BF16),
        scratch_shapes=[pltpu.VMEM((HG_DV, HG_DK), F32), pltpu.VMEM((HG_DV, HG_DK), F32)],
        compiler_params=_cparams(("parallel", "parallel", "arbitrary")),
    )(proj, proj, proj, proj, lb_logits, gnorm, tri3, gsel3)


def _rope_kernel(x_ref, c_ref, s1_ref, s2_ref, o_ref, *, scale):
    width = x_ref.shape[-1]
    c = c_ref[...]
    s1 = s1_ref[...]
    s2 = s2_ref[...]
    for j in range(width // LANES):
        cols = slice(j * LANES, (j + 1) * LANES)
        x = x_ref[0, :, cols].astype(F32)
        if scale != 1.0:
            x = x * scale
        o_ref[0, :, cols] = _rope_lanes(x, c, s1, s2).astype(o_ref.dtype)


def _rope(src, col_block, width, tables, scale, bsz, seq):
    tt = min(TOK_TILE, seq)
    tab = pl.BlockSpec((tt, LANES), lambda b, t: (t, 0))
    return pl.pallas_call(
        functools.partial(_rope_kernel, scale=scale),
        grid=(bsz, seq // tt),
        in_specs=[pl.BlockSpec((1, tt, width), lambda b, t: (b, t, col_block)), tab, tab, tab],
        out_specs=pl.BlockSpec((1, tt, width), lambda b, t: (b, t, 0)),
        out_shape=jax.ShapeDtypeStruct((bsz, seq, width), BF16),
        compiler_params=_cparams(("parallel", "parallel")),
    )(src, *tables)


def _rope_lanes(x, c, s1, s2):
    return x * c + pltpu.roll(x, LANES - ROPE_DIM // 2, 1) * s1 + pltpu.roll(x, ROPE_DIM // 2, 1) * s2


def _kv_prep_kernel(k_ref, v_ref, c_ref, s1_ref, s2_ref, ko_ref, vo_ref, *, pad_blocks, with_block_id):
    tt = k_ref.shape[1]
    step = pl.program_id(1)
    is_pad = step < pad_blocks
    pos = (step - pad_blocks) * tt + lax.broadcasted_iota(jnp.int32, (tt, NSA_DH), 0)
    lane = lax.broadcasted_iota(jnp.int32, (tt, NSA_DH), 1)
    if with_block_id:
        extra = jnp.where(pos // SEL_BLOCK == lane, 1.0, 0.0)
    else:
        extra = jnp.where((lane == 0) & is_pad, 1.0, 0.0)
    c, s1, s2 = c_ref[...], s1_ref[...], s2_ref[...]
    vt = v_ref[0].astype(F32).T
    for j in range(NSA_KVW // LANES):
        y = _rope_lanes(k_ref[0, :, j * LANES:(j + 1) * LANES].astype(F32), c, s1, s2)
        y = jnp.where(is_pad, 0.0, y)
        for g2 in range(LANES // NSA_DH):
            g = j * (LANES // NSA_DH) + g2
            ko_ref[0, g] = jnp.concatenate([y[:, g2 * NSA_DH:(g2 + 1) * NSA_DH], extra], axis=1).astype(BF16)
    for g in range(NSA_KV):
        vo_ref[0, g, 0:NSA_DH, :] = jnp.where(is_pad, 0.0, vt[g * NSA_DH:(g + 1) * NSA_DH]).astype(BF16)
        vo_ref[0, g, NSA_DH:NSA_DH + SUBLANES_BF16, :] = jnp.ones((SUBLANES_BF16, tt), BF16)


def _kv_prep(proj, k_block, v_block, tables, pad_rows, with_block_id, bsz, seq):
    tt = min(TOK_TILE, seq)
    pad_blocks = pad_rows // tt
    assert pad_blocks * tt == pad_rows
    src_t = lambda t: jnp.maximum(t - pad_blocks, 0)
    tab = pl.BlockSpec((tt, LANES), lambda b, t: (src_t(t), 0))
    rows = seq + pad_rows
    return pl.pallas_call(
        functools.partial(_kv_prep_kernel, pad_blocks=pad_blocks, with_block_id=with_block_id),
        grid=(bsz, rows // tt),
        in_specs=[pl.BlockSpec((1, tt, NSA_KVW), lambda b, t: (b, src_t(t), k_block)),
                  pl.BlockSpec((1, tt, NSA_KVW), lambda b, t: (b, src_t(t), v_block)), tab, tab, tab],
        out_specs=[pl.BlockSpec((1, NSA_KV, tt, 2 * NSA_DH), lambda b, t: (b, 0, t, 0)),
                   pl.BlockSpec((1, NSA_KV, NSA_DH + SUBLANES_BF16, tt), lambda b, t: (b, 0, 0, t))],
        out_shape=[jax.ShapeDtypeStruct((bsz, NSA_KV, rows, 2 * NSA_DH), BF16),
                   jax.ShapeDtypeStruct((bsz, NSA_KV, NSA_DH + SUBLANES_BF16, rows), BF16)],
        compiler_params=_cparams(("parallel", "parallel")),
    )(proj, proj, *tables)


def _rope_tables(seq):
    half = ROPE_DIM // 2
    inv = jnp.asarray(ROPE_THETA ** (-np.arange(half) * 2.0 / ROPE_DIM), F32)
    ang = jnp.arange(seq, dtype=F32)[:, None] * inv[None, :]
    cos, sin = jnp.cos(ang), jnp.sin(ang)
    lane = np.arange(LANES) % NSA_DH
    pick = lane % half
    cos_l = cos[:, pick]
    sin_l = sin[:, pick]
    first = jnp.asarray(lane < half)
    second = jnp.asarray((lane >= half) & (lane < ROPE_DIM))
    c = jnp.where(first | second, cos_l, 1.0)
    s1 = jnp.where(first, -sin_l, 0.0)
    s2 = jnp.where(second, sin_l, 0.0)
    return c, s1, s2


def _gelu_tanh(x):
    return 0.5 * x * (1.0 + jnp.tanh(np.sqrt(2.0 / np.pi).astype(np.float32) * (x + 0.044715 * (x * x * x))))


def _compress_kernel(xk_ref, xv_ref, pek_ref, pev_ref, wk1_ref, wk2_ref, wv1_ref, wv2_ref, ok_ref, ov_ref):
    half_w = CMP_STRIDE * NSA_DH

    def one(x_ref, pe_ref, w1_ref, w2_ref):
        x = x_ref[0, 0]
        n_slab = x.shape[0]
        p0 = _dot(x, w1_ref[0:half_w, :])
        p1 = _dot(x, w1_ref[half_w:2 * half_w, :])
        pe = jnp.broadcast_to(pe_ref[...], (8, 2 * half_w)).astype(BF16)
        pw = _dot(pe, w1_ref[...])[0:1, :]
        h = p0 + pltpu.roll(p1, n_slab - 1, 0) + pw
        return _dot(_gelu_tanh(h).astype(BF16), w2_ref[...])

    ok_ref[0, 0] = (one(xk_ref, pek_ref, wk1_ref, wk2_ref) * (ATT_SCALE * LOG2E)).astype(ok_ref.dtype)
    ov_ref[0, 0] = one(xv_ref, pev_ref, wv1_ref, wv2_ref).astype(ov_ref.dtype)


def _compress(xk, xv, pek, pev, wk1, wk2, wv1, wv2):
    bsz, grp, n_slab, wdt = xk.shape
    xs = pl.BlockSpec((1, 1, n_slab, wdt), lambda b, g: (b, g, 0, 0))
    full = lambda a: pl.BlockSpec(a.shape, lambda b, g: (0,) * a.ndim)
    osd = jax.ShapeDtypeStruct((bsz, grp, n_slab, NSA_DH), BF16)
    os_ = pl.BlockSpec((1, 1, n_slab, NSA_DH), lambda b, g: (b, g, 0, 0))
    return pl.pallas_call(
        _compress_kernel,
        grid=(bsz, grp),
        in_specs=[xs, xs, full(pek), full(pev), full(wk1), full(wk2), full(wv1), full(wv2)],
        out_specs=[os_, os_],
        out_shape=[osd, osd],
        compiler_params=_cparams(("parallel", "parallel")),
    )(xk, xv, pek, pev, wk1, wk2, wv1, wv2)


def _attn_kernel(nq_ref, qr_ref, kcmp_ref, vcmpt_ref, ks_ref, vst_ref, kw_ref, vwt_ref, ng_ref, ovt_ref,
                 o_ref, gt_scr, qaug_scr, *, seq):
    qn = ATT_Q
    grp = NSA_GROUP
    ln = grp * qn
    gw = grp * NSA_DH
    n_sel = seq // SEL_BLOCK
    top = min(SEL_TOPK, n_sel)
    groups = range(ATT_GROUPS)
    g_first = pl.program_id(1) * ATT_GROUPS
    q0 = pl.multiple_of(pl.program_id(2) * qn, qn)

    def stack_heads(x):
        return jnp.concatenate([x[:, r * NSA_DH:(r + 1) * NSA_DH] for r in range(grp)], axis=0)

    def add_per_head(s, bias):
        return jnp.concatenate([s[:, r * qn:(r + 1) * qn] + bias for r in range(grp)], axis=1)

    t_lane = q0 + (lax.broadcasted_iota(jnp.int32, (1, ln), 1) & (qn - 1))
    t_q = q0 + lax.broadcasted_iota(jnp.int32, (1, qn), 1)

    n_slab = kcmp_ref.shape[2]
    wlen = WINDOW + qn
    pad_col = jnp.where(lax.broadcasted_iota(jnp.int32, (ln, NSA_DH), 1) == 0, NEG, 0.0).astype(BF16)
    qs = [stack_heads(nq_ref[0, :, g * gw:(g + 1) * gw]) for g in groups]
    qrs = [stack_heads(qr_ref[0, :, g * gw:(g + 1) * gw]) for g in groups]

    def window_scores():
        return [_dot_nt(kw_ref[0, g, pl.ds(q0, wlen), :], jnp.concatenate([qrs[g], pad_col], axis=1))
                for g in groups]

    def diag_scores():
        return [_dot_nt(ks_ref[0, g, pl.ds(q0, qn), :][:, 0:NSA_DH], qrs[g]) for g in groups]

    def window_branch(sw):
        i_loc = lax.broadcasted_iota(jnp.int32, (qn, 1), 0)
        j_loc = lax.broadcasted_iota(jnp.int32, (1, qn), 1)
        win_lo = jnp.where(i_loc > j_loc, 0.0, NEG)
        win_hi = jnp.where(i_loc <= j_loc, 0.0, NEG)
        ow = []
        for g in groups:
            s = jnp.concatenate([add_per_head(sw[g][0:qn], win_lo), sw[g][qn:WINDOW],
                                 add_per_head(sw[g][WINDOW:wlen], win_hi)], axis=0)
            pw = jnp.exp2(s - _colmax(s))
            rw = _dot(vwt_ref[0, g, :, pl.ds(q0, wlen)], pw.astype(BF16))
            ow.append(rw[0:NSA_DH] / rw[NSA_DH:NSA_DH + 1])
        return ow

    def diag_tile(sd):
        kpos = q0 + lax.broadcasted_iota(jnp.int32, (qn, 1), 0)
        diag_bias = jnp.where(kpos <= t_q, 0.0, NEG)
        init = []
        for g in groups:
            s = add_per_head(sd[g], diag_bias)
            m0 = _colmax(s)
            init += [m0, _dot(vst_ref[0, g, :, pl.ds(q0, qn)], jnp.exp2(s - m0).astype(BF16))]
        return init

    gt_scr[...] = jax.nn.sigmoid(ng_ref[0].astype(F32)).T

    def gate_row(branch, g):
        return jnp.concatenate(
            [gt_scr[pl.ds(branch * NSA_HEADS + (g_first + g) * grp + r, 1), :] for r in range(grp)], axis=1)

    gate_sel = [gate_row(1, g) for g in groups]

    def before_sweep(n_spans):
        rows = min(n_spans * ATT_SPAN // CMP_STRIDE, n_slab)
        nb = min(n_spans * ATT_SPAN // SEL_BLOCK, n_sel)
        cend = lax.broadcasted_iota(jnp.int32, (rows, 1), 0) * CMP_STRIDE + (CMP_LEN - 1)
        cmp_bias = jnp.where(cend <= t_q, 0.0, NEG)
        ovt = ovt_ref[0:nb, 0:rows]
        sc = [_dot_nt(kcmp_ref[0, g, 0:rows, :], qs[g]) for g in groups]
        sw = window_scores()
        sd = diag_scores()
        imp, oc = [], []
        for g in groups:
            sm = add_per_head(sc[g], cmp_bias)
            e = jnp.exp2(sm - _colmax(sm))
            den = jnp.sum(e, axis=0, keepdims=True)
            p = e * jnp.where(t_lane >= CMP_LEN - 1, 1.0 / den, 0.0)
            oc.append(_dot(vcmpt_ref[0, g, :, 0:rows], p.astype(BF16)))
            psum = p[:, 0:qn]
            for r in range(1, grp):
                psum = psum + p[:, r * qn:(r + 1) * qn]
            hi, mid, lo = _split3(psum)
            imp.append(_dot(ovt, hi) + _dot(ovt, mid) + _dot(ovt, lo))
        ow = window_branch(sw)
        init = diag_tile(sd)
        out_cw = [gate_row(0, g) * oc[g] + gate_row(2, g) * ow[g] for g in groups]

        jidx = lax.broadcasted_iota(jnp.int32, (nb, qn), 0)
        tq = q0 + lax.broadcasted_iota(jnp.int32, (nb, qn), 1)
        cur = tq // SEL_BLOCK
        forced = (jidx == 0) | (jidx == cur) | (jidx == cur - 1)
        causal_blk = jidx * SEL_BLOCK <= tq
        rows8 = 8
        jrow = lax.broadcasted_iota(jnp.int32, (rows8, qn), 0)
        for g in groups:
            val = jnp.where(forced, -2.0, imp[g])
            val = jnp.where(causal_blk, val, -1.0)
            parts = [val[v * rows8:(v + 1) * rows8] for v in range(nb // rows8)]
            counts = [jnp.zeros((rows8, qn), F32) for _ in parts]
            for jp in range(nb):
                other = jnp.broadcast_to(val[jp:jp + 1, :], (rows8, qn))
                for v, part in enumerate(parts):
                    ge = jnp.where(other >= part, 1.0, 0.0)
                    gt = jnp.where(other > part, 1.0, 0.0)
                    if v * rows8 > jp:
                        before = ge
                    elif (v + 1) * rows8 <= jp:
                        before = gt
                    else:
                        before = jnp.where(jrow > jp - v * rows8, ge, gt)
                    counts[v] = counts[v] + before
            rank = jnp.concatenate(counts, axis=0)
            chosen = forced | (rank < float(max(top - 3, 0)))
            selb = jnp.where(chosen & causal_blk, 0.0, NEG)
            selb_t = jnp.concatenate([selb, jnp.zeros((LANES - nb, qn), F32)], axis=0).T
            selb_t = selb_t[:, 0:NSA_DH].astype(BF16)
            qaug_scr[g] = jnp.concatenate([qrs[g], jnp.concatenate([selb_t] * grp, axis=0)], axis=1)
        return tuple(out_cw) + tuple(init)

    n_variants = (seq + ATT_SPAN - 1) // ATT_SPAN
    staged = lax.switch(q0 // ATT_SPAN, [functools.partial(before_sweep, k + 1) for k in range(n_variants)])
    out_cw, init = staged[:ATT_GROUPS], staged[ATT_GROUPS:]

    kt_len = ATT_KT
    sub = kt_len // ATT_SUBTILES

    def sweep_block(k_base, n_sub, carry):
        state = list(carry)
        items = [(j, g) for j in range(n_sub) for g in groups]
        k0s = [pl.multiple_of(k_base + j * sub, sub) for j in range(n_sub)]
        score = lambda j, g: _dot_nt(ks_ref[0, g, pl.ds(k0s[j], sub), :], qaug_scr[g])
        ahead = min(ATT_LOOKAHEAD * ATT_GROUPS, len(items))
        scores = [score(*it) for it in items[:ahead]]
        for n, (j, g) in enumerate(items):
            s = scores[n]
            m_run, acc = state[2 * g], state[2 * g + 1]
            m_new = jnp.maximum(m_run, _colmax(s))
            alpha = jnp.exp2(m_run - m_new)
            pt = jnp.exp2(s - m_new).astype(BF16)
            state[2 * g + 1] = alpha * acc + _dot(vst_ref[0, g, :, pl.ds(k0s[j], sub)], pt)
            state[2 * g] = m_new
            if n + ahead < len(items):
                scores.append(score(*items[n + ahead]))
        return tuple(state)

    n_whole = q0 // kt_len
    carry = lax.fori_loop(0, n_whole, lambda kt, cy: sweep_block(kt * kt_len, ATT_SUBTILES, cy), tuple(init))
    tails = [functools.partial(sweep_block, n_whole * kt_len, r) for r in range(ATT_SUBTILES)]
    final = lax.switch((q0 - n_whole * kt_len) // sub, tails, carry)

    for g in groups:
        acc = final[2 * g + 1]
        out_t = out_cw[g] + gate_sel[g] * (acc[0:NSA_DH] / acc[NSA_DH:NSA_DH + 1])
        pieces = []
        for r2 in range(grp // 2):
            pair = jnp.concatenate([out_t[:, (2 * r2) * qn:(2 * r2 + 1) * qn],
                                    out_t[:, (2 * r2 + 1) * qn:(2 * r2 + 2) * qn]], axis=0)
            pieces.append(pair.T)
        o_ref[0, :, g * gw:(g + 1) * gw] = jnp.concatenate(pieces, axis=1).astype(o_ref.dtype)


def _attention(proj, q_rot, k_cmp, v_cmp_t, ks, vs_t, kw, vw_t, ov_t, bsz, seq):
    qn = ATT_Q
    gw = ATT_GROUPS * NSA_GROUP * NSA_DH
    n_slab = k_cmp.shape[2]
    n_sel = seq // SEL_BLOCK
    qspec = pl.BlockSpec((1, qn, gw), lambda b, g, i: (b, i, g))
    per_group = lambda a: pl.BlockSpec((1, ATT_GROUPS) + a.shape[2:], lambda b, g, i: (b, g, 0, 0))
    return pl.pallas_call(
        functools.partial(_attn_kernel, seq=seq),
        grid=(bsz, NSA_KV // ATT_GROUPS, seq // qn),
        in_specs=[pl.BlockSpec((1, qn, gw), lambda b, g, i: (b, i, P_NQ // gw + g)), qspec,
                  per_group(k_cmp), per_group(v_cmp_t), per_group(ks), per_group(vs_t),
                  per_group(kw), per_group(vw_t),
                  pl.BlockSpec((1, qn, LANES), lambda b, g, i: (b, i, P_NG // LANES)),
                  pl.BlockSpec((n_sel, n_slab), lambda b, g, i: (0, 0))],
        out_specs=qspec,
        out_shape=jax.ShapeDtypeStruct((bsz, seq, NSA_WIDTH), BF16),
        scratch_shapes=[pltpu.VMEM((LANES, qn), F32),
                        pltpu.VMEM((ATT_GROUPS, NSA_GROUP * qn, 2 * NSA_DH), BF16)],
        compiler_params=_cparams(("parallel", "parallel", "arbitrary")),
    )(proj, q_rot, k_cmp, v_cmp_t, ks, vs_t, kw, vw_t, proj, ov_t)


def _merge_kernel(x_ref, ya_ref, yb_ref, ga_ref, gb_ref, wa_ref, wb_ref, wo_ref, gn_ref, o_ref):
    ga = jax.nn.sigmoid(ga_ref[...].astype(F32))
    gb = jax.nn.sigmoid(gb_ref[...].astype(F32))
    merged = ga * _dot(ya_ref[...], wa_ref[...]) + gb * _dot(yb_ref[...], wb_ref[...])
    mo = _dot(merged.astype(BF16), wo_ref[...])
    o_ref[...] = x_ref[...] + _rms(mo, gn_ref[...])


def _merge(x, ya, yb, gates, wa, wb, wo, gn):
    n, d = x.shape
    tm = min(TOK_TILE, n)
    row = lambda c: pl.BlockSpec((tm, d), lambda i: (i, c))
    wsp = pl.BlockSpec((d, d), lambda i: (0, 0))
    return pl.pallas_call(
        _merge_kernel,
        grid=(n // tm,),
        in_specs=[row(0), row(0), row(0), row(P_GA // d), row(P_GB // d), wsp, wsp, wsp,
                  pl.BlockSpec((1, d), lambda i: (0, 0))],
        out_specs=row(0),
        out_shape=jax.ShapeDtypeStruct((n, d), F32),
        compiler_params=_cparams(("parallel",)),
    )(x, ya, yb, gates, gates, wa, wb, wo, gn)


def _mlp_kernel(h_ref, gpre_ref, wu_ref, wd_ref, gpost_ref, o_ref, v_scr, acc_scr):
    j = pl.program_id(1)

    @pl.when(j == 0)
    def _():
        v_scr[...] = _rms(h_ref[...], gpre_ref[...]).astype(BF16)
        acc_scr[...] = jnp.zeros_like(acc_scr)

    up = _dot(v_scr[...], wu_ref[...])
    act = jnp.square(jnp.maximum(up, 0.0))
    acc_scr[...] += _dot(act.astype(BF16), wd_ref[...])

    @pl.when(j == pl.num_programs(1) - 1)
    def _():
        o_ref[...] = h_ref[...] + _rms(acc_scr[...], gpost_ref[...])


def _mlp(h, gpre, wu, wd, gpost):
    n, d = h.shape
    dff = wu.shape[1]
    tm = min(2 * TOK_TILE, n)
    tf = 1024
    vec = pl.BlockSpec((1, d), lambda i, j: (0, 0))
    return pl.pallas_call(
        _mlp_kernel,
        grid=(n // tm, dff // tf),
        in_specs=[pl.BlockSpec((tm, d), lambda i, j: (i, 0)), vec,
                  pl.BlockSpec((d, tf), lambda i, j: (0, j)),
                  pl.BlockSpec((tf, d), lambda i, j: (j, 0)), vec],
        out_specs=pl.BlockSpec((tm, d), lambda i, j: (i, 0)),
        out_shape=jax.ShapeDtypeStruct((n, d), F32),
        scratch_shapes=[pltpu.VMEM((tm, d), BF16), pltpu.VMEM((tm, d), F32)],
        compiler_params=_cparams(("parallel", "arbitrary")),
    )(h, gpre, wu, wd, gpost)


def _ple_kernel(h_ref, p_ref, wp_ref, wg_ref, gn_ref, o_ref):
    h = h_ref[...]
    e = _dot(p_ref[...].astype(BF16), wp_ref[...]) * jax.nn.sigmoid(_dot(h.astype(BF16), wg_ref[...]))
    o_ref[...] = h + _rms(e, gn_ref[...])


def _ple(h, p, wp, wg, gn):
    n, d = h.shape
    pd = p.shape[1]
    tm = min(2 * TOK_TILE, n)
    return pl.pallas_call(
        _ple_kernel,
        grid=(n // tm,),
        in_specs=[pl.BlockSpec((tm, d), lambda i: (i, 0)),
                  pl.BlockSpec((tm, pd), lambda i: (i, 0)),
                  pl.BlockSpec((pd, d), lambda i: (0, 0)),
                  pl.BlockSpec((d, d), lambda i: (0, 0)),
                  pl.BlockSpec((1, d), lambda i: (0, 0))],
        out_specs=pl.BlockSpec((tm, d), lambda i: (i, 0)),
        out_shape=jax.ShapeDtypeStruct((n, d), F32),
        compiler_params=_cparams(("parallel",)),
    )(h, p, wp, wg, gn)


def _overlap_t(seq):
    n_slab = seq // CMP_STRIDE
    n_cmp = (seq - CMP_LEN) // CMP_STRIDE + 1
    n_sel = seq // SEL_BLOCK
    start = np.arange(n_slab) * CMP_STRIDE
    end = start + CMP_LEN - 1
    sel_start = np.arange(n_sel) * SEL_BLOCK
    ov = (start[None, :] < sel_start[:, None] + SEL_BLOCK) & (end[None, :] >= sel_start[:, None])
    ov = ov & (np.arange(n_slab)[None, :] < n_cmp)
    return jnp.asarray(ov.astype(np.float32), BF16)


def kernel(x, p, w_in, w_branch_a, w_branch_b, w_out, norm_pre_mix, norm_post_mix, norm_pre_mlp,
           norm_post_mlp, hg_lb_logits, hg_gnorm, cmp_pe_k, cmp_pe_v, cmp_wk1, cmp_wk2, cmp_wv1, cmp_wv2,
           w_up, w_down, w_ple, w_ple_gate, norm_ple):
    bsz, seq, d = x.shape
    depth = w_in.shape[0]
    assert depth == 1 and hg_lb_logits.shape[0] == 2 and d == D_MODEL
    assert seq % ATT_KT == 0 and seq >= WINDOW + ATT_Q and seq // SEL_BLOCK <= NSA_DH
    assert seq % HG_TOK == 0 and (HG_TOK // HG_CHUNK) % HG_GROUP == 0
    assert ATT_Q in (SEL_BLOCK, 2 * SEL_BLOCK)
    assert ATT_Q % (ATT_KT // ATT_SUBTILES) == 0
    n = bsz * seq
    x2 = x.reshape(n, d)
    row = lambda a: a.reshape(1, -1)

    w = w_in[0]
    s_hq, s_hf, s_hi, s_hg = 0, HG_WIDTH, 2 * HG_WIDTH, 3 * HG_WIDTH
    s_nq = 4 * HG_WIDTH
    s_ng = s_nq + NSA_WIDTH + 6 * NSA_KVW
    s_ga = s_ng + 3 * NSA_HEADS
    col = lambda a, b: w[:, a:b]
    w_all = jnp.concatenate(
        [col(s_ga, s_ga + 2 * D_MODEL), col(s_hq, s_hf), col(s_hi, s_hg), col(s_hg, s_nq), col(s_hf, s_hi),
         col(s_nq, s_ga), jnp.zeros((d, LANES - 3 * NSA_HEADS), F32)], axis=1).astype(BF16)
    assert w_all.shape[1] == P_WIDTH and P_WIDTH % (P_COL_TILES * LANES) == 0
    proj2 = _rms_matmul(x2, row(norm_pre_mix[0]), w_all, BF16, P_WIDTH // P_COL_TILES)
    proj = proj2.reshape(bsz, seq, P_WIDTH)

    y_a = _hgrn(proj, hg_lb_logits, row(hg_gnorm[0]), bsz, seq)

    tables = _rope_tables(seq)
    kvb = P_KV // NSA_KVW
    q_rot = _rope(proj, P_NQ // NSA_WIDTH, NSA_WIDTH, tables, ATT_SCALE * LOG2E, bsz, seq)
    ks_aug, vs_t = _kv_prep(proj, kvb + 2, kvb + 3, tables, 0, True, bsz, seq)
    kw_aug, vw_t = _kv_prep(proj, kvb + 4, kvb + 5, tables, WINDOW, False, bsz, seq)
    kv_slice = lambda i: proj[:, :, P_KV + i * NSA_KVW: P_KV + (i + 1) * NSA_KVW]
    slabs = lambda t: t.reshape(bsz, seq, NSA_KV, NSA_DH).transpose(0, 2, 1, 3).reshape(
        bsz, NSA_KV, seq // CMP_STRIDE, CMP_STRIDE * NSA_DH)
    k_cmp, v_cmp = _compress(
        slabs(kv_slice(0)), slabs(kv_slice(1)), cmp_pe_k[0].reshape(1, -1), cmp_pe_v[0].reshape(1, -1),
        cmp_wk1[0].astype(BF16), cmp_wk2[0].astype(BF16), cmp_wv1[0].astype(BF16), cmp_wv2[0].astype(BF16))
    y_b = _attention(proj, q_rot, k_cmp, v_cmp.transpose(0, 1, 3, 2), ks_aug, vs_t, kw_aug, vw_t,
                     _overlap_t(seq), bsz, seq)

    h1 = _merge(x2, y_a.reshape(n, -1), y_b.reshape(n, -1), proj2, w_branch_a[0].astype(BF16),
                w_branch_b[0].astype(BF16), w_out[0].astype(BF16), row(norm_post_mix[0]))
    h2 = _mlp(h1, row(norm_pre_mlp[0]), w_up[0].astype(BF16), w_down[0].astype(BF16), row(norm_post_mlp[0]))
    h3o = _ple(h2, p[0].reshape(n, -1), w_ple[0].astype(BF16), w_ple_gate[0].astype(BF16), row(norm_ple[0]))
    return h3o.reshape(bsz, seq, d)
```

```python
import functools

import numpy as np
import jax
import jax.numpy as jnp
from jax import lax
from jax.experimental import pallas as pl
from jax.experimental.pallas import tpu as pltpu

F32 = jnp.float32
BF16 = jnp.bfloat16

D_MODEL = 1024
HG_HEADS = 8
HG_DK = 128
HG_DV = 128
HG_WIDTH = HG_HEADS * HG_DK
HG_CHUNK = 64
NSA_HEADS = 16
NSA_KV = 4
NSA_GROUP = NSA_HEADS // NSA_KV
NSA_DH = 64
NSA_WIDTH = NSA_HEADS * NSA_DH
NSA_KVW = NSA_KV * NSA_DH
CMP_LEN = 32
CMP_STRIDE = 16
CMP_HIDDEN = 256
SEL_BLOCK = 64
SEL_TOPK = 16
WINDOW = 512
ROPE_THETA = 500000.0
ROPE_DIM = NSA_DH // 4
D_FF = 4 * D_MODEL
EPS = 1e-6
NEG = -1e30
ATT_SCALE = NSA_DH ** -0.5
LOG2E = float(np.log2(np.e))

LANES = 128
SUBLANES_BF16 = 16
VMEM_LIMIT = 56 * 1024 * 1024

TOK_TILE = 512
ATT_Q = 128
ATT_GROUPS = 4
ATT_KT = 1024
ATT_SUBTILES = 8
ATT_LOOKAHEAD = 7
ATT_SPAN = 2048
HG_TOK = 1024
HG_GROUP = 16

P_GA, P_GB = 0, D_MODEL
P_HQ, P_HI, P_HG, P_HF = (2 * D_MODEL + i * HG_WIDTH for i in range(4))
P_NQ = P_HF + HG_WIDTH
P_KV = P_NQ + NSA_WIDTH
P_NG = P_KV + 6 * NSA_KVW
P_WIDTH = P_NG + LANES
P_COL_TILES = 3
HG_EXP2_LIMIT = 115.0


def _cparams(sem):
    return pltpu.CompilerParams(dimension_semantics=sem, vmem_limit_bytes=VMEM_LIMIT)


def _rms(x, gain):
    ms = jnp.mean(x * x, axis=-1, keepdims=True)
    return (x * lax.rsqrt(ms + EPS)) * gain


def _silu(x):
    return x * (0.5 * jnp.tanh(0.5 * x) + 0.5)


def _dot(a, b):
    return jnp.dot(a, b, preferred_element_type=F32)


def _dot_nt(a, b):
    return lax.dot_general(a, b, (((1,), (1,)), ((), ())), preferred_element_type=F32)


def _dot_tn(a, b):
    return lax.dot_general(a, b, (((0,), (0,)), ((), ())), preferred_element_type=F32)


def _colmax(s):
    parts = [s[i:i + 64] for i in range(0, s.shape[0], 64)] if s.shape[0] % 64 == 0 else [s]
    while len(parts) > 1:
        parts = [jnp.maximum(parts[i], parts[i + 1]) if i + 1 < len(parts) else parts[i]
                 for i in range(0, len(parts), 2)]
    return jnp.max(parts[0], axis=0, keepdims=True)


def _split3(x):
    hi = x.astype(BF16)
    r1 = x - hi.astype(F32)
    mid = r1.astype(BF16)
    lo = (r1 - mid.astype(F32)).astype(BF16)
    return hi, mid, lo


def _rms_matmul_kernel(x_ref, g_ref, w_ref, o_ref, u_scr):
    @pl.when(pl.program_id(1) == 0)
    def _():
        u_scr[...] = _rms(x_ref[...], g_ref[...]).astype(BF16)

    o_ref[...] = _dot(u_scr[...], w_ref[...]).astype(o_ref.dtype)


def _rms_matmul(x, gain, w, out_dtype, tn):
    n, d = x.shape
    wd = w.shape[1]
    tm = min(2 * TOK_TILE, n)
    return pl.pallas_call(
        _rms_matmul_kernel,
        grid=(n // tm, wd // tn),
        in_specs=[pl.BlockSpec((tm, d), lambda i, j: (i, 0)),
                  pl.BlockSpec((1, d), lambda i, j: (0, 0)),
                  pl.BlockSpec((d, tn), lambda i, j: (0, j))],
        out_specs=pl.BlockSpec((tm, tn), lambda i, j: (i, j)),
        out_shape=jax.ShapeDtypeStruct((n, wd), out_dtype),
        scratch_shapes=[pltpu.VMEM((tm, d), BF16)],
        compiler_params=_cparams(("parallel", "arbitrary")),
    )(x, gain, w)


def _hgrn_levels():
    return [HG_CHUNK >> (i + 1) for i in range(HG_CHUNK.bit_length() - 1)]


def _hgrn_intra_exact(qf, kk, b, gsel_ref):
    c_len = HG_CHUNK
    row = lax.broadcasted_iota(jnp.int32, (c_len, c_len), 0)
    col = lax.broadcasted_iota(jnp.int32, (c_len, c_len), 1)
    trow = lax.broadcasted_iota(jnp.int32, (c_len, HG_DK), 0)
    a = jnp.where(row == col, _dot_nt(qf.astype(BF16), kk.astype(BF16)), 0.0)
    b3 = jnp.concatenate(_split3(b), axis=0)
    for lvl, h in enumerate(_hgrn_levels()):
        b_at_ref = _dot(gsel_ref[lvl], b3)
        upper = (trow & (2 * h - 1)) >= h
        qe = jnp.where(upper, qf * jnp.exp2(jnp.minimum(b - b_at_ref, 0.0)), 0.0)
        ke = jnp.where(upper, 0.0, kk * jnp.exp2(jnp.minimum(b_at_ref - b, 0.0)))
        same_block = (row & -(2 * h)) == (col & -(2 * h))
        block = same_block & ((row & (2 * h - 1)) >= h) & ((col & (2 * h - 1)) < h)
        a = a + jnp.where(block, _dot_nt(qe.astype(BF16), ke.astype(BF16)), 0.0)
    return a


def _hgrn_kernel(q_ref, i_ref, g_ref, f_ref, lbl_ref, gn_ref, tri_ref, gsel_ref, o_ref, st_ref, st0_scr, *,
                 n_chunks):
    c_len = HG_CHUNK
    half = c_len // 2

    @pl.when(pl.program_id(2) == 0)
    def _():
        st_ref[...] = jnp.zeros_like(st_ref)

    lg = lbl_ref[...]
    mx = jnp.max(lg, axis=0, keepdims=True)
    ex = jnp.exp(lg - mx)
    lb = ex[0:1, :] / jnp.sum(ex, axis=0, keepdims=True)

    row = lax.broadcasted_iota(jnp.int32, (c_len, c_len), 0)
    col = lax.broadcasted_iota(jnp.int32, (c_len, c_len), 1)
    causal = col <= row
    tri3 = tri_ref[...]

    def run_group(first, state, exact):
        chunks = [slice((first + c) * c_len, (first + c + 1) * c_len) for c in range(HG_GROUP)]
        qf, kk, lf = [], [], []
        for rows in chunks:
            qx = q_ref[0, rows, :].astype(F32)
            qf.append(_silu(qx) * (HG_DK ** -0.5))
            forget = lb + (1.0 - lb) * jax.nn.sigmoid(f_ref[0, rows, :].astype(F32))
            kk.append(1.0 - forget)
            lf.append(jnp.log2(forget))
        bb = [_dot(tri3, jnp.concatenate(_split3(x), axis=0)) for x in lf]
        qe, ke, qb, kd, dlast = [], [], [], [], []
        need = jnp.zeros((1, HG_DK), F32)
        for c in range(HG_GROUP):
            b = bb[c]
            b_mid = b[half - 1:half, :]
            b_last = b[c_len - 1:c_len, :]
            if not exact:
                qe.append((qf[c] * jnp.exp2(b - b_mid)).astype(BF16))
                ke.append((kk[c] * jnp.exp2(b_mid - b)).astype(BF16))
                need = jnp.maximum(need, jnp.maximum(b[0:1, :] - b_mid, b_mid - b_last))
            qb.append((qf[c] * jnp.exp2(b)).astype(BF16))
            kd.append((kk[c] * jnp.exp2(b_last - b)).astype(BF16))
            dlast.append(jnp.exp2(b_last))
        vv = [i_ref[0, rows, :] for rows in chunks]
        if exact:
            aa = [_hgrn_intra_exact(qf[c], kk[c], bb[c], gsel_ref) for c in range(HG_GROUP)]
        else:
            aa = [_dot_nt(qe[c], ke[c]) for c in range(HG_GROUP)]
        kv = [_dot_tn(vv[c], kd[c]) for c in range(HG_GROUP)]
        o_intra = [_dot(jnp.where(causal, aa[c], 0.0).astype(BF16), vv[c]) for c in range(HG_GROUP)]
        states = [state]
        for c in range(HG_GROUP):
            states.append(states[c] * dlast[c] + kv[c])
        o_inter = [_dot_nt(qb[c], states[c].astype(BF16)) for c in range(HG_GROUP)]
        for c, rows in enumerate(chunks):
            gx = g_ref[0, rows, :].astype(F32)
            y = _rms(o_intra[c] + o_inter[c], gn_ref[...]) * _silu(gx)
            o_ref[0, rows, :] = y.astype(o_ref.dtype)
        return states[HG_GROUP], need

    def run_all(exact):
        state = st0_scr[...]
        need = jnp.zeros((1, HG_DK), F32)
        for first in range(0, n_chunks, HG_GROUP):
            state, need_g = run_group(first, state, exact)
            need = jnp.maximum(need, need_g)
        return state, need

    st0_scr[...] = st_ref[...]
    state, need = run_all(exact=False)
    st_ref[...] = state

    @pl.when(jnp.max(need) > HG_EXP2_LIMIT)
    def _():
        run_all(exact=True)


def _hgrn(proj, lb_logits, gnorm, bsz, seq):
    n_chunks = HG_TOK // HG_CHUNK
    tri = np.tril(np.ones((HG_CHUNK, HG_CHUNK), np.float32))
    tri3 = jnp.asarray(np.concatenate([tri, tri, tri], axis=1), BF16)
    t_idx = np.arange(HG_CHUNK)
    gsel = np.zeros((len(_hgrn_levels()), HG_CHUNK, HG_CHUNK), np.float32)
    for lvl, h in enumerate(_hgrn_levels()):
        gsel[lvl, t_idx, (t_idx // (2 * h)) * (2 * h) + h - 1] = 1.0
    gsel3 = jnp.asarray(np.concatenate([gsel, gsel, gsel], axis=2), BF16)
    hh = HG_HEADS
    blk = (1, HG_TOK, HG_DK)
    head_cols = lambda off: pl.BlockSpec(blk, lambda b, h, t: (b, t, off // HG_DK + h))
    return pl.pallas_call(
        functools.partial(_hgrn_kernel, n_chunks=n_chunks),
        grid=(bsz, hh, seq // HG_TOK),
        in_specs=[head_cols(P_HQ), head_cols(P_HI), head_cols(P_HG), head_cols(P_HF),
                  pl.BlockSpec((2, HG_DK), lambda b, h, t: (0, h)),
                  pl.BlockSpec((1, HG_DV), lambda b, h, t: (0, 0)),
                  pl.BlockSpec((HG_CHUNK, 3 * HG_CHUNK), lambda b, h, t: (0, 0)),
                  pl.BlockSpec(gsel3.shape, lambda b, h, t: (0, 0, 0))],
        out_specs=pl.BlockSpec(blk, lambda b, h, t: (b, t, h)),
        out_shape=jax.ShapeDtypeStruct((bsz, seq, HG_WIDTH), BF16),
        scratch_shapes=[pltpu.VMEM((HG_DV, HG_DK), F32), pltpu.VMEM((HG_DV, HG_DK), F32)],
        compiler_params=_cparams(("parallel", "parallel", "arbitrary")),
    )(proj, proj, proj, proj, lb_logits, gnorm, tri3, gsel3)


def _rope_kernel(x_ref, c_ref, s1_ref, s2_ref, o_ref, *, scale):
    width = x_ref.shape[-1]
    c = c_ref[...]
    s1 = s1_ref[...]
    s2 = s2_ref[...]
    for j in range(width // LANES):
        cols = slice(j * LANES, (j + 1) * LANES)
        x = x_ref[0, :, cols].astype(F32)
        if scale != 1.0:
            x = x * scale
        o_ref[0, :, cols] = _rope_lanes(x, c, s1, s2).astype(o_ref.dtype)


def _rope(src, col_block, width, tables, scale, bsz, seq):
    tt = min(TOK_TILE, seq)
    tab = pl.BlockSpec((tt, LANES), lambda b, t: (t, 0))
    return pl.pallas_call(
        functools.partial(_rope_kernel, scale=scale),
        grid=(bsz, seq // tt),
        in_specs=[pl.BlockSpec((1, tt, width), lambda b, t: (b, t, col_block)), tab, tab, tab],
        out_specs=pl.BlockSpec((1, tt, width), lambda b, t: (b, t, 0)),
        out_shape=jax.ShapeDtypeStruct((bsz, seq, width), BF16),
        compiler_params=_cparams(("parallel", "parallel")),
    )(src, *tables)


def _rope_lanes(x, c, s1, s2):
    return x * c + pltpu.roll(x, LANES - ROPE_DIM // 2, 1) * s1 + pltpu.roll(x, ROPE_DIM // 2, 1) * s2


def _kv_prep_kernel(k_ref, v_ref, c_ref, s1_ref, s2_ref, ko_ref, vo_ref, *, pad_blocks, with_block_id):
    tt = k_ref.shape[1]
    step = pl.program_id(1)
    is_pad = step < pad_blocks
    pos = (step - pad_blocks) * tt + lax.broadcasted_iota(jnp.int32, (tt, NSA_DH), 0)
    lane = lax.broadcasted_iota(jnp.int32, (tt, NSA_DH), 1)
    if with_block_id:
        extra = jnp.where(pos // SEL_BLOCK == lane, 1.0, 0.0)
    else:
        extra = jnp.where((lane == 0) & is_pad, 1.0, 0.0)
    c, s1, s2 = c_ref[...], s1_ref[...], s2_ref[...]
    vt = v_ref[0].astype(F32).T
    for j in range(NSA_KVW // LANES):
        y = _rope_lanes(k_ref[0, :, j * LANES:(j + 1) * LANES].astype(F32), c, s1, s2)
        y = jnp.where(is_pad, 0.0, y)
        for g2 in range(LANES // NSA_DH):
            g = j * (LANES // NSA_DH) + g2
            ko_ref[0, g] = jnp.concatenate([y[:, g2 * NSA_DH:(g2 + 1) * NSA_DH], extra], axis=1).astype(BF16)
    for g in range(NSA_KV):
        vo_ref[0, g, 0:NSA_DH, :] = jnp.where(is_pad, 0.0, vt[g * NSA_DH:(g + 1) * NSA_DH]).astype(BF16)
        vo_ref[0, g, NSA_DH:NSA_DH + SUBLANES_BF16, :] = jnp.ones((SUBLANES_BF16, tt), BF16)


def _kv_prep(proj, k_block, v_block, tables, pad_rows, with_block_id, bsz, seq):
    tt = min(TOK_TILE, seq)
    pad_blocks = pad_rows // tt
    assert pad_blocks * tt == pad_rows
    src_t = lambda t: jnp.maximum(t - pad_blocks, 0)
    tab = pl.BlockSpec((tt, LANES), lambda b, t: (src_t(t), 0))
    rows = seq + pad_rows
    return pl.pallas_call(
        functools.partial(_kv_prep_kernel, pad_blocks=pad_blocks, with_block_id=with_block_id),
        grid=(bsz, rows // tt),
        in_specs=[pl.BlockSpec((1, tt, NSA_KVW), lambda b, t: (b, src_t(t), k_block)),
                  pl.BlockSpec((1, tt, NSA_KVW), lambda b, t: (b, src_t(t), v_block)), tab, tab, tab],
        out_specs=[pl.BlockSpec((1, NSA_KV, tt, 2 * NSA_DH), lambda b, t: (b, 0, t, 0)),
                   pl.BlockSpec((1, NSA_KV, NSA_DH + SUBLANES_BF16, tt), lambda b, t: (b, 0, 0, t))],
        out_shape=[jax.ShapeDtypeStruct((bsz, NSA_KV, rows, 2 * NSA_DH), BF16),
                   jax.ShapeDtypeStruct((bsz, NSA_KV, NSA_DH + SUBLANES_BF16, rows), BF16)],
        compiler_params=_cparams(("parallel", "parallel")),
    )(proj, proj, *tables)


def _rope_tables(seq):
    half = ROPE_DIM // 2
    inv = jnp.asarray(ROPE_THETA ** (-np.arange(half) * 2.0 / ROPE_DIM), F32)
    ang = jnp.arange(seq, dtype=F32)[:, None] * inv[None, :]
    cos, sin = jnp.cos(ang), jnp.sin(ang)
    lane = np.arange(LANES) % NSA_DH
    pick = lane % half
    cos_l = cos[:, pick]
    sin_l = sin[:, pick]
    first = jnp.asarray(lane < half)
    second = jnp.asarray((lane >= half) & (lane < ROPE_DIM))
    c = jnp.where(first | second, cos_l, 1.0)
    s1 = jnp.where(first, -sin_l, 0.0)
    s2 = jnp.where(second, sin_l, 0.0)
    return c, s1, s2


def _gelu_tanh(x):
    return 0.5 * x * (1.0 + jnp.tanh(np.sqrt(2.0 / np.pi).astype(np.float32) * (x + 0.044715 * (x * x * x))))


def _compress_kernel(xk_ref, xv_ref, pek_ref, pev_ref, wk1_ref, wk2_ref, wv1_ref, wv2_ref, ok_ref, ov_ref):
    half_w = CMP_STRIDE * NSA_DH

    def one(x_ref, pe_ref, w1_ref, w2_ref):
        x = x_ref[0, 0]
        n_slab = x.shape[0]
        p0 = _dot(x, w1_ref[0:half_w, :])
        p1 = _dot(x, w1_ref[half_w:2 * half_w, :])
        pe = jnp.broadcast_to(pe_ref[...], (8, 2 * half_w)).astype(BF16)
        pw = _dot(pe, w1_ref[...])[0:1, :]
        h = p0 + pltpu.roll(p1, n_slab - 1, 0) + pw
        return _dot(_gelu_tanh(h).astype(BF16), w2_ref[...])

    ok_ref[0, 0] = (one(xk_ref, pek_ref, wk1_ref, wk2_ref) * (ATT_SCALE * LOG2E)).astype(ok_ref.dtype)
    ov_ref[0, 0] = one(xv_ref, pev_ref, wv1_ref, wv2_ref).astype(ov_ref.dtype)


def _compress(xk, xv, pek, pev, wk1, wk2, wv1, wv2):
    bsz, grp, n_slab, wdt = xk.shape
    xs = pl.BlockSpec((1, 1, n_slab, wdt), lambda b, g: (b, g, 0, 0))
    full = lambda a: pl.BlockSpec(a.shape, lambda b, g: (0,) * a.ndim)
    osd = jax.ShapeDtypeStruct((bsz, grp, n_slab, NSA_DH), BF16)
    os_ = pl.BlockSpec((1, 1, n_slab, NSA_DH), lambda b, g: (b, g, 0, 0))
    return pl.pallas_call(
        _compress_kernel,
        grid=(bsz, grp),
        in_specs=[xs, xs, full(pek), full(pev), full(wk1), full(wk2), full(wv1), full(wv2)],
        out_specs=[os_, os_],
        out_shape=[osd, osd],
        compiler_params=_cparams(("parallel", "parallel")),
    )(xk, xv, pek, pev, wk1, wk2, wv1, wv2)


def _attn_kernel(nq_ref, qr_ref, kcmp_ref, vcmpt_ref, ks_ref, vst_ref, kw_ref, vwt_ref, ng_ref, ovt_ref,
                 o_ref, gt_scr, qaug_scr, *, seq):
    qn = ATT_Q
    grp = NSA_GROUP
    ln = grp * qn
    gw = grp * NSA_DH
    n_sel = seq // SEL_BLOCK
    top = min(SEL_TOPK, n_sel)
    groups = range(ATT_GROUPS)
    g_first = pl.program_id(1) * ATT_GROUPS
    q0 = pl.multiple_of(pl.program_id(2) * qn, qn)

    def stack_heads(x):
        return jnp.concatenate([x[:, r * NSA_DH:(r + 1) * NSA_DH] for r in range(grp)], axis=0)

    def add_per_head(s, bias):
        return jnp.concatenate([s[:, r * qn:(r + 1) * qn] + bias for r in range(grp)], axis=1)

    t_lane = q0 + (lax.broadcasted_iota(jnp.int32, (1, ln), 1) & (qn - 1))
    t_q = q0 + lax.broadcasted_iota(jnp.int32, (1, qn), 1)

    n_slab = kcmp_ref.shape[2]
    wlen = WINDOW + qn
    pad_col = jnp.where(lax.broadcasted_iota(jnp.int32, (ln, NSA_DH), 1) == 0, NEG, 0.0).astype(BF16)
    qs = [stack_heads(nq_ref[0, :, g * gw:(g + 1) * gw]) for g in groups]
    qrs = [stack_heads(qr_ref[0, :, g * gw:(g + 1) * gw]) for g in groups]

    def window_scores():
        return [_dot_nt(kw_ref[0, g, pl.ds(q0, wlen), :], jnp.concatenate([qrs[g], pad_col], axis=1))
                for g in groups]

    def diag_scores():
        return [_dot_nt(ks_ref[0, g, pl.ds(q0, qn), :][:, 0:NSA_DH], qrs[g]) for g in groups]

    def window_branch(sw):
        i_loc = lax.broadcasted_iota(jnp.int32, (qn, 1), 0)
        j_loc = lax.broadcasted_iota(jnp.int32, (1, qn), 1)
        win_lo = jnp.where(i_loc > j_loc, 0.0, NEG)
        win_hi = jnp.where(i_loc <= j_loc, 0.0, NEG)
        ow = []
        for g in groups:
            s = jnp.concatenate([add_per_head(sw[g][0:qn], win_lo), sw[g][qn:WINDOW],
                                 add_per_head(sw[g][WINDOW:wlen], win_hi)], axis=0)
            pw = jnp.exp2(s - _colmax(s))
            rw = _dot(vwt_ref[0, g, :, pl.ds(q0, wlen)], pw.astype(BF16))
            ow.append(rw[0:NSA_DH] / rw[NSA_DH:NSA_DH + 1])
        return ow

    def diag_tile(sd):
        kpos = q0 + lax.broadcasted_iota(jnp.int32, (qn, 1), 0)
        diag_bias = jnp.where(kpos <= t_q, 0.0, NEG)
        init = []
        for g in groups:
            s = add_per_head(sd[g], diag_bias)
            m0 = _colmax(s)
            init += [m0, _dot(vst_ref[0, g, :, pl.ds(q0, qn)], jnp.exp2(s - m0).astype(BF16))]
        return init

    gt_scr[...] = jax.nn.sigmoid(ng_ref[0].astype(F32)).T

    def gate_row(branch, g):
        return jnp.concatenate(
            [gt_scr[pl.ds(branch * NSA_HEADS + (g_first + g) * grp + r, 1), :] for r in range(grp)], axis=1)

    gate_sel = [gate_row(1, g) for g in groups]

    def before_sweep(n_spans):
        rows = min(n_spans * ATT_SPAN // CMP_STRIDE, n_slab)
        nb = min(n_spans * ATT_SPAN // SEL_BLOCK, n_sel)
        cend = lax.broadcasted_iota(jnp.int32, (rows, 1), 0) * CMP_STRIDE + (CMP_LEN - 1)
        cmp_bias = jnp.where(cend <= t_q, 0.0, NEG)
        ovt = ovt_ref[0:nb, 0:rows]
        sc = [_dot_nt(kcmp_ref[0, g, 0:rows, :], qs[g]) for g in groups]
        sw = window_scores()
        sd = diag_scores()
        imp, oc = [], []
        for g in groups:
            sm = add_per_head(sc[g], cmp_bias)
            e = jnp.exp2(sm - _colmax(sm))
            den = jnp.sum(e, axis=0, keepdims=True)
            p = e * jnp.where(t_lane >= CMP_LEN - 1, 1.0 / den, 0.0)
            oc.append(_dot(vcmpt_ref[0, g, :, 0:rows], p.astype(BF16)))
            psum = p[:, 0:qn]
            for r in range(1, grp):
                psum = psum + p[:, r * qn:(r + 1) * qn]
            hi, mid, lo = _split3(psum)
            imp.append(_dot(ovt, hi) + _dot(ovt, mid) + _dot(ovt, lo))
        ow = window_branch(sw)
        init = diag_tile(sd)
        out_cw = [gate_row(0, g) * oc[g] + gate_row(2, g) * ow[g] for g in groups]

        jidx = lax.broadcasted_iota(jnp.int32, (nb, qn), 0)
        tq = q0 + lax.broadcasted_iota(jnp.int32, (nb, qn), 1)
        cur = tq // SEL_BLOCK
        forced = (jidx == 0) | (jidx == cur) | (jidx == cur - 1)
        causal_blk = jidx * SEL_BLOCK <= tq
        rows8 = 8
        jrow = lax.broadcasted_iota(jnp.int32, (rows8, qn), 0)
        for g in groups:
            val = jnp.where(forced, -2.0, imp[g])
            val = jnp.where(causal_blk, val, -1.0)
            parts = [val[v * rows8:(v + 1) * rows8] for v in range(nb // rows8)]
            counts = [jnp.zeros((rows8, qn), F32) for _ in parts]
            for jp in range(nb):
                other = jnp.broadcast_to(val[jp:jp + 1, :], (rows8, qn))
                for v, part in enumerate(parts):
                    ge = jnp.where(other >= part, 1.0, 0.0)
                    gt = jnp.where(other > part, 1.0, 0.0)
                    if v * rows8 > jp:
                        before = ge
                    elif (v + 1) * rows8 <= jp:
                        before = gt
                    else:
                        before = jnp.where(jrow > jp - v * rows8, ge, gt)
                    counts[v] = counts[v] + before
            rank = jnp.concatenate(counts, axis=0)
            chosen = forced | (rank < float(max(top - 3, 0)))
            selb = jnp.where(chosen & causal_blk, 0.0, NEG)
            selb_t = jnp.concatenate([selb, jnp.zeros((LANES - nb, qn), F32)], axis=0).T
            selb_t = selb_t[:, 0:NSA_DH].astype(BF16)
            qaug_scr[g] = jnp.concatenate([qrs[g], jnp.concatenate([selb_t] * grp, axis=0)], axis=1)
        return tuple(out_cw) + tuple(init)

    n_variants = (seq + ATT_SPAN - 1) // ATT_SPAN
    staged = lax.switch(q0 // ATT_SPAN, [functools.partial(before_sweep, k + 1) for k in range(n_variants)])
    out_cw, init = staged[:ATT_GROUPS], staged[ATT_GROUPS:]

    kt_len = ATT_KT
    sub = kt_len // ATT_SUBTILES

    def sweep_block(k_base, n_sub, carry):
        state = list(carry)
        items = [(j, g) for j in range(n_sub) for g in groups]
        k0s = [pl.multiple_of(k_base + j * sub, sub) for j in range(n_sub)]
        score = lambda j, g: _dot_nt(ks_ref[0, g, pl.ds(k0s[j], sub), :], qaug_scr[g])
        ahead = min(ATT_LOOKAHEAD * ATT_GROUPS, len(items))
        scores = [score(*it) for it in items[:ahead]]
        for n, (j, g) in enumerate(items):
            s = scores[n]
            m_run, acc = state[2 * g], state[2 * g + 1]
            m_new = jnp.maximum(m_run, _colmax(s))
            alpha = jnp.exp2(m_run - m_new)
            pt = jnp.exp2(s - m_new).astype(BF16)
            state[2 * g + 1] = alpha * acc + _dot(vst_ref[0, g, :, pl.ds(k0s[j], sub)], pt)
            state[2 * g] = m_new
            if n + ahead < len(items):
                scores.append(score(*items[n + ahead]))
        return tuple(state)

    n_whole = q0 // kt_len
    carry = lax.fori_loop(0, n_whole, lambda kt, cy: sweep_block(kt * kt_len, ATT_SUBTILES, cy), tuple(init))
    tails = [functools.partial(sweep_block, n_whole * kt_len, r) for r in range(ATT_SUBTILES)]
    final = lax.switch((q0 - n_whole * kt_len) // sub, tails, carry)

    for g in groups:
        acc = final[2 * g + 1]
        out_t = out_cw[g] + gate_sel[g] * (acc[0:NSA_DH] / acc[NSA_DH:NSA_DH + 1])
        pieces = []
        for r2 in range(grp // 2):
            pair = jnp.concatenate([out_t[:, (2 * r2) * qn:(2 * r2 + 1) * qn],
                                    out_t[:, (2 * r2 + 1) * qn:(2 * r2 + 2) * qn]], axis=0)
            pieces.append(pair.T)
        o_ref[0, :, g * gw:(g + 1) * gw] = jnp.concatenate(pieces, axis=1).astype(o_ref.dtype)


def _attention(proj, q_rot, k_cmp, v_cmp_t, ks, vs_t, kw, vw_t, ov_t, bsz, seq):
    qn = ATT_Q
    gw = ATT_GROUPS * NSA_GROUP * NSA_DH
    n_slab = k_cmp.shape[2]
    n_sel = seq // SEL_BLOCK
    qspec = pl.BlockSpec((1, qn, gw), lambda b, g, i: (b, i, g))
    per_group = lambda a: pl.BlockSpec((1, ATT_GROUPS) + a.shape[2:], lambda b, g, i: (b, g, 0, 0))
    return pl.pallas_call(
        functools.partial(_attn_kernel, seq=seq),
        grid=(bsz, NSA_KV // ATT_GROUPS, seq // qn),
        in_specs=[pl.BlockSpec((1, qn, gw), lambda b, g, i: (b, i, P_NQ // gw + g)), qspec,
                  per_group(k_cmp), per_group(v_cmp_t), per_group(ks), per_group(vs_t),
                  per_group(kw), per_group(vw_t),
                  pl.BlockSpec((1, qn, LANES), lambda b, g, i: (b, i, P_NG // LANES)),
                  pl.BlockSpec((n_sel, n_slab), lambda b, g, i: (0, 0))],
        out_specs=qspec,
        out_shape=jax.ShapeDtypeStruct((bsz, seq, NSA_WIDTH), BF16),
        scratch_shapes=[pltpu.VMEM((LANES, qn), F32),
                        pltpu.VMEM((ATT_GROUPS, NSA_GROUP * qn, 2 * NSA_DH), BF16)],
        compiler_params=_cparams(("parallel", "parallel", "arbitrary")),
    )(proj, q_rot, k_cmp, v_cmp_t, ks, vs_t, kw, vw_t, proj, ov_t)


def _merge_kernel(x_ref, ya_ref, yb_ref, ga_ref, gb_ref, wa_ref, wb_ref, wo_ref, gn_ref, o_ref):
    ga = jax.nn.sigmoid(ga_ref[...].astype(F32))
    gb = jax.nn.sigmoid(gb_ref[...].astype(F32))
    merged = ga * _dot(ya_ref[...], wa_ref[...]) + gb * _dot(yb_ref[...], wb_ref[...])
    mo = _dot(merged.astype(BF16), wo_ref[...])
    o_ref[...] = x_ref[...] + _rms(mo, gn_ref[...])


def _merge(x, ya, yb, gates, wa, wb, wo, gn):
    n, d = x.shape
    tm = min(TOK_TILE, n)
    row = lambda c: pl.BlockSpec((tm, d), lambda i: (i, c))
    wsp = pl.BlockSpec((d, d), lambda i: (0, 0))
    return pl.pallas_call(
        _merge_kernel,
        grid=(n // tm,),
        in_specs=[row(0), row(0), row(0), row(P_GA // d), row(P_GB // d), wsp, wsp, wsp,
                  pl.BlockSpec((1, d), lambda i: (0, 0))],
        out_specs=row(0),
        out_shape=jax.ShapeDtypeStruct((n, d), F32),
        compiler_params=_cparams(("parallel",)),
    )(x, ya, yb, gates, gates, wa, wb, wo, gn)


def _mlp_kernel(h_ref, gpre_ref, wu_ref, wd_ref, gpost_ref, o_ref, v_scr, acc_scr):
    j = pl.program_id(1)

    @pl.when(j == 0)
    def _():
        v_scr[...] = _rms(h_ref[...], gpre_ref[...]).astype(BF16)
        acc_scr[...] = jnp.zeros_like(acc_scr)

    up = _dot(v_scr[...], wu_ref[...])
    act = jnp.square(jnp.maximum(up, 0.0))
    acc_scr[...] += _dot(act.astype(BF16), wd_ref[...])

    @pl.when(j == pl.num_programs(1) - 1)
    def _():
        o_ref[...] = h_ref[...] + _rms(acc_scr[...], gpost_ref[...])


def _mlp(h, gpre, wu, wd, gpost):
    n, d = h.shape
    dff = wu.shape[1]
    tm = min(2 * TOK_TILE, n)
    tf = 1024
    vec = pl.BlockSpec((1, d), lambda i, j: (0, 0))
    return pl.pallas_call(
        _mlp_kernel,
        grid=(n // tm, dff // tf),
        in_specs=[pl.BlockSpec((tm, d), lambda i, j: (i, 0)), vec,
                  pl.BlockSpec((d, tf), lambda i, j: (0, j)),
                  pl.BlockSpec((tf, d), lambda i, j: (j, 0)), vec],
        out_specs=pl.BlockSpec((tm, d), lambda i, j: (i, 0)),
        out_shape=jax.ShapeDtypeStruct((n, d), F32),
        scratch_shapes=[pltpu.VMEM((tm, d), BF16), pltpu.VMEM((tm, d), F32)],
        compiler_params=_cparams(("parallel", "arbitrary")),
    )(h, gpre, wu, wd, gpost)


def _ple_kernel(h_ref, p_ref, wp_ref, wg_ref, gn_ref, o_ref):
    h = h_ref[...]
    e = _dot(p_ref[...].astype(BF16), wp_ref[...]) * jax.nn.sigmoid(_dot(h.astype(BF16), wg_ref[...]))
    o_ref[...] = h + _rms(e, gn_ref[...])


def _ple(h, p, wp, wg, gn):
    n, d = h.shape
    pd = p.shape[1]
    tm = min(2 * TOK_TILE, n)
    return pl.pallas_call(
        _ple_kernel,
        grid=(n // tm,),
        in_specs=[pl.BlockSpec((tm, d), lambda i: (i, 0)),
                  pl.BlockSpec((tm, pd), lambda i: (i, 0)),
                  pl.BlockSpec((pd, d), lambda i: (0, 0)),
                  pl.BlockSpec((d, d), lambda i: (0, 0)),
                  pl.BlockSpec((1, d), lambda i: (0, 0))],
        out_specs=pl.BlockSpec((tm, d), lambda i: (i, 0)),
        out_shape=jax.ShapeDtypeStruct((n, d), F32),
        compiler_params=_cparams(("parallel",)),
    )(h, p, wp, wg, gn)


def _overlap_t(seq):
    n_slab = seq // CMP_STRIDE
    n_cmp = (seq - CMP_LEN) // CMP_STRIDE + 1
    n_sel = seq // SEL_BLOCK
    start = np.arange(n_slab) * CMP_STRIDE
    end = start + CMP_LEN - 1
    sel_start = np.arange(n_sel) * SEL_BLOCK
    ov = (start[None, :] < sel_start[:, None] + SEL_BLOCK) & (end[None, :] >= sel_start[:, None])
    ov = ov & (np.arange(n_slab)[None, :] < n_cmp)
    return jnp.asarray(ov.astype(np.float32), BF16)


def kernel(x, p, w_in, w_branch_a, w_branch_b, w_out, norm_pre_mix, norm_post_mix, norm_pre_mlp,
           norm_post_mlp, hg_lb_logits, hg_gnorm, cmp_pe_k, cmp_pe_v, cmp_wk1, cmp_wk2, cmp_wv1, cmp_wv2,
           w_up, w_down, w_ple, w_ple_gate, norm_ple):
    bsz, seq, d = x.shape
    depth = w_in.shape[0]
    assert depth == 1 and hg_lb_logits.shape[0] == 2 and d == D_MODEL
    assert seq % ATT_KT == 0 and seq >= WINDOW + ATT_Q and seq // SEL_BLOCK <= NSA_DH
    assert seq % HG_TOK == 0 and (HG_TOK // HG_CHUNK) % HG_GROUP == 0
    assert ATT_Q in (SEL_BLOCK, 2 * SEL_BLOCK)
    assert ATT_Q % (ATT_KT // ATT_SUBTILES) == 0
    n = bsz * seq
    x2 = x.reshape(n, d)
    row = lambda a: a.reshape(1, -1)

    w = w_in[0]
    s_hq, s_hf, s_hi, s_hg = 0, HG_WIDTH, 2 * HG_WIDTH, 3 * HG_WIDTH
    s_nq = 4 * HG_WIDTH
    s_ng = s_nq + NSA_WIDTH + 6 * NSA_KVW
    s_ga = s_ng + 3 * NSA_HEADS
    col = lambda a, b: w[:, a:b]
    w_all = jnp.concatenate(
        [col(s_ga, s_ga + 2 * D_MODEL), col(s_hq, s_hf), col(s_hi, s_hg), col(s_hg, s_nq), col(s_hf, s_hi),
         col(s_nq, s_ga), jnp.zeros((d, LANES - 3 * NSA_HEADS), F32)], axis=1).astype(BF16)
    assert w_all.shape[1] == P_WIDTH and P_WIDTH % (P_COL_TILES * LANES) == 0
    proj2 = _rms_matmul(x2, row(norm_pre_mix[0]), w_all, BF16, P_WIDTH // P_COL_TILES)
    proj = proj2.reshape(bsz, seq, P_WIDTH)

    y_a = _hgrn(proj, hg_lb_logits, row(hg_gnorm[0]), bsz, seq)

    tables = _rope_tables(seq)
    kvb = P_KV // NSA_KVW
    q_rot = _rope(proj, P_NQ // NSA_WIDTH, NSA_WIDTH, tables, ATT_SCALE * LOG2E, bsz, seq)
    ks_aug, vs_t = _kv_prep(proj, kvb + 2, kvb + 3, tables, 0, True, bsz, seq)
    kw_aug, vw_t = _kv_prep(proj, kvb + 4, kvb + 5, tables, WINDOW, False, bsz, seq)
    kv_slice = lambda i: proj[:, :, P_KV + i * NSA_KVW: P_KV + (i + 1) * NSA_KVW]
    slabs = lambda t: t.reshape(bsz, seq, NSA_KV, NSA_DH).transpose(0, 2, 1, 3).reshape(
        bsz, NSA_KV, seq // CMP_STRIDE, CMP_STRIDE * NSA_DH)
    k_cmp, v_cmp = _compress(
        slabs(kv_slice(0)), slabs(kv_slice(1)), cmp_pe_k[0].reshape(1, -1), cmp_pe_v[0].reshape(1, -1),
        cmp_wk1[0].astype(BF16), cmp_wk2[0].astype(BF16), cmp_wv1[0].astype(BF16), cmp_wv2[0].astype(BF16))
    y_b = _attention(proj, q_rot, k_cmp, v_cmp.transpose(0, 1, 3, 2), ks_aug, vs_t, kw_aug, vw_t,
                     _overlap_t(seq), bsz, seq)

    h1 = _merge(x2, y_a.reshape(n, -1), y_b.reshape(n, -1), proj2, w_branch_a[0].astype(BF16),
                w_branch_b[0].astype(BF16), w_out[0].astype(BF16), row(norm_post_mix[0]))
    h2 = _mlp(h1, row(norm_pre_mlp[0]), w_up[0].astype(BF16), w_down[0].astype(BF16), row(norm_post_mlp[0]))
    h3o = _ple(h2, p[0].reshape(n, -1), w_ple[0].astype(BF16), w_ple_gate[0].astype(BF16), row(norm_ple[0]))
    return h3o.reshape(bsz, seq, d)
```

```python
import functools

import numpy as np
import jax
import jax.numpy as jnp
from jax import lax
from jax.experimental import pallas as pl
from jax.experimental.pallas import tpu as pltpu

F32 = jnp.float32
BF16 = jnp.bfloat16

D_MODEL = 1024
HG_HEADS = 8
HG_DK = 128
HG_DV = 128
HG_WIDTH = HG_HEADS * HG_DK
HG_CHUNK = 64
NSA_HEADS = 16
NSA_KV = 4
NSA_GROUP = NSA_HEADS // NSA_KV
NSA_DH = 64
NSA_WIDTH = NSA_HEADS * NSA_DH
NSA_KVW = NSA_KV * NSA_DH
CMP_LEN = 32
CMP_STRIDE = 16
CMP_HIDDEN = 256
SEL_BLOCK = 64
SEL_TOPK = 16
WINDOW = 512
ROPE_THETA = 500000.0
ROPE_DIM = NSA_DH // 4
D_FF = 4 * D_MODEL
EPS = 1e-6
NEG = -1e30
ATT_SCALE = NSA_DH ** -0.5
LOG2E = float(np.log2(np.e))

LANES = 128
SUBLANES_BF16 = 16
VMEM_LIMIT = 56 * 1024 * 1024

TOK_TILE = 512
ATT_Q = 128
ATT_GROUPS = 4
ATT_KT = 512
ATT_SUBTILES = 4
ATT_LOOKAHEAD = 3
ATT_SPAN = 2048
HG_TOK = 1024
HG_GROUP = 16

P_GA, P_GB = 0, D_MODEL
P_HQ, P_HI, P_HG, P_HF = (2 * D_MODEL + i * HG_WIDTH for i in range(4))
P_NQ = P_HF + HG_WIDTH
P_KV = P_NQ + NSA_WIDTH
P_NG = P_KV + 6 * NSA_KVW
P_WIDTH = P_NG + LANES
P_COL_TILES = 3
HG_EXP2_LIMIT = 115.0


def _cparams(sem):
    return pltpu.CompilerParams(dimension_semantics=sem, vmem_limit_bytes=VMEM_LIMIT)


def _rms(x, gain):
    ms = jnp.mean(x * x, axis=-1, keepdims=True)
    return (x * lax.rsqrt(ms + EPS)) * gain


def _silu(x):
    return x * (0.5 * jnp.tanh(0.5 * x) + 0.5)


def _dot(a, b):
    return jnp.dot(a, b, preferred_element_type=F32)


def _dot_nt(a, b):
    return lax.dot_general(a, b, (((1,), (1,)), ((), ())), preferred_element_type=F32)


def _dot_tn(a, b):
    return lax.dot_general(a, b, (((0,), (0,)), ((), ())), preferred_element_type=F32)


def _colmax(s):
    parts = [s[i:i + 64] for i in range(0, s.shape[0], 64)] if s.shape[0] % 64 == 0 else [s]
    while len(parts) > 1:
        parts = [jnp.maximum(parts[i], parts[i + 1]) if i + 1 < len(parts) else parts[i]
                 for i in range(0, len(parts), 2)]
    return jnp.max(parts[0], axis=0, keepdims=True)


def _split3(x):
    hi = x.astype(BF16)
    r1 = x - hi.astype(F32)
    mid = r1.astype(BF16)
    lo = (r1 - mid.astype(F32)).astype(BF16)
    return hi, mid, lo


def _rms_matmul_kernel(x_ref, g_ref, w_ref, o_ref, u_scr):
    @pl.when(pl.program_id(1) == 0)
    def _():
        u_scr[...] = _rms(x_ref[...], g_ref[...]).astype(BF16)

    o_ref[...] = _dot(u_scr[...], w_ref[...]).astype(o_ref.dtype)


def _rms_matmul(x, gain, w, out_dtype, tn):
    n, d = x.shape
    wd = w.shape[1]
    tm = min(2 * TOK_TILE, n)
    return pl.pallas_call(
        _rms_matmul_kernel,
        grid=(n // tm, wd // tn),
        in_specs=[pl.BlockSpec((tm, d), lambda i, j: (i, 0)),
                  pl.BlockSpec((1, d), lambda i, j: (0, 0)),
                  pl.BlockSpec((d, tn), lambda i, j: (0, j))],
        out_specs=pl.BlockSpec((tm, tn), lambda i, j: (i, j)),
        out_shape=jax.ShapeDtypeStruct((n, wd), out_dtype),
        scratch_shapes=[pltpu.VMEM((tm, d), BF16)],
        compiler_params=_cparams(("parallel", "arbitrary")),
    )(x, gain, w)


def _hgrn_levels():
    return [HG_CHUNK >> (i + 1) for i in range(HG_CHUNK.bit_length() - 1)]


def _hgrn_intra_exact(qf, kk, b, gsel_ref):
    c_len = HG_CHUNK
    row = lax.broadcasted_iota(jnp.int32, (c_len, c_len), 0)
    col = lax.broadcasted_iota(jnp.int32, (c_len, c_len), 1)
    trow = lax.broadcasted_iota(jnp.int32, (c_len, HG_DK), 0)
    a = jnp.where(row == col, _dot_nt(qf.astype(BF16), kk.astype(BF16)), 0.0)
    b3 = jnp.concatenate(_split3(b), axis=0)
    for lvl, h in enumerate(_hgrn_levels()):
        b_at_ref = _dot(gsel_ref[lvl], b3)
        upper = (trow & (2 * h - 1)) >= h
        qe = jnp.where(upper, qf * jnp.exp2(jnp.minimum(b - b_at_ref, 0.0)), 0.0)
        ke = jnp.where(upper, 0.0, kk * jnp.exp2(jnp.minimum(b_at_ref - b, 0.0)))
        same_block = (row & -(2 * h)) == (col & -(2 * h))
        block = same_block & ((row & (2 * h - 1)) >= h) & ((col & (2 * h - 1)) < h)
        a = a + jnp.where(block, _dot_nt(qe.astype(BF16), ke.astype(BF16)), 0.0)
    return a


def _hgrn_kernel(q_ref, i_ref, g_ref, f_ref, lbl_ref, gn_ref, tri_ref, gsel_ref, o_ref, st_ref, st0_scr, *,
                 n_chunks):
    c_len = HG_CHUNK
    half = c_len // 2

    @pl.when(pl.program_id(2) == 0)
    def _():
        st_ref[...] = jnp.zeros_like(st_ref)

    lg = lbl_ref[...]
    mx = jnp.max(lg, axis=0, keepdims=True)
    ex = jnp.exp(lg - mx)
    lb = ex[0:1, :] / jnp.sum(ex, axis=0, keepdims=True)

    row = lax.broadcasted_iota(jnp.int32, (c_len, c_len), 0)
    col = lax.broadcasted_iota(jnp.int32, (c_len, c_len), 1)
    causal = col <= row
    tri3 = tri_ref[...]

    def run_group(first, state, exact):
        chunks = [slice((first + c) * c_len, (first + c + 1) * c_len) for c in range(HG_GROUP)]
        qf, kk, lf = [], [], []
        for rows in chunks:
            qx = q_ref[0, rows, :].astype(F32)
            qf.append(_silu(qx) * (HG_DK ** -0.5))
            forget = lb + (1.0 - lb) * jax.nn.sigmoid(f_ref[0, rows, :].astype(F32))
            kk.append(1.0 - forget)
            lf.append(jnp.log2(forget))
        bb = [_dot(tri3, jnp.concatenate(_split3(x), axis=0)) for x in lf]
        qe, ke, qb, kd, dlast = [], [], [], [], []
        need = jnp.zeros((1, HG_DK), F32)
        for c in range(HG_GROUP):
            b = bb[c]
            b_mid = b[half - 1:half, :]
            b_last = b[c_len - 1:c_len, :]
            if not exact:
                qe.append((qf[c] * jnp.exp2(b - b_mid)).astype(BF16))
                ke.append((kk[c] * jnp.exp2(b_mid - b)).astype(BF16))
                need = jnp.maximum(need, jnp.maximum(b[0:1, :] - b_mid, b_mid - b_last))
            qb.append((qf[c] * jnp.exp2(b)).astype(BF16))
            kd.append((kk[c] * jnp.exp2(b_last - b)).astype(BF16))
            dlast.append(jnp.exp2(b_last))
        vv = [i_ref[0, rows, :] for rows in chunks]
        if exact:
            aa = [_hgrn_intra_exact(qf[c], kk[c], bb[c], gsel_ref) for c in range(HG_GROUP)]
        else:
            aa = [_dot_nt(qe[c], ke[c]) for c in range(HG_GROUP)]
        kv = [_dot_tn(vv[c], kd[c]) for c in range(HG_GROUP)]
        o_intra = [_dot(jnp.where(causal, aa[c], 0.0).astype(BF16), vv[c]) for c in range(HG_GROUP)]
        states = [state]
        for c in range(HG_GROUP):
            states.append(states[c] * dlast[c] + kv[c])
        o_inter = [_dot_nt(qb[c], states[c].astype(BF16)) for c in range(HG_GROUP)]
        for c, rows in enumerate(chunks):
            gx = g_ref[0, rows, :].astype(F32)
            y = _rms(o_intra[c] + o_inter[c], gn_ref[...]) * _silu(gx)
            o_ref[0, rows, :] = y.astype(o_ref.dtype)
        return states[HG_GROUP], need

    def run_all(exact):
        state = st0_scr[...]
        need = jnp.zeros((1, HG_DK), F32)
        for first in range(0, n_chunks, HG_GROUP):
            state, need_g = run_group(first, state, exact)
            need = jnp.maximum(need, need_g)
        return state, need

    st0_scr[...] = st_ref[...]
    state, need = run_all(exact=False)
    st_ref[...] = state

    @pl.when(jnp.max(need) > HG_EXP2_LIMIT)
    def _():
        run_all(exact=True)


def _hgrn(proj, lb_logits, gnorm, bsz, seq):
    n_chunks = HG_TOK // HG_CHUNK
    tri = np.tril(np.ones((HG_CHUNK, HG_CHUNK), np.float32))
    tri3 = jnp.asarray(np.concatenate([tri, tri, tri], axis=1), BF16)
    t_idx = np.arange(HG_CHUNK)
    gsel = np.zeros((len(_hgrn_levels()), HG_CHUNK, HG_CHUNK), np.float32)
    for lvl, h in enumerate(_hgrn_levels()):
        gsel[lvl, t_idx, (t_idx // (2 * h)) * (2 * h) + h - 1] = 1.0
    gsel3 = jnp.asarray(np.concatenate([gsel, gsel, gsel], axis=2), BF16)
    hh = HG_HEADS
    blk = (1, HG_TOK, HG_DK)
    head_cols = lambda off: pl.BlockSpec(blk, lambda b, h, t: (b, t, off // HG_DK + h))
    return pl.pallas_call(
        functools.partial(_hgrn_kernel, n_chunks=n_chunks),
        grid=(bsz, hh, seq // HG_TOK),
        in_specs=[head_cols(P_HQ), head_cols(P_HI), head_cols(P_HG), head_cols(P_HF),
                  pl.BlockSpec((2, HG_DK), lambda b, h, t: (0, h)),
                  pl.BlockSpec((1, HG_DV), lambda b, h, t: (0, 0)),
                  pl.BlockSpec((HG_CHUNK, 3 * HG_CHUNK), lambda b, h, t: (0, 0)),
                  pl.BlockSpec(gsel3.shape, lambda b, h, t: (0, 0, 0))],
        out_specs=pl.BlockSpec(blk, lambda b, h, t: (b, t, h)),
        out_shape=jax.ShapeDtypeStruct((bsz, seq, HG_WIDTH), BF16),
        scratch_shapes=[pltpu.VMEM((HG_DV, HG_DK), F32), pltpu.VMEM((HG_DV, HG_DK), F32)],
        compiler_params=_cparams(("parallel", "parallel", "arbitrary")),
    )(proj, proj, proj, proj, lb_logits, gnorm, tri3, gsel3)


def _rope_kernel(x_ref, c_ref, s1_ref, s2_ref, o_ref, *, scale):
    width = x_ref.shape[-1]
    c = c_ref[...]
    s1 = s1_ref[...]
    s2 = s2_ref[...]
    for j in range(width // LANES):
        cols = slice(j * LANES, (j + 1) * LANES)
        x = x_ref[0, :, cols].astype(F32)
        if scale != 1.0:
            x = x * scale
        o_ref[0, :, cols] = _rope_lanes(x, c, s1, s2).astype(o_ref.dtype)


def _rope(src, col_block, width, tables, scale, bsz, seq):
    tt = min(TOK_TILE, seq)
    tab = pl.BlockSpec((tt, LANES), lambda b, t: (t, 0))
    return pl.pallas_call(
        functools.partial(_rope_kernel, scale=scale),
        grid=(bsz, seq // tt),
        in_specs=[pl.BlockSpec((1, tt, width), lambda b, t: (b, t, col_block)), tab, tab, tab],
        out_specs=pl.BlockSpec((1, tt, width), lambda b, t: (b, t, 0)),
        out_shape=jax.ShapeDtypeStruct((bsz, seq, width), BF16),
        compiler_params=_cparams(("parallel", "parallel")),
    )(src, *tables)


def _rope_lanes(x, c, s1, s2):
    return x * c + pltpu.roll(x, LANES - ROPE_DIM // 2, 1) * s1 + pltpu.roll(x, ROPE_DIM // 2, 1) * s2


def _kv_prep_kernel(k_ref, v_ref, c_ref, s1_ref, s2_ref, ko_ref, vo_ref, *, pad_blocks, with_block_id):
    tt = k_ref.shape[1]
    step = pl.program_id(1)
    is_pad = step < pad_blocks
    pos = (step - pad_blocks) * tt + lax.broadcasted_iota(jnp.int32, (tt, NSA_DH), 0)
    lane = lax.broadcasted_iota(jnp.int32, (tt, NSA_DH), 1)
    if with_block_id:
        extra = jnp.where(pos // SEL_BLOCK == lane, 1.0, 0.0)
    else:
        extra = jnp.where((lane == 0) & is_pad, 1.0, 0.0)
    c, s1, s2 = c_ref[...], s1_ref[...], s2_ref[...]
    vt = v_ref[0].astype(F32).T
    for j in range(NSA_KVW // LANES):
        y = _rope_lanes(k_ref[0, :, j * LANES:(j + 1) * LANES].astype(F32), c, s1, s2)
        y = jnp.where(is_pad, 0.0, y)
        for g2 in range(LANES // NSA_DH):
            g = j * (LANES // NSA_DH) + g2
            ko_ref[0, g] = jnp.concatenate([y[:, g2 * NSA_DH:(g2 + 1) * NSA_DH], extra], axis=1).astype(BF16)
    for g in range(NSA_KV):
        vo_ref[0, g, 0:NSA_DH, :] = jnp.where(is_pad, 0.0, vt[g * NSA_DH:(g + 1) * NSA_DH]).astype(BF16)
        vo_ref[0, g, NSA_DH:NSA_DH + SUBLANES_BF16, :] = jnp.ones((SUBLANES_BF16, tt), BF16)


def _kv_prep(proj, k_block, v_block, tables, pad_rows, with_block_id, bsz, seq):
    tt = min(TOK_TILE, seq)
    pad_blocks = pad_rows // tt
    assert pad_blocks * tt == pad_rows
    src_t = lambda t: jnp.maximum(t - pad_blocks, 0)
    tab = pl.BlockSpec((tt, LANES), lambda b, t: (src_t(t), 0))
    rows = seq + pad_rows
    return pl.pallas_call(
        functools.partial(_kv_prep_kernel, pad_blocks=pad_blocks, with_block_id=with_block_id),
        grid=(bsz, rows // tt),
        in_specs=[pl.BlockSpec((1, tt, NSA_KVW), lambda b, t: (b, src_t(t), k_block)),
                  pl.BlockSpec((1, tt, NSA_KVW), lambda b, t: (b, src_t(t), v_block)), tab, tab, tab],
        out_specs=[pl.BlockSpec((1, NSA_KV, tt, 2 * NSA_DH), lambda b, t: (b, 0, t, 0)),
                   pl.BlockSpec((1, NSA_KV, NSA_DH + SUBLANES_BF16, tt), lambda b, t: (b, 0, 0, t))],
        out_shape=[jax.ShapeDtypeStruct((bsz, NSA_KV, rows, 2 * NSA_DH), BF16),
                   jax.ShapeDtypeStruct((bsz, NSA_KV, NSA_DH + SUBLANES_BF16, rows), BF16)],
        compiler_params=_cparams(("parallel", "parallel")),
    )(proj, proj, *tables)


def _rope_tables(seq):
    half = ROPE_DIM // 2
    inv = jnp.asarray(ROPE_THETA ** (-np.arange(half) * 2.0 / ROPE_DIM), F32)
    ang = jnp.arange(seq, dtype=F32)[:, None] * inv[None, :]
    cos, sin = jnp.cos(ang), jnp.sin(ang)
    lane = np.arange(LANES) % NSA_DH
    pick = lane % half
    cos_l = cos[:, pick]
    sin_l = sin[:, pick]
    first = jnp.asarray(lane < half)
    second = jnp.asarray((lane >= half) & (lane < ROPE_DIM))
    c = jnp.where(first | second, cos_l, 1.0)
    s1 = jnp.where(first, -sin_l, 0.0)
    s2 = jnp.where(second, sin_l, 0.0)
    return c, s1, s2


def _gelu_tanh(x):
    return 0.5 * x * (1.0 + jnp.tanh(np.sqrt(2.0 / np.pi).astype(np.float32) * (x + 0.044715 * (x * x * x))))


def _compress_kernel(xk_ref, xv_ref, pek_ref, pev_ref, wk1_ref, wk2_ref, wv1_ref, wv2_ref, ok_ref, ov_ref):
    half_w = CMP_STRIDE * NSA_DH

    def one(x_ref, pe_ref, w1_ref, w2_ref):
        x = x_ref[0, 0]
        n_slab = x.shape[0]
        p0 = _dot(x, w1_ref[0:half_w, :])
        p1 = _dot(x, w1_ref[half_w:2 * half_w, :])
        pe = jnp.broadcast_to(pe_ref[...], (8, 2 * half_w)).astype(BF16)
        pw = _dot(pe, w1_ref[...])[0:1, :]
        h = p0 + pltpu.roll(p1, n_slab - 1, 0) + pw
        return _dot(_gelu_tanh(h).astype(BF16), w2_ref[...])

    ok_ref[0, 0] = (one(xk_ref, pek_ref, wk1_ref, wk2_ref) * (ATT_SCALE * LOG2E)).astype(ok_ref.dtype)
    ov_ref[0, 0] = one(xv_ref, pev_ref, wv1_ref, wv2_ref).astype(ov_ref.dtype)


def _compress(xk, xv, pek, pev, wk1, wk2, wv1, wv2):
    bsz, grp, n_slab, wdt = xk.shape
    xs = pl.BlockSpec((1, 1, n_slab, wdt), lambda b, g: (b, g, 0, 0))
    full = lambda a: pl.BlockSpec(a.shape, lambda b, g: (0,) * a.ndim)
    osd = jax.ShapeDtypeStruct((bsz, grp, n_slab, NSA_DH), BF16)
    os_ = pl.BlockSpec((1, 1, n_slab, NSA_DH), lambda b, g: (b, g, 0, 0))
    return pl.pallas_call(
        _compress_kernel,
        grid=(bsz, grp),
        in_specs=[xs, xs, full(pek), full(pev), full(wk1), full(wk2), full(wv1), full(wv2)],
        out_specs=[os_, os_],
        out_shape=[osd, osd],
        compiler_params=_cparams(("parallel", "parallel")),
    )(xk, xv, pek, pev, wk1, wk2, wv1, wv2)


def _attn_kernel(nq_ref, qr_ref, kcmp_ref, vcmpt_ref, ks_ref, vst_ref, kw_ref, vwt_ref, ng_ref, ovt_ref,
                 o_ref, gt_scr, qaug_scr, *, seq):
    qn = ATT_Q
    grp = NSA_GROUP
    ln = grp * qn
    gw = grp * NSA_DH
    n_sel = seq // SEL_BLOCK
    top = min(SEL_TOPK, n_sel)
    groups = range(ATT_GROUPS)
    g_first = pl.program_id(1) * ATT_GROUPS
    q0 = pl.multiple_of(pl.program_id(2) * qn, qn)

    def stack_heads(x):
        return jnp.concatenate([x[:, r * NSA_DH:(r + 1) * NSA_DH] for r in range(grp)], axis=0)

    def add_per_head(s, bias):
        return jnp.concatenate([s[:, r * qn:(r + 1) * qn] + bias for r in range(grp)], axis=1)

    t_lane = q0 + (lax.broadcasted_iota(jnp.int32, (1, ln), 1) & (qn - 1))
    t_q = q0 + lax.broadcasted_iota(jnp.int32, (1, qn), 1)

    n_slab = kcmp_ref.shape[2]
    wlen = WINDOW + qn
    pad_col = jnp.where(lax.broadcasted_iota(jnp.int32, (ln, NSA_DH), 1) == 0, NEG, 0.0).astype(BF16)
    qs = [stack_heads(nq_ref[0, :, g * gw:(g + 1) * gw]) for g in groups]
    qrs = [stack_heads(qr_ref[0, :, g * gw:(g + 1) * gw]) for g in groups]

    def window_scores():
        return [_dot_nt(kw_ref[0, g, pl.ds(q0, wlen), :], jnp.concatenate([qrs[g], pad_col], axis=1))
                for g in groups]

    def diag_scores():
        return [_dot_nt(ks_ref[0, g, pl.ds(q0, qn), :][:, 0:NSA_DH], qrs[g]) for g in groups]

    def window_branch(sw):
        i_loc = lax.broadcasted_iota(jnp.int32, (qn, 1), 0)
        j_loc = lax.broadcasted_iota(jnp.int32, (1, qn), 1)
        win_lo = jnp.where(i_loc > j_loc, 0.0, NEG)
        win_hi = jnp.where(i_loc <= j_loc, 0.0, NEG)
        ow = []
        for g in groups:
            s = jnp.concatenate([add_per_head(sw[g][0:qn], win_lo), sw[g][qn:WINDOW],
                                 add_per_head(sw[g][WINDOW:wlen], win_hi)], axis=0)
            pw = jnp.exp2(s - _colmax(s))
            rw = _dot(vwt_ref[0, g, :, pl.ds(q0, wlen)], pw.astype(BF16))
            ow.append(rw[0:NSA_DH] / rw[NSA_DH:NSA_DH + 1])
        return ow

    def diag_tile(sd):
        kpos = q0 + lax.broadcasted_iota(jnp.int32, (qn, 1), 0)
        diag_bias = jnp.where(kpos <= t_q, 0.0, NEG)
        init = []
        for g in groups:
            s = add_per_head(sd[g], diag_bias)
            m0 = _colmax(s)
            init += [m0, _dot(vst_ref[0, g, :, pl.ds(q0, qn)], jnp.exp2(s - m0).astype(BF16))]
        return init

    gt_scr[...] = jax.nn.sigmoid(ng_ref[0].astype(F32)).T

    def gate_row(branch, g):
        return jnp.concatenate(
            [gt_scr[pl.ds(branch * NSA_HEADS + (g_first + g) * grp + r, 1), :] for r in range(grp)], axis=1)

    gate_sel = [gate_row(1, g) for g in groups]

    def before_sweep(n_spans):
        rows = min(n_spans * ATT_SPAN // CMP_STRIDE, n_slab)
        nb = min(n_spans * ATT_SPAN // SEL_BLOCK, n_sel)
        cend = lax.broadcasted_iota(jnp.int32, (rows, 1), 0) * CMP_STRIDE + (CMP_LEN - 1)
        cmp_bias = jnp.where(cend <= t_q, 0.0, NEG)
        ovt = ovt_ref[0:nb, 0:rows]
        sc = [_dot_nt(kcmp_ref[0, g, 0:rows, :], qs[g]) for g in groups]
        sw = window_scores()
        sd = diag_scores()
        imp, oc = [], []
        for g in groups:
            sm = add_per_head(sc[g], cmp_bias)
            e = jnp.exp2(sm - _colmax(sm))
            den = jnp.sum(e, axis=0, keepdims=True)
            p = e * jnp.where(t_lane >= CMP_LEN - 1, 1.0 / den, 0.0)
            oc.append(_dot(vcmpt_ref[0, g, :, 0:rows], p.astype(BF16)))
            psum = p[:, 0:qn]
            for r in range(1, grp):
                psum = psum + p[:, r * qn:(r + 1) * qn]
            hi, mid, lo = _split3(psum)
            imp.append(_dot(ovt, hi) + _dot(ovt, mid) + _dot(ovt, lo))
        ow = window_branch(sw)
        init = diag_tile(sd)
        out_cw = [gate_row(0, g) * oc[g] + gate_row(2, g) * ow[g] for g in groups]

        jidx = lax.broadcasted_iota(jnp.int32, (nb, qn), 0)
        tq = q0 + lax.broadcasted_iota(jnp.int32, (nb, qn), 1)
        cur = tq // SEL_BLOCK
        forced = (jidx == 0) | (jidx == cur) | (jidx == cur - 1)
        causal_blk = jidx * SEL_BLOCK <= tq
        rows8 = 8
        jrow = lax.broadcasted_iota(jnp.int32, (rows8, qn), 0)
        for g in groups:
            val = jnp.where(forced, -2.0, imp[g])
            val = jnp.where(causal_blk, val, -1.0)
            parts = [val[v * rows8:(v + 1) * rows8] for v in range(nb // rows8)]
            counts = [jnp.zeros((rows8, qn), F32) for _ in parts]
            for jp in range(nb):
                other = jnp.broadcast_to(val[jp:jp + 1, :], (rows8, qn))
                for v, part in enumerate(parts):
                    ge = jnp.where(other >= part, 1.0, 0.0)
                    gt = jnp.where(other > part, 1.0, 0.0)
                    if v * rows8 > jp:
                        before = ge
                    elif (v + 1) * rows8 <= jp:
                        before = gt
                    else:
                        before = jnp.where(jrow > jp - v * rows8, ge, gt)
                    counts[v] = counts[v] + before
            rank = jnp.concatenate(counts, axis=0)
            chosen = forced | (rank < float(max(top - 3, 0)))
            selb = jnp.where(chosen & causal_blk, 0.0, NEG)
            selb_t = jnp.concatenate([selb, jnp.zeros((LANES - nb, qn), F32)], axis=0).T
            selb_t = selb_t[:, 0:NSA_DH].astype(BF16)
            qaug_scr[g] = jnp.concatenate([qrs[g], jnp.concatenate([selb_t] * grp, axis=0)], axis=1)
        return tuple(out_cw) + tuple(init)

    n_variants = (seq + ATT_SPAN - 1) // ATT_SPAN
    staged = lax.switch(q0 // ATT_SPAN, [functools.partial(before_sweep, k + 1) for k in range(n_variants)])
    out_cw, init = staged[:ATT_GROUPS], staged[ATT_GROUPS:]

    kt_len = ATT_KT
    sub = kt_len // ATT_SUBTILES

    def sweep_block(k_base, n_sub, carry):
        state = list(carry)
        items = [(j, g) for j in range(n_sub) for g in groups]
        k0s = [pl.multiple_of(k_base + j * sub, sub) for j in range(n_sub)]
        score = lambda j, g: _dot_nt(ks_ref[0, g, pl.ds(k0s[j], sub), :], qaug_scr[g])
        ahead = min(ATT_LOOKAHEAD * ATT_GROUPS, len(items))
        scores = [score(*it) for it in items[:ahead]]
        for n, (j, g) in enumerate(items):
            s = scores[n]
            m_run, acc = state[2 * g], state[2 * g + 1]
            m_new = jnp.maximum(m_run, _colmax(s))
            alpha = jnp.exp2(m_run - m_new)
            pt = jnp.exp2(s - m_new).astype(BF16)
            state[2 * g + 1] = alpha * acc + _dot(vst_ref[0, g, :, pl.ds(k0s[j], sub)], pt)
            state[2 * g] = m_new
            if n + ahead < len(items):
                scores.append(score(*items[n + ahead]))
        return tuple(state)

    n_whole = q0 // kt_len
    carry = lax.fori_loop(0, n_whole, lambda kt, cy: sweep_block(kt * kt_len, ATT_SUBTILES, cy), tuple(init))
    tails = [functools.partial(sweep_block, n_whole * kt_len, r) for r in range(ATT_SUBTILES)]
    final = lax.switch((q0 - n_whole * kt_len) // sub, tails, carry)

    for g in groups:
        acc = final[2 * g + 1]
        out_t = out_cw[g] + gate_sel[g] * (acc[0:NSA_DH] / acc[NSA_DH:NSA_DH + 1])
        pieces = []
        for r2 in range(grp // 2):
            pair = jnp.concatenate([out_t[:, (2 * r2) * qn:(2 * r2 + 1) * qn],
                                    out_t[:, (2 * r2 + 1) * qn:(2 * r2 + 2) * qn]], axis=0)
            pieces.append(pair.T)
        o_ref[0, :, g * gw:(g + 1) * gw] = jnp.concatenate(pieces, axis=1).astype(o_ref.dtype)


def _attention(proj, q_rot, k_cmp, v_cmp_t, ks, vs_t, kw, vw_t, ov_t, bsz, seq):
    qn = ATT_Q
    gw = ATT_GROUPS * NSA_GROUP * NSA_DH
    n_slab = k_cmp.shape[2]
    n_sel = seq // SEL_BLOCK
    qspec = pl.BlockSpec((1, qn, gw), lambda b, g, i: (b, i, g))
    per_group = lambda a: pl.BlockSpec((1, ATT_GROUPS) + a.shape[2:], lambda b, g, i: (b, g, 0, 0))
    return pl.pallas_call(
        functools.partial(_attn_kernel, seq=seq),
        grid=(bsz, NSA_KV // ATT_GROUPS, seq // qn),
        in_specs=[pl.BlockSpec((1, qn, gw), lambda b, g, i: (b, i, P_NQ // gw + g)), qspec,
                  per_group(k_cmp), per_group(v_cmp_t), per_group(ks), per_group(vs_t),
                  per_group(kw), per_group(vw_t),
                  pl.BlockSpec((1, qn, LANES), lambda b, g, i: (b, i, P_NG // LANES)),
                  pl.BlockSpec((n_sel, n_slab), lambda b, g, i: (0, 0))],
        out_specs=qspec,
        out_shape=jax.ShapeDtypeStruct((bsz, seq, NSA_WIDTH), BF16),
        scratch_shapes=[pltpu.VMEM((LANES, qn), F32),
                        pltpu.VMEM((ATT_GROUPS, NSA_GROUP * qn, 2 * NSA_DH), BF16)],
        compiler_params=_cparams(("parallel", "parallel", "arbitrary")),
    )(proj, q_rot, k_cmp, v_cmp_t, ks, vs_t, kw, vw_t, proj, ov_t)


def _merge_kernel(x_ref, ya_ref, yb_ref, ga_ref, gb_ref, wa_ref, wb_ref, wo_ref, gn_ref, o_ref):
    ga = jax.nn.sigmoid(ga_ref[...].astype(F32))
    gb = jax.nn.sigmoid(gb_ref[...].astype(F32))
    merged = ga * _dot(ya_ref[...], wa_ref[...]) + gb * _dot(yb_ref[...], wb_ref[...])
    mo = _dot(merged.astype(BF16), wo_ref[...])
    o_ref[...] = x_ref[...] + _rms(mo, gn_ref[...])


def _merge(x, ya, yb, gates, wa, wb, wo, gn):
    n, d = x.shape
    tm = min(TOK_TILE, n)
    row = lambda c: pl.BlockSpec((tm, d), lambda i: (i, c))
    wsp = pl.BlockSpec((d, d), lambda i: (0, 0))
    return pl.pallas_call(
        _merge_kernel,
        grid=(n // tm,),
        in_specs=[row(0), row(0), row(0), row(P_GA // d), row(P_GB // d), wsp, wsp, wsp,
                  pl.BlockSpec((1, d), lambda i: (0, 0))],
        out_specs=row(0),
        out_shape=jax.ShapeDtypeStruct((n, d), F32),
        compiler_params=_cparams(("parallel",)),
    )(x, ya, yb, gates, gates, wa, wb, wo, gn)


def _mlp_kernel(h_ref, gpre_ref, wu_ref, wd_ref, gpost_ref, o_ref, v_scr, acc_scr):
    j = pl.program_id(1)

    @pl.when(j == 0)
    def _():
        v_scr[...] = _rms(h_ref[...], gpre_ref[...]).astype(BF16)
        acc_scr[...] = jnp.zeros_like(acc_scr)

    up = _dot(v_scr[...], wu_ref[...])
    act = jnp.square(jnp.maximum(up, 0.0))
    acc_scr[...] += _dot(act.astype(BF16), wd_ref[...])

    @pl.when(j == pl.num_programs(1) - 1)
    def _():
        o_ref[...] = h_ref[...] + _rms(acc_scr[...], gpost_ref[...])


def _mlp(h, gpre, wu, wd, gpost):
    n, d = h.shape
    dff = wu.shape[1]
    tm = min(2 * TOK_TILE, n)
    tf = 1024
    vec = pl.BlockSpec((1, d), lambda i, j: (0, 0))
    return pl.pallas_call(
        _mlp_kernel,
        grid=(n // tm, dff // tf),
        in_specs=[pl.BlockSpec((tm, d), lambda i, j: (i, 0)), vec,
                  pl.BlockSpec((d, tf), lambda i, j: (0, j)),
                  pl.BlockSpec((tf, d), lambda i, j: (j, 0)), vec],
        out_specs=pl.BlockSpec((tm, d), lambda i, j: (i, 0)),
        out_shape=jax.ShapeDtypeStruct((n, d), F32),
        scratch_shapes=[pltpu.VMEM((tm, d), BF16), pltpu.VMEM((tm, d), F32)],
        compiler_params=_cparams(("parallel", "arbitrary")),
    )(h, gpre, wu, wd, gpost)


def _ple_kernel(h_ref, p_ref, wp_ref, wg_ref, gn_ref, o_ref):
    h = h_ref[...]
    e = _dot(p_ref[...].astype(BF16), wp_ref[...]) * jax.nn.sigmoid(_dot(h.astype(BF16), wg_ref[...]))
    o_ref[...] = h + _rms(e, gn_ref[...])


def _ple(h, p, wp, wg, gn):
    n, d = h.shape
    pd = p.shape[1]
    tm = min(2 * TOK_TILE, n)
    return pl.pallas_call(
        _ple_kernel,
        grid=(n // tm,),
        in_specs=[pl.BlockSpec((tm, d), lambda i: (i, 0)),
                  pl.BlockSpec((tm, pd), lambda i: (i, 0)),
                  pl.BlockSpec((pd, d), lambda i: (0, 0)),
                  pl.BlockSpec((d, d), lambda i: (0, 0)),
                  pl.BlockSpec((1, d), lambda i: (0, 0))],
        out_specs=pl.BlockSpec((tm, d), lambda i: (i, 0)),
        out_shape=jax.ShapeDtypeStruct((n, d), F32),
        compiler_params=_cparams(("parallel",)),
    )(h, p, wp, wg, gn)


def _overlap_t(seq):
    n_slab = seq // CMP_STRIDE
    n_cmp = (seq - CMP_LEN) // CMP_STRIDE + 1
    n_sel = seq // SEL_BLOCK
    start = np.arange(n_slab) * CMP_STRIDE
    end = start + CMP_LEN - 1
    sel_start = np.arange(n_sel) * SEL_BLOCK
    ov = (start[None, :] < sel_start[:, None] + SEL_BLOCK) & (end[None, :] >= sel_start[:, None])
    ov = ov & (np.arange(n_slab)[None, :] < n_cmp)
    return jnp.asarray(ov.astype(np.float32), BF16)


def kernel(x, p, w_in, w_branch_a, w_branch_b, w_out, norm_pre_mix, norm_post_mix, norm_pre_mlp,
           norm_post_mlp, hg_lb_logits, hg_gnorm, cmp_pe_k, cmp_pe_v, cmp_wk1, cmp_wk2, cmp_wv1, cmp_wv2,
           w_up, w_down, w_ple, w_ple_gate, norm_ple):
    bsz, seq, d = x.shape
    depth = w_in.shape[0]
    assert depth == 1 and hg_lb_logits.shape[0] == 2 and d == D_MODEL
    assert seq % ATT_KT == 0 and seq >= WINDOW + ATT_Q and seq // SEL_BLOCK <= NSA_DH
    assert seq % HG_TOK == 0 and (HG_TOK // HG_CHUNK) % HG_GROUP == 0
    assert ATT_Q in (SEL_BLOCK, 2 * SEL_BLOCK)
    assert ATT_Q % (ATT_KT // ATT_SUBTILES) == 0
    n = bsz * seq
    x2 = x.reshape(n, d)
    row = lambda a: a.reshape(1, -1)

    w = w_in[0]
    s_hq, s_hf, s_hi, s_hg = 0, HG_WIDTH, 2 * HG_WIDTH, 3 * HG_WIDTH
    s_nq = 4 * HG_WIDTH
    s_ng = s_nq + NSA_WIDTH + 6 * NSA_KVW
    s_ga = s_ng + 3 * NSA_HEADS
    col = lambda a, b: w[:, a:b]
    w_all = jnp.concatenate(
        [col(s_ga, s_ga + 2 * D_MODEL), col(s_hq, s_hf), col(s_hi, s_hg), col(s_hg, s_nq), col(s_hf, s_hi),
         col(s_nq, s_ga), jnp.zeros((d, LANES - 3 * NSA_HEADS), F32)], axis=1).astype(BF16)
    assert w_all.shape[1] == P_WIDTH and P_WIDTH % (P_COL_TILES * LANES) == 0
    proj2 = _rms_matmul(x2, row(norm_pre_mix[0]), w_all, BF16, P_WIDTH // P_COL_TILES)
    proj = proj2.reshape(bsz, seq, P_WIDTH)

    y_a = _hgrn(proj, hg_lb_logits, row(hg_gnorm[0]), bsz, seq)

    tables = _rope_tables(seq)
    kvb = P_KV // NSA_KVW
    q_rot = _rope(proj, P_NQ // NSA_WIDTH, NSA_WIDTH, tables, ATT_SCALE * LOG2E, bsz, seq)
    ks_aug, vs_t = _kv_prep(proj, kvb + 2, kvb + 3, tables, 0, True, bsz, seq)
    kw_aug, vw_t = _kv_prep(proj, kvb + 4, kvb + 5, tables, WINDOW, False, bsz, seq)
    kv_slice = lambda i: proj[:, :, P_KV + i * NSA_KVW: P_KV + (i + 1) * NSA_KVW]
    slabs = lambda t: t.reshape(bsz, seq, NSA_KV, NSA_DH).transpose(0, 2, 1, 3).reshape(
        bsz, NSA_KV, seq // CMP_STRIDE, CMP_STRIDE * NSA_DH)
    k_cmp, v_cmp = _compress(
        slabs(kv_slice(0)), slabs(kv_slice(1)), cmp_pe_k[0].reshape(1, -1), cmp_pe_v[0].reshape(1, -1),
        cmp_wk1[0].astype(BF16), cmp_wk2[0].astype(BF16), cmp_wv1[0].astype(BF16), cmp_wv2[0].astype(BF16))
    y_b = _attention(proj, q_rot, k_cmp, v_cmp.transpose(0, 1, 3, 2), ks_aug, vs_t, kw_aug, vw_t,
                     _overlap_t(seq), bsz, seq)

    h1 = _merge(x2, y_a.reshape(n, -1), y_b.reshape(n, -1), proj2, w_branch_a[0].astype(BF16),
                w_branch_b[0].astype(BF16), w_out[0].astype(BF16), row(norm_post_mix[0]))
    h2 = _mlp(h1, row(norm_pre_mlp[0]), w_up[0].astype(BF16), w_down[0].astype(BF16), row(norm_post_mlp[0]))
    h3o = _ple(h2, p[0].reshape(n, -1), w_ple[0].astype(BF16), w_ple_gate[0].astype(BF16), row(norm_ple[0]))
    return h3o.reshape(bsz, seq, d)
```
